```python
import jax, jax.numpy as jnp
from jax import lax
import numpy as np

D_MODEL = 1024
BATCH = 8
SEQ = 2048
DEPTH = 1

HEAD_DIM = 64
HEADS_PER_GROUP = 8
ATTN_PATTERNS = ((128, 1), (512, 4), (2048, 16))
N_ATTN_GROUPS = len(ATTN_PATTERNS)
ATTN_WIDTH = N_ATTN_GROUPS * HEADS_PER_GROUP * HEAD_DIM
ATTN_OUT = HEADS_PER_GROUP * HEAD_DIM
ATTN_BLOCK = 128
ROPE_THETA = 10000.0
GMLP_CHUNK = 128
GMLP_GROUPS = 8
GMLP_WIDTH = D_MODEL
GMLP_GROUP_DIM = GMLP_WIDTH // GMLP_GROUPS
N_BRANCHES = 2
IN_WIDTH = 3 * ATTN_WIDTH + 2 * GMLP_WIDTH + N_BRANCHES * D_MODEL
D_FF = 2816
ALPHA = (2 * DEPTH) ** 0.25
BETA = (8 * DEPTH) ** -0.25
LN_EPS = 1e-5

kernel_name = 'hybrid_dilated_attn_gmlp_macaron_deepnorm'


def layer_norm(x, g, b):
    xf = x.astype(jnp.float32)
    mu = jnp.mean(xf, -1, keepdims=True)
    var = jnp.mean(jnp.square(xf - mu), -1, keepdims=True)
    y = (xf - mu) * lax.rsqrt(var + LN_EPS) * g.astype(jnp.float32) + b.astype(jnp.float32)
    return y.astype(x.dtype)


def swiglu_ffn(x, w_gate, w_up, w_down):
    return (jax.nn.silu(x @ w_gate) * (x @ w_up)) @ w_down


def rotary(t, cos, sin):
    tf = t.astype(jnp.float32)
    t1, t2 = jnp.split(tf, 2, axis=-1)
    c = cos[:, :, None, None, :]
    s = sin[:, :, None, None, :]
    return jnp.concatenate([t1 * c - t2 * s, t2 * c + t1 * s], axis=-1).astype(t.dtype)


def dilated_window_attention(q, k, v, window, dilation):
    b, s, h, dh = q.shape
    w = window // dilation
    sub_len = s // dilation
    n_blk = -(-sub_len // ATTN_BLOCK)
    pad = n_blk * ATTN_BLOCK - sub_len

    def to_blocks(t):
        t = t.reshape(b, sub_len, dilation, h, dh).transpose(0, 2, 1, 3, 4)
        t = jnp.pad(t, ((0, 0), (0, 0), (0, pad), (0, 0), (0, 0)))
        return t.reshape(b, dilation, n_blk, ATTN_BLOCK, h, dh)

    def with_prev(t):
        prev = jnp.pad(t, ((0, 0), (0, 0), (1, 0), (0, 0), (0, 0), (0, 0)))[:, :, :-1]
        return jnp.concatenate([prev, t], axis=3)

    qb = to_blocks(q)
    kc = with_prev(to_blocks(k))
    vc = with_prev(to_blocks(v))
    scores = jnp.einsum('brnqhd,brnkhd->brnhqk', qb, kc,
                        preferred_element_type=jnp.float32) * (dh ** -0.5)
    blk = np.arange(n_blk)[:, None, None]
    qi = np.arange(ATTN_BLOCK)[None, :, None]
    kj = np.arange(2 * ATTN_BLOCK)[None, None, :]
    dist = qi + ATTN_BLOCK - kj
    kpos = (blk - 1) * ATTN_BLOCK + kj
    mask = (dist >= 0) & (dist <= w) & (kpos >= 0)
    scores = jnp.where(jnp.asarray(mask)[None, None, :, None], scores, -jnp.inf)
    m = jnp.max(scores, axis=-1, keepdims=True)
    p = jnp.exp(scores - m)
    l = jnp.sum(p, axis=-1, keepdims=True)
    o = jnp.einsum('brnhqk,brnkhd->brnqhd', p / l, vc.astype(jnp.float32))
    lse = (m + jnp.log(l))[..., 0].transpose(0, 1, 2, 4, 3)

    def from_blocks(t):
        t = t.reshape((b, dilation, n_blk * ATTN_BLOCK) + t.shape[4:])[:, :, :sub_len]
        t = jnp.swapaxes(t, 1, 2)
        return t.reshape((b, s) + t.shape[3:])

    return from_blocks(o), from_blocks(lse)


def hybrid_mixer(h, cos, sin, w_in, b_gates, gmlp_ln_g, gmlp_ln_b, gmlp_w_s, gmlp_b_s,
                 w_attn_branch, w_gmlp_branch, w_out):
    b, s, _ = h.shape
    proj = h @ w_in
    qkv, z, g = jnp.split(proj, [3 * ATTN_WIDTH, 3 * ATTN_WIDTH + 2 * GMLP_WIDTH], axis=-1)

    qkv = qkv.reshape(b, s, 3, N_ATTN_GROUPS, HEADS_PER_GROUP, HEAD_DIM)
    q = rotary(qkv[:, :, 0], cos, sin)
    k = rotary(qkv[:, :, 1], cos, sin)
    v = qkv[:, :, 2]
    outs, lses = [], []
    for gi, (window, dilation) in enumerate(ATTN_PATTERNS):
        o, lse = dilated_window_attention(q[:, :, gi], k[:, :, gi], v[:, :, gi], window, dilation)
        outs.append(o)
        lses.append(lse)
    wts = jax.nn.softmax(jnp.stack(lses), axis=0)
    y_attn = jnp.sum(wts[..., None] * jnp.stack(outs), axis=0).reshape(b, s, ATTN_OUT).astype(h.dtype)

    u, vg = jnp.split(jax.nn.gelu(z, approximate=False), 2, axis=-1)
    vg = layer_norm(vg, gmlp_ln_g, gmlp_ln_b)
    n_chunk = s // GMLP_CHUNK
    vg = vg.reshape(b, n_chunk, GMLP_CHUNK, GMLP_GROUPS, GMLP_GROUP_DIM)
    w_s = gmlp_w_s * jnp.tril(jnp.ones((GMLP_CHUNK, GMLP_CHUNK), gmlp_w_s.dtype))
    mixed = jnp.einsum('gts,bnsgc->bntgc', w_s, vg) + gmlp_b_s.T[:, :, None]
    y_gmlp = u * mixed.reshape(b, s, GMLP_WIDTH)

    branches = jnp.stack([y_attn @ w_attn_branch, y_gmlp @ w_gmlp_branch], axis=2)
    gates = jax.nn.sigmoid(g.reshape(b, s, N_BRANCHES, D_MODEL) + b_gates.reshape(N_BRANCHES, D_MODEL))
    return jnp.sum(gates * branches, axis=2) @ w_out


def setup_inputs(seed: int = 0) -> dict:
    key = jax.random.key(seed)
    ks = jax.random.split(key, 32)

    def nrm(k, shape, scale):
        return jax.random.normal(k, shape, jnp.float32) * scale

    d_s = D_MODEL ** -0.5
    w_in = jnp.concatenate([
        nrm(ks[2], (DEPTH, D_MODEL, ATTN_WIDTH), d_s),
        nrm(ks[3], (DEPTH, D_MODEL, ATTN_WIDTH), d_s),
        nrm(ks[4], (DEPTH, D_MODEL, ATTN_WIDTH), BETA * d_s),
        nrm(ks[5], (DEPTH, D_MODEL, 2 * GMLP_WIDTH), d_s),
        nrm(ks[6], (DEPTH, D_MODEL, N_BRANCHES * D_MODEL), d_s),
    ], axis=-1)
    return {
        'x': jax.random.normal(ks[0], (BATCH, SEQ, D_MODEL), jnp.float32),
        'positions': jnp.broadcast_to(jnp.arange(SEQ, dtype=jnp.int32), (BATCH, SEQ)),
        'ffn1_w_gate': nrm(ks[7], (DEPTH, D_MODEL, D_FF), d_s),
        'ffn1_w_up': nrm(ks[8], (DEPTH, D_MODEL, D_FF), d_s),
        'ffn1_w_down': nrm(ks[9], (DEPTH, D_FF, D_MODEL), BETA * D_FF ** -0.5),
        'ln1_g': 1.0 + nrm(ks[10], (DEPTH, D_MODEL), 0.02),
        'ln1_b': nrm(ks[11], (DEPTH, D_MODEL), 0.02),
        'w_in': w_in,
        'b_gates': nrm(ks[12], (DEPTH, N_BRANCHES * D_MODEL), 0.02),
        'gmlp_ln_g': 1.0 + nrm(ks[13], (DEPTH, GMLP_WIDTH), 0.02),
        'gmlp_ln_b': nrm(ks[14], (DEPTH, GMLP_WIDTH), 0.02),
        'gmlp_w_s': nrm(ks[15], (DEPTH, GMLP_GROUPS, GMLP_CHUNK, GMLP_CHUNK), 0.5 * GMLP_CHUNK ** -0.5),
        'gmlp_b_s': 1.0 + nrm(ks[16], (DEPTH, GMLP_GROUPS, GMLP_CHUNK), 0.02),
        'w_attn_branch': nrm(ks[17], (DEPTH, ATTN_OUT, D_MODEL), BETA * ATTN_OUT ** -0.5),
        'w_gmlp_branch': nrm(ks[18], (DEPTH, GMLP_WIDTH, D_MODEL), BETA * GMLP_WIDTH ** -0.5),
        'w_out': nrm(ks[19], (DEPTH, D_MODEL, D_MODEL), BETA * d_s),
        'ln2_g': 1.0 + nrm(ks[20], (DEPTH, D_MODEL), 0.02),
        'ln2_b': nrm(ks[21], (DEPTH, D_MODEL), 0.02),
        'ffn2_w_gate': nrm(ks[22], (DEPTH, D_MODEL, D_FF), d_s),
        'ffn2_w_up': nrm(ks[23], (DEPTH, D_MODEL, D_FF), d_s),
        'ffn2_w_down': nrm(ks[24], (DEPTH, D_FF, D_MODEL), BETA * D_FF ** -0.5),
        'ln3_g': 1.0 + nrm(ks[25], (DEPTH, D_MODEL), 0.02),
        'ln3_b': nrm(ks[26], (DEPTH, D_MODEL), 0.02),
    }


def reference(x, positions, ffn1_w_gate, ffn1_w_up, ffn1_w_down, ln1_g, ln1_b, w_in, b_gates,
              gmlp_ln_g, gmlp_ln_b, gmlp_w_s, gmlp_b_s, w_attn_branch, w_gmlp_branch, w_out,
              ln2_g, ln2_b, ffn2_w_gate, ffn2_w_up, ffn2_w_down, ln3_g, ln3_b):
    inv_freq = ROPE_THETA ** (-jnp.arange(0, HEAD_DIM, 2, dtype=jnp.float32) / HEAD_DIM)
    ang = positions.astype(jnp.float32)[..., None] * inv_freq
    cos, sin = jnp.cos(ang), jnp.sin(ang)
    h = x
    for l in range(DEPTH):
        h = layer_norm(ALPHA * h + 0.5 * swiglu_ffn(h, ffn1_w_gate[l], ffn1_w_up[l], ffn1_w_down[l]),
                       ln1_g[l], ln1_b[l])
        mix = hybrid_mixer(h, cos, sin, w_in[l], b_gates[l], gmlp_ln_g[l], gmlp_ln_b[l], gmlp_w_s[l],
                           gmlp_b_s[l], w_attn_branch[l], w_gmlp_branch[l], w_out[l])
        h = layer_norm(ALPHA * h + mix, ln2_g[l], ln2_b[l])
        h = layer_norm(ALPHA * h + 0.5 * swiglu_ffn(h, ffn2_w_gate[l], ffn2_w_up[l], ffn2_w_down[l]),
                       ln3_g[l], ln3_b[l])
    return h
```

```python
import functools
import math

import jax
import jax.numpy as jnp
from jax import lax
from jax.experimental import pallas as pl
from jax.experimental.pallas import tpu as pltpu

D_MODEL = 1024
HEAD_DIM = 64
HEADS_PER_GROUP = 8
ATTN_PATTERNS = ((128, 1), (512, 4), (2048, 16))
N_ATTN_GROUPS = len(ATTN_PATTERNS)
GROUP_WIDTH = HEADS_PER_GROUP * HEAD_DIM
ATTN_WIDTH = N_ATTN_GROUPS * GROUP_WIDTH
ATTN_BLOCK = 128
ROPE_THETA = 10000.0
GMLP_CHUNK = 128
GMLP_GROUPS = 8
GMLP_WIDTH = D_MODEL
D_FF = 2816
DEPTH = 1
ALPHA = (2 * DEPTH) ** 0.25
LN_EPS = 1e-5

LANES = 128
VMEM_LIMIT_BYTES = 56 * 1024 * 1024

_Q0, _K0, _V0 = 0, ATTN_WIDTH, 2 * ATTN_WIDTH
_U0 = 3 * ATTN_WIDTH
_VG0 = _U0 + GMLP_WIDTH
_GA0 = _VG0 + GMLP_WIDTH
_GM0 = _GA0 + D_MODEL

BF16 = jnp.bfloat16
F32 = jnp.float32


def _layer_norm(x, g, b):
    mu = jnp.mean(x, axis=-1, keepdims=True)
    xc = x - mu
    var = jnp.mean(xc * xc, axis=-1, keepdims=True)
    return xc * lax.rsqrt(var + LN_EPS) * g + b


def _dot(a, b):
    return jnp.dot(a, b, preferred_element_type=F32)


def _swiglu(xb, wg_ref, wu_ref, wd_ref):
    g = _dot(xb, wg_ref[...])
    u = _dot(xb, wu_ref[...])
    a = g * jax.nn.sigmoid(g) * u
    return _dot(a.astype(BF16), wd_ref[...])


def _gelu(x):
    return 0.5 * x * (1.0 + lax.erf(x * math.sqrt(0.5)))


def _resident(shape):
    return pl.BlockSpec(shape, lambda *_: (0,) * len(shape), pipeline_mode=pl.Buffered(1))


def _rows(tm, width):
    return pl.BlockSpec((tm, width), lambda i: (i, 0))


def _params(n_axes):
    return pltpu.CompilerParams(dimension_semantics=("arbitrary",) * n_axes,
                                vmem_limit_bytes=VMEM_LIMIT_BYTES)


def _ffn_ln_kernel(x_ref, wg_ref, wu_ref, wd_ref, g_ref, b_ref, h_ref, hb_ref):
    x = x_ref[...]
    y = _swiglu(x.astype(BF16), wg_ref, wu_ref, wd_ref)
    h = _layer_norm(ALPHA * x + 0.5 * y, g_ref[...], b_ref[...])
    h_ref[...] = h
    hb_ref[...] = h.astype(BF16)


def _ffn_ln(x, wg, wu, wd, g, b, tm):
    t = x.shape[0]
    return pl.pallas_call(
        _ffn_ln_kernel,
        grid=(t // tm,),
        in_specs=[_rows(tm, D_MODEL), _resident((D_MODEL, D_FF)), _resident((D_MODEL, D_FF)),
                  _resident((D_FF, D_MODEL)), _resident((1, D_MODEL)), _resident((1, D_MODEL))],
        out_specs=[_rows(tm, D_MODEL), _rows(tm, D_MODEL)],
        out_shape=[jax.ShapeDtypeStruct((t, D_MODEL), F32), jax.ShapeDtypeStruct((t, D_MODEL), BF16)],
        compiler_params=_params(1),
        name="ffn_ln",
    )(x, wg, wu, wd, g, b)


def _mixer_in_kernel(hb_ref, pos_ref, invf_ref, w_ref, bg_ref, lng_ref, lnb_ref, ws_ref, bs_ref, wgb_ref,
                     q0_ref, k0_ref, v0_ref, q1_ref, k1_ref, v1_ref, q2_ref, k2_ref, v2_ref,
                     ga_ref, gm_ref):
    tm = hb_ref.shape[0]
    hb = hb_ref[...]

    ang = pos_ref[...].astype(F32) * invf_ref[...]
    cos = jnp.cos(ang)
    sin = jnp.sin(ang)
    lane = lax.broadcasted_iota(jnp.int32, (1, LANES), 1)
    first_half = (lane % HEAD_DIM) < (HEAD_DIM // 2)
    sin = jnp.where(first_half, -sin, sin)

    def rope(p, scale):
        c, s = cos * scale, sin * scale
        blocks = []
        for j in range(GROUP_WIDTH // LANES):
            x = p[:, j * LANES:(j + 1) * LANES]
            partner = jnp.where(first_half, pltpu.roll(x, LANES - HEAD_DIM // 2, 1),
                                pltpu.roll(x, HEAD_DIM // 2, 1))
            blocks.append(x * c + partner * s)
        return jnp.concatenate(blocks, axis=1)

    q_refs = (q0_ref, q1_ref, q2_ref)
    k_refs = (k0_ref, k1_ref, k2_ref)
    v_refs = (v0_ref, v1_ref, v2_ref)
    for gi in range(N_ATTN_GROUPS):
        c0 = gi * GROUP_WIDTH
        q = _dot(hb, w_ref[:, _Q0 + c0:_Q0 + c0 + GROUP_WIDTH])
        q_refs[gi][...] = rope(q, HEAD_DIM ** -0.5).astype(BF16)
        k = _dot(hb, w_ref[:, _K0 + c0:_K0 + c0 + GROUP_WIDTH])
        k_refs[gi][...] = rope(k, 1.0).astype(BF16)
        v = _dot(hb, w_ref[:, _V0 + c0:_V0 + c0 + GROUP_WIDTH])
        v_refs[gi][...] = v.astype(BF16)

    u = _gelu(_dot(hb, w_ref[:, _U0:_U0 + GMLP_WIDTH]))
    vg = _gelu(_dot(hb, w_ref[:, _VG0:_VG0 + GMLP_WIDTH]))
    vgn = _layer_norm(vg, lng_ref[...], lnb_ref[...]).astype(BF16)
    n_chunk = tm // GMLP_CHUNK
    gdim = GMLP_WIDTH // GMLP_GROUPS
    row = lax.broadcasted_iota(jnp.int32, (GMLP_CHUNK, GMLP_CHUNK), 0)
    col = lax.broadcasted_iota(jnp.int32, (GMLP_CHUNK, GMLP_CHUNK), 1)
    causal = col <= row
    mixed_cols = []
    for g in range(GMLP_GROUPS):
        ws = jnp.where(causal, ws_ref[g], 0.0).astype(BF16)
        rhs = jnp.concatenate(
            [vgn[c * GMLP_CHUNK:(c + 1) * GMLP_CHUNK, g * gdim:(g + 1) * gdim] for c in range(n_chunk)], axis=1)
        mixed_cols.append(_dot(ws, rhs))
    mixed = jnp.concatenate(
        [jnp.concatenate([mixed_cols[g][:, c * gdim:(c + 1) * gdim] for g in range(GMLP_GROUPS)], axis=1)
         + bs_ref[...] for c in range(n_chunk)], axis=0)
    y_gmlp = (u * mixed).astype(BF16)
    branch = _dot(y_gmlp, wgb_ref[...])
    gate_m = jax.nn.sigmoid(_dot(hb, w_ref[:, _GM0:_GM0 + D_MODEL]) + bg_ref[:, D_MODEL:])
    gm_ref[...] = (gate_m * branch).astype(BF16)
    gate_a = jax.nn.sigmoid(_dot(hb, w_ref[:, _GA0:_GA0 + D_MODEL]) + bg_ref[:, :D_MODEL])
    ga_ref[...] = gate_a.astype(BF16)


def _mixer_in(hb, pos, invf, w_in, b_gates, ln_g, ln_b, w_s, b_s, w_gb, tm):
    t = hb.shape[0]
    in_width = w_in.shape[1]
    qkv_shape = jax.ShapeDtypeStruct((t, GROUP_WIDTH), BF16)
    return pl.pallas_call(
        _mixer_in_kernel,
        grid=(t // tm,),
        in_specs=[_rows(tm, D_MODEL), _rows(tm, 1), _resident((1, LANES)), _resident((D_MODEL, in_width)),
                  _resident((1, 2 * D_MODEL)), _resident((1, GMLP_WIDTH)), _resident((1, GMLP_WIDTH)),
                  _resident((GMLP_GROUPS, GMLP_CHUNK, GMLP_CHUNK)), _resident((GMLP_CHUNK, GMLP_WIDTH)),
                  _resident((GMLP_WIDTH, D_MODEL))],
        out_specs=[_rows(tm, GROUP_WIDTH)] * 9 + [_rows(tm, D_MODEL), _rows(tm, D_MODEL)],
        out_shape=[qkv_shape] * 9 + [jax.ShapeDtypeStruct((t, D_MODEL), BF16)] * 2,
        compiler_params=_params(1),
        name="mixer_in",
    )(hb, pos, invf, w_in, b_gates, ln_g, ln_b, w_s, b_s, w_gb)


def _attn_kernel(q_ref, kc_ref, kp_ref, vc_ref, vp_ref, o_ref, lse_ref, *, has_prev):
    blk = pl.program_id(2)
    row = lax.broadcasted_iota(jnp.int32, (ATTN_BLOCK, ATTN_BLOCK), 0)
    col = lax.broadcasted_iota(jnp.int32, (ATTN_BLOCK, ATTN_BLOCK), 1)
    mask_cur = col <= row
    mask_prev = jnp.logical_and(col >= row, blk > 0)
    low_head = col < HEAD_DIM
    contract_last = (((1,), (1,)), ((), ()))
    lse_tile = jnp.zeros((ATTN_BLOCK, LANES), F32)
    for pair in range(GROUP_WIDTH // LANES):
        sl = slice(pair * LANES, (pair + 1) * LANES)
        q = q_ref[:, sl]
        kc, vc = kc_ref[:, sl], vc_ref[:, sl]
        if has_prev:
            kp, vp = kp_ref[:, sl], vp_ref[:, sl]
        outs = []
        for hh in range(2):
            head_lanes = low_head if hh == 0 else jnp.logical_not(low_head)
            qh = jnp.where(head_lanes, q, jnp.zeros_like(q))
            s_c = jnp.where(mask_cur, lax.dot_general(qh, kc, contract_last, preferred_element_type=F32), -jnp.inf)
            m = jnp.max(s_c, axis=1, keepdims=True)
            if has_prev:
                s_p = jnp.where(mask_prev, lax.dot_general(qh, kp, contract_last, preferred_element_type=F32),
                                -jnp.inf)
                m = jnp.maximum(m, jnp.max(s_p, axis=1, keepdims=True))
            p_c = jnp.exp(s_c - m)
            l = jnp.sum(p_c, axis=1, keepdims=True)
            o = _dot(p_c.astype(BF16), vc)
            if has_prev:
                p_p = jnp.exp(s_p - m)
                l = l + jnp.sum(p_p, axis=1, keepdims=True)
                o = o + _dot(p_p.astype(BF16), vp)
            outs.append(o / l)
            lse_tile = jnp.where(col == 2 * pair + hh, m + jnp.log(l), lse_tile)
        o_ref[:, sl] = jnp.where(low_head, outs[0], outs[1]).astype(BF16)
    lse_ref[...] = lse_tile


def _attention(q, k, v):
    b, d, l, _ = q.shape
    n_blk = l // ATTN_BLOCK
    has_prev = n_blk > 1
    cur = pl.BlockSpec((None, None, ATTN_BLOCK, GROUP_WIDTH), lambda bi, r, n: (bi, r, n, 0))
    prev = pl.BlockSpec((None, None, ATTN_BLOCK, GROUP_WIDTH), lambda bi, r, n: (bi, r, jnp.maximum(n - 1, 0), 0))
    lse_spec = pl.BlockSpec((None, None, ATTN_BLOCK, LANES), lambda bi, r, n: (bi, r, n, 0))
    return pl.pallas_call(
        functools.partial(_attn_kernel, has_prev=has_prev),
        grid=(b, d, n_blk),
        in_specs=[cur, cur, prev, cur, prev],
        out_specs=[cur, lse_spec],
        out_shape=[jax.ShapeDtypeStruct(q.shape, BF16), jax.ShapeDtypeStruct((b, d, l, LANES), F32)],
        compiler_params=_params(3),
        name=f"band_attention_d{d}",
    )(q, k, k, v, v)


def _mixer_out_kernel(o0_ref, o1_ref, o2_ref, l0_ref, l1_ref, l2_ref, ga_ref, gm_ref, h1_ref, expand_ref,
                      wab_ref, wout_ref, g2_ref, b2_ref, wg_ref, wu_ref, wd_ref, g3_ref, b3_ref, out_ref):
    lses = (l0_ref[...], l1_ref[...], l2_ref[...])
    m = jnp.maximum(jnp.maximum(lses[0], lses[1]), lses[2])
    es = [jnp.exp(x - m) for x in lses]
    inv_den = 1.0 / (es[0] + es[1] + es[2])
    y = None
    for e, o_ref in zip(es, (o0_ref, o1_ref, o2_ref)):
        w = e * inv_den
        w_hi = w.astype(BF16)
        w_lo = (w - w_hi.astype(F32)).astype(BF16)
        w_wide = _dot(w_hi, expand_ref[...]) + _dot(w_lo, expand_ref[...])
        term = w_wide * o_ref[...].astype(F32)
        y = term if y is None else y + term
    branch_a = _dot(y.astype(BF16), wab_ref[...])
    merged = ga_ref[...].astype(F32) * branch_a + gm_ref[...].astype(F32)
    mix = _dot(merged.astype(BF16), wout_ref[...])
    h2 = _layer_norm(ALPHA * h1_ref[...] + mix, g2_ref[...], b2_ref[...])
    ffn = _swiglu(h2.astype(BF16), wg_ref, wu_ref, wd_ref)
    out_ref[...] = _layer_norm(ALPHA * h2 + 0.5 * ffn, g3_ref[...], b3_ref[...])


def _mixer_out(os, lses, ga, gm, h1, expand, wab, wout, g2, b2, wg, wu, wd, g3, b3, tm):
    t = h1.shape[0]
    return pl.pallas_call(
        _mixer_out_kernel,
        grid=(t // tm,),
        in_specs=[_rows(tm, GROUP_WIDTH)] * 3 + [_rows(tm, LANES)] * 3
                 + [_rows(tm, D_MODEL)] * 3
                 + [_resident((LANES, GROUP_WIDTH)), _resident((GROUP_WIDTH, D_MODEL)),
                    _resident((D_MODEL, D_MODEL)), _resident((1, D_MODEL)), _resident((1, D_MODEL)),
                    _resident((D_MODEL, D_FF)), _resident((D_MODEL, D_FF)), _resident((D_FF, D_MODEL)),
                    _resident((1, D_MODEL)), _resident((1, D_MODEL))],
        out_specs=_rows(tm, D_MODEL),
        out_shape=jax.ShapeDtypeStruct((t, D_MODEL), F32),
        compiler_params=_params(1),
        name="mixer_out_ffn",
    )(*os, *lses, ga, gm, h1, expand, wab, wout, g2, b2, wg, wu, wd, g3, b3)


def _to_residue_major(x, b, s, d):
    return x.reshape(b, s // d, d, x.shape[-1]).transpose(0, 2, 1, 3)


def _from_residue_major(x, b, s):
    return x.transpose(0, 2, 1, 3).reshape(b * s, x.shape[-1])


def kernel(x, positions, ffn1_w_gate, ffn1_w_up, ffn1_w_down, ln1_g, ln1_b, w_in, b_gates, gmlp_ln_g, gmlp_ln_b, gmlp_w_s, gmlp_b_s, w_attn_branch, w_gmlp_branch, w_out, ln2_g, ln2_b, ffn2_w_gate, ffn2_w_up, ffn2_w_down, ln3_g, ln3_b):
    b, s, d_model = x.shape
    assert d_model == D_MODEL and ln1_g.shape[0] == DEPTH == 1
    t = b * s
    tm = 512
    assert t % tm == 0 and tm % GMLP_CHUNK == 0 and s % tm == 0

    lane = jnp.arange(LANES) % (HEAD_DIM // 2)
    invf = (ROPE_THETA ** (-(2 * lane).astype(F32) / HEAD_DIM)).reshape(1, LANES)
    head_of_lane = jnp.arange(GROUP_WIDTH) // HEAD_DIM
    expand = (jnp.arange(LANES)[:, None] == head_of_lane[None, :]).astype(BF16)
    b_s = jnp.repeat(gmlp_b_s[0].T, GMLP_WIDTH // GMLP_GROUPS, axis=1)

    h = x.reshape(t, D_MODEL)
    h1, h1b = _ffn_ln(h, ffn1_w_gate[0].astype(BF16), ffn1_w_up[0].astype(BF16), ffn1_w_down[0].astype(BF16),
                      ln1_g, ln1_b, tm)
    outs = _mixer_in(h1b, positions.reshape(t, 1), invf, w_in[0].astype(BF16), b_gates, gmlp_ln_g, gmlp_ln_b,
                     gmlp_w_s[0], b_s, w_gmlp_branch[0].astype(BF16), tm)
    ga, gm = outs[9], outs[10]
    os, lses = [], []
    for gi, (_, dil) in enumerate(ATTN_PATTERNS):
        q, k, v = (_to_residue_major(a, b, s, dil) for a in outs[3 * gi:3 * gi + 3])
        o, lse = _attention(q, k, v)
        os.append(_from_residue_major(o, b, s))
        lses.append(_from_residue_major(lse, b, s))
    out = _mixer_out(os, lses, ga, gm, h1, expand, w_attn_branch[0].astype(BF16), w_out[0].astype(BF16),
                     ln2_g, ln2_b, ffn2_w_gate[0].astype(BF16), ffn2_w_up[0].astype(BF16),
                     ffn2_w_down[0].astype(BF16), ln3_g, ln3_b, tm)
    return out.reshape(b, s, D_MODEL)
```

```python
import functools
import math

import jax
import jax.numpy as jnp
from jax import lax
from jax.experimental import pallas as pl
from jax.experimental.pallas import tpu as pltpu

D_MODEL = 1024
HEAD_DIM = 64
HEADS_PER_GROUP = 8
ATTN_PATTERNS = ((128, 1), (512, 4), (2048, 16))
N_ATTN_GROUPS = len(ATTN_PATTERNS)
GROUP_WIDTH = HEADS_PER_GROUP * HEAD_DIM
ATTN_WIDTH = N_ATTN_GROUPS * GROUP_WIDTH
ATTN_BLOCK = 128
ROPE_THETA = 10000.0
GMLP_CHUNK = 128
GMLP_GROUPS = 8
GMLP_WIDTH = D_MODEL
D_FF = 2816
DEPTH = 1
ALPHA = (2 * DEPTH) ** 0.25
LN_EPS = 1e-5

LANES = 128
VMEM_LIMIT_BYTES = 56 * 1024 * 1024

_Q0, _K0, _V0 = 0, ATTN_WIDTH, 2 * ATTN_WIDTH
_U0 = 3 * ATTN_WIDTH
_VG0 = _U0 + GMLP_WIDTH
_GA0 = _VG0 + GMLP_WIDTH
_GM0 = _GA0 + D_MODEL

BF16 = jnp.bfloat16
F32 = jnp.float32


def _layer_norm(x, g, b):
    mu = jnp.mean(x, axis=-1, keepdims=True)
    xc = x - mu
    var = jnp.mean(xc * xc, axis=-1, keepdims=True)
    return xc * lax.rsqrt(var + LN_EPS) * g + b


def _dot(a, b):
    return jnp.dot(a, b, preferred_element_type=F32)


def _swiglu(xb, wg_ref, wu_ref, wd_ref):
    g = _dot(xb, wg_ref[...])
    u = _dot(xb, wu_ref[...])
    a = g * jax.nn.sigmoid(g) * u
    return _dot(a.astype(BF16), wd_ref[...])


def _gelu(x):
    return 0.5 * x * (1.0 + lax.erf(x * math.sqrt(0.5)))


def _resident(shape):
    return pl.BlockSpec(shape, lambda *_: (0,) * len(shape), pipeline_mode=pl.Buffered(1))


def _rows(tm, width):
    return pl.BlockSpec((tm, width), lambda i: (i, 0))


def _params(n_axes):
    return pltpu.CompilerParams(dimension_semantics=("arbitrary",) * n_axes,
                                vmem_limit_bytes=VMEM_LIMIT_BYTES)


def _ffn_ln_kernel(x_ref, wg_ref, wu_ref, wd_ref, g_ref, b_ref, h_ref, hb_ref):
    x = x_ref[...]
    y = _swiglu(x.astype(BF16), wg_ref, wu_ref, wd_ref)
    h = _layer_norm(ALPHA * x + 0.5 * y, g_ref[...], b_ref[...])
    h_ref[...] = h
    hb_ref[...] = h.astype(BF16)


def _ffn_ln(x, wg, wu, wd, g, b, tm):
    t = x.shape[0]
    return pl.pallas_call(
        _ffn_ln_kernel,
        grid=(t // tm,),
        in_specs=[_rows(tm, D_MODEL), _resident((D_MODEL, D_FF)), _resident((D_MODEL, D_FF)),
                  _resident((D_FF, D_MODEL)), _resident((1, D_MODEL)), _resident((1, D_MODEL))],
        out_specs=[_rows(tm, D_MODEL), _rows(tm, D_MODEL)],
        out_shape=[jax.ShapeDtypeStruct((t, D_MODEL), F32), jax.ShapeDtypeStruct((t, D_MODEL), BF16)],
        compiler_params=_params(1),
        name="ffn_ln",
    )(x, wg, wu, wd, g, b)


def _mixer_in_kernel(hb_ref, pos_ref, invf_ref, w_ref, bg_ref, lng_ref, lnb_ref, ws_ref, bs_ref, wgb_ref,
                     q0_ref, k0_ref, v0_ref, q1_ref, k1_ref, v1_ref, q2_ref, k2_ref, v2_ref,
                     ga_ref, gm_ref):
    tm = hb_ref.shape[0]
    hb = hb_ref[...]

    ang = pos_ref[...].astype(F32) * invf_ref[...]
    cos = jnp.cos(ang)
    sin = jnp.sin(ang)
    lane = lax.broadcasted_iota(jnp.int32, (1, LANES), 1)
    first_half = (lane % HEAD_DIM) < (HEAD_DIM // 2)
    sin = jnp.where(first_half, -sin, sin)

    def rope(p, scale):
        c, s = cos * scale, sin * scale
        blocks = []
        for j in range(GROUP_WIDTH // LANES):
            x = p[:, j * LANES:(j + 1) * LANES]
            partner = jnp.where(first_half, pltpu.roll(x, LANES - HEAD_DIM // 2, 1),
                                pltpu.roll(x, HEAD_DIM // 2, 1))
            blocks.append(x * c + partner * s)
        return jnp.concatenate(blocks, axis=1)

    q_refs = (q0_ref, q1_ref, q2_ref)
    k_refs = (k0_ref, k1_ref, k2_ref)
    v_refs = (v0_ref, v1_ref, v2_ref)
    for gi in range(N_ATTN_GROUPS):
        c0 = gi * GROUP_WIDTH
        q = _dot(hb, w_ref[:, _Q0 + c0:_Q0 + c0 + GROUP_WIDTH])
        q_refs[gi][...] = rope(q, HEAD_DIM ** -0.5).astype(BF16)
        k = _dot(hb, w_ref[:, _K0 + c0:_K0 + c0 + GROUP_WIDTH])
        k_refs[gi][...] = rope(k, 1.0).astype(BF16)
        v = _dot(hb, w_ref[:, _V0 + c0:_V0 + c0 + GROUP_WIDTH])
        v_refs[gi][...] = v.astype(BF16)

    u = _gelu(_dot(hb, w_ref[:, _U0:_U0 + GMLP_WIDTH]))
    vg = _gelu(_dot(hb, w_ref[:, _VG0:_VG0 + GMLP_WIDTH]))
    vgn = _layer_norm(vg, lng_ref[...], lnb_ref[...]).astype(BF16)
    n_chunk = tm // GMLP_CHUNK
    gdim = GMLP_WIDTH // GMLP_GROUPS
    row = lax.broadcasted_iota(jnp.int32, (GMLP_CHUNK, GMLP_CHUNK), 0)
    col = lax.broadcasted_iota(jnp.int32, (GMLP_CHUNK, GMLP_CHUNK), 1)
    causal = col <= row
    mixed_cols = []
    for g in range(GMLP_GROUPS):
        ws = jnp.where(causal, ws_ref[g], 0.0).astype(BF16)
        rhs = jnp.concatenate(
            [vgn[c * GMLP_CHUNK:(c + 1) * GMLP_CHUNK, g * gdim:(g + 1) * gdim] for c in range(n_chunk)], axis=1)
        mixed_cols.append(_dot(ws, rhs))
    mixed = jnp.concatenate(
        [jnp.concatenate([mixed_cols[g][:, c * gdim:(c + 1) * gdim] for g in range(GMLP_GROUPS)], axis=1)
         + bs_ref[...] for c in range(n_chunk)], axis=0)
    y_gmlp = (u * mixed).astype(BF16)
    branch = _dot(y_gmlp, wgb_ref[...])
    gate_m = jax.nn.sigmoid(_dot(hb, w_ref[:, _GM0:_GM0 + D_MODEL]) + bg_ref[:, D_MODEL:])
    gm_ref[...] = (gate_m * branch).astype(BF16)
    gate_a = jax.nn.sigmoid(_dot(hb, w_ref[:, _GA0:_GA0 + D_MODEL]) + bg_ref[:, :D_MODEL])
    ga_ref[...] = gate_a.astype(BF16)


def _mixer_in(hb, pos, invf, w_in, b_gates, ln_g, ln_b, w_s, b_s, w_gb, tm):
    t = hb.shape[0]
    in_width = w_in.shape[1]
    qkv_shape = jax.ShapeDtypeStruct((t, GROUP_WIDTH), BF16)
    return pl.pallas_call(
        _mixer_in_kernel,
        grid=(t // tm,),
        in_specs=[_rows(tm, D_MODEL), _rows(tm, 1), _resident((1, LANES)), _resident((D_MODEL, in_width)),
                  _resident((1, 2 * D_MODEL)), _resident((1, GMLP_WIDTH)), _resident((1, GMLP_WIDTH)),
                  _resident((GMLP_GROUPS, GMLP_CHUNK, GMLP_CHUNK)), _resident((GMLP_CHUNK, GMLP_WIDTH)),
                  _resident((GMLP_WIDTH, D_MODEL))],
        out_specs=[_rows(tm, GROUP_WIDTH)] * 9 + [_rows(tm, D_MODEL), _rows(tm, D_MODEL)],
        out_shape=[qkv_shape] * 9 + [jax.ShapeDtypeStruct((t, D_MODEL), BF16)] * 2,
        compiler_params=_params(1),
        name="mixer_in",
    )(hb, pos, invf, w_in, b_gates, ln_g, ln_b, w_s, b_s, w_gb)


def _attn_kernel(q_ref, k_ref, v_ref, o_ref, lse_ref, *, n_blk):
    n_items = q_ref.shape[0] // ATTN_BLOCK
    n_pairs = GROUP_WIDTH // LANES
    has_prev = n_blk > 1
    row = lax.broadcasted_iota(jnp.int32, (2 * ATTN_BLOCK, ATTN_BLOCK), 0) % ATTN_BLOCK
    col = lax.broadcasted_iota(jnp.int32, (2 * ATTN_BLOCK, ATTN_BLOCK), 1)
    neg_inf = jnp.float32(-jnp.inf)
    bias_cur = jnp.where(col <= row, 0.0, neg_inf)
    bias_prev = jnp.where(col >= row, 0.0, neg_inf)
    lane = lax.broadcasted_iota(jnp.int32, (ATTN_BLOCK, LANES), 1)
    low_head = lane < HEAD_DIM
    ones = jnp.ones((2 * ATTN_BLOCK if has_prev else ATTN_BLOCK, LANES), BF16)
    contract_last = (((1,), (1,)), ((), ()))

    def item(j, carry):
        r0 = pl.multiple_of(j * ATTN_BLOCK, ATTN_BLOCK)
        if has_prev:
            p0 = pl.multiple_of(jnp.maximum(j - 1, 0) * ATTN_BLOCK, ATTN_BLOCK)
            prev_bias = bias_prev + jnp.where(j % n_blk > 0, 0.0, neg_inf)
        scores = []
        for pair in range(n_pairs):
            sl = slice(pair * LANES, (pair + 1) * LANES)
            q = q_ref[pl.ds(r0, ATTN_BLOCK), sl]
            zero = jnp.zeros_like(q)
            q2 = jnp.concatenate([jnp.where(low_head, q, zero), jnp.where(low_head, zero, q)], axis=0)
            keys = k_ref[pl.ds(r0, ATTN_BLOCK), sl]
            if has_prev:
                keys = jnp.concatenate([k_ref[pl.ds(p0, ATTN_BLOCK), sl], keys], axis=0)
            s = lax.dot_general(q2, keys, contract_last, preferred_element_type=F32)
            if has_prev:
                s = s + jnp.concatenate([prev_bias, bias_cur], axis=1)
            else:
                s = s + bias_cur
            scores.append(s)
        maxes = [jnp.max(s, axis=1, keepdims=True) for s in scores]
        probs = [jnp.exp(s - m).astype(BF16) for s, m in zip(scores, maxes)]
        lse_tile = jnp.zeros((ATTN_BLOCK, LANES), F32)
        for pair in range(n_pairs):
            sl = slice(pair * LANES, (pair + 1) * LANES)
            vals = v_ref[pl.ds(r0, ATTN_BLOCK), sl]
            if has_prev:
                vals = jnp.concatenate([v_ref[pl.ds(p0, ATTN_BLOCK), sl], vals], axis=0)
            o2 = _dot(probs[pair], vals)
            l2 = _dot(probs[pair], ones)
            o2 = o2 / l2
            lse2 = maxes[pair] + jnp.log(l2)
            o_ref[pl.ds(r0, ATTN_BLOCK), sl] = jnp.where(low_head, o2[:ATTN_BLOCK], o2[ATTN_BLOCK:]).astype(BF16)
            lse_tile = jnp.where(lane == 2 * pair, lse2[:ATTN_BLOCK], lse_tile)
            lse_tile = jnp.where(lane == 2 * pair + 1, lse2[ATTN_BLOCK:], lse_tile)
        lse_ref[pl.ds(r0, ATTN_BLOCK), :] = lse_tile
        return carry

    lax.fori_loop(0, n_items, item, 0)


def _attention(q, k, v):
    b, d, l, _ = q.shape
    n_blk = l // ATTN_BLOCK
    seq = pl.BlockSpec((None, d * l, GROUP_WIDTH), lambda bi: (bi, 0, 0))
    lse_spec = pl.BlockSpec((None, d * l, LANES), lambda bi: (bi, 0, 0))
    q, k, v = (a.reshape(b, d * l, GROUP_WIDTH) for a in (q, k, v))
    o, lse = pl.pallas_call(
        functools.partial(_attn_kernel, n_blk=n_blk),
        grid=(b,),
        in_specs=[seq, seq, seq],
        out_specs=[seq, lse_spec],
        out_shape=[jax.ShapeDtypeStruct(q.shape, BF16), jax.ShapeDtypeStruct((b, d * l, LANES), F32)],
        compiler_params=_params(1),
        name=f"band_attention_d{d}",
    )(q, k, v)
    return o.reshape(b, d, l, GROUP_WIDTH), lse.reshape(b, d, l, LANES)


def _mixer_out_kernel(o0_ref, o1_ref, o2_ref, l0_ref, l1_ref, l2_ref, ga_ref, gm_ref, h1_ref, expand_ref,
                      wab_ref, wout_ref, g2_ref, b2_ref, wg_ref, wu_ref, wd_ref, g3_ref, b3_ref, out_ref):
    lses = (l0_ref[...], l1_ref[...], l2_ref[...])
    m = jnp.maximum(jnp.maximum(lses[0], lses[1]), lses[2])
    es = [jnp.exp(x - m) for x in lses]
    inv_den = 1.0 / (es[0] + es[1] + es[2])
    y = None
    for e, o_ref in zip(es, (o0_ref, o1_ref, o2_ref)):
        w = e * inv_den
        w_hi = w.astype(BF16)
        w_lo = (w - w_hi.astype(F32)).astype(BF16)
        w_wide = _dot(w_hi, expand_ref[...]) + _dot(w_lo, expand_ref[...])
        term = w_wide * o_ref[...].astype(F32)
        y = term if y is None else y + term
    branch_a = _dot(y.astype(BF16), wab_ref[...])
    merged = ga_ref[...].astype(F32) * branch_a + gm_ref[...].astype(F32)
    mix = _dot(merged.astype(BF16), wout_ref[...])
    h2 = _layer_norm(ALPHA * h1_ref[...] + mix, g2_ref[...], b2_ref[...])
    ffn = _swiglu(h2.astype(BF16), wg_ref, wu_ref, wd_ref)
    out_ref[...] = _layer_norm(ALPHA * h2 + 0.5 * ffn, g3_ref[...], b3_ref[...])


def _mixer_out(os, lses, ga, gm, h1, expand, wab, wout, g2, b2, wg, wu, wd, g3, b3, tm):
    t = h1.shape[0]
    return pl.pallas_call(
        _mixer_out_kernel,
        grid=(t // tm,),
        in_specs=[_rows(tm, GROUP_WIDTH)] * 3 + [_rows(tm, LANES)] * 3
                 + [_rows(tm, D_MODEL)] * 3
                 + [_resident((LANES, GROUP_WIDTH)), _resident((GROUP_WIDTH, D_MODEL)),
                    _resident((D_MODEL, D_MODEL)), _resident((1, D_MODEL)), _resident((1, D_MODEL)),
                    _resident((D_MODEL, D_FF)), _resident((D_MODEL, D_FF)), _resident((D_FF, D_MODEL)),
                    _resident((1, D_MODEL)), _resident((1, D_MODEL))],
        out_specs=_rows(tm, D_MODEL),
        out_shape=jax.ShapeDtypeStruct((t, D_MODEL), F32),
        compiler_params=_params(1),
        name="mixer_out_ffn",
    )(*os, *lses, ga, gm, h1, expand, wab, wout, g2, b2, wg, wu, wd, g3, b3)


def _to_residue_major(x, b, s, d):
    return x.reshape(b, s // d, d, x.shape[-1]).transpose(0, 2, 1, 3)


def _from_residue_major(x, b, s):
    return x.transpose(0, 2, 1, 3).reshape(b * s, x.shape[-1])


def kernel(x, positions, ffn1_w_gate, ffn1_w_up, ffn1_w_down, ln1_g, ln1_b, w_in, b_gates, gmlp_ln_g, gmlp_ln_b, gmlp_w_s, gmlp_b_s, w_attn_branch, w_gmlp_branch, w_out, ln2_g, ln2_b, ffn2_w_gate, ffn2_w_up, ffn2_w_down, ln3_g, ln3_b):
    b, s, d_model = x.shape
    assert d_model == D_MODEL and ln1_g.shape[0] == DEPTH == 1
    t = b * s
    tm = 512
    assert t % tm == 0 and tm % GMLP_CHUNK == 0 and s % tm == 0

    lane = jnp.arange(LANES) % (HEAD_DIM // 2)
    invf = (ROPE_THETA ** (-(2 * lane).astype(F32) / HEAD_DIM)).reshape(1, LANES)
    head_of_lane = jnp.arange(GROUP_WIDTH) // HEAD_DIM
    expand = (jnp.arange(LANES)[:, None] == head_of_lane[None, :]).astype(BF16)
    b_s = jnp.repeat(gmlp_b_s[0].T, GMLP_WIDTH // GMLP_GROUPS, axis=1)

    h = x.reshape(t, D_MODEL)
    h1, h1b = _ffn_ln(h, ffn1_w_gate[0].astype(BF16), ffn1_w_up[0].astype(BF16), ffn1_w_down[0].astype(BF16),
                      ln1_g, ln1_b, tm)
    outs = _mixer_in(h1b, positions.reshape(t, 1), invf, w_in[0].astype(BF16), b_gates, gmlp_ln_g, gmlp_ln_b,
                     gmlp_w_s[0], b_s, w_gmlp_branch[0].astype(BF16), tm)
    ga, gm = outs[9], outs[10]
    os, lses = [], []
    for gi, (_, dil) in enumerate(ATTN_PATTERNS):
        q, k, v = (_to_residue_major(a, b, s, dil) for a in outs[3 * gi:3 * gi + 3])
        o, lse = _attention(q, k, v)
        os.append(_from_residue_major(o, b, s))
        lses.append(_from_residue_major(lse, b, s))
    out = _mixer_out(os, lses, ga, gm, h1, expand, w_attn_branch[0].astype(BF16), w_out[0].astype(BF16),
                     ln2_g, ln2_b, ffn2_w_gate[0].astype(BF16), ffn2_w_up[0].astype(BF16),
                     ffn2_w_down[0].astype(BF16), ln3_g, ln3_b, tm)
    return out.reshape(b, s, D_MODEL)
```

```python
import functools
import math

import jax
import jax.numpy as jnp
from jax import lax
from jax.experimental import pallas as pl
from jax.experimental.pallas import tpu as pltpu

D_MODEL = 1024
HEAD_DIM = 64
HEADS_PER_GROUP = 8
ATTN_PATTERNS = ((128, 1), (512, 4), (2048, 16))
N_ATTN_GROUPS = len(ATTN_PATTERNS)
GROUP_WIDTH = HEADS_PER_GROUP * HEAD_DIM
ATTN_WIDTH = N_ATTN_GROUPS * GROUP_WIDTH
ATTN_BLOCK = 128
ROPE_THETA = 10000.0
GMLP_CHUNK = 128
GMLP_GROUPS = 8
GMLP_WIDTH = D_MODEL
D_FF = 2816
DEPTH = 1
ALPHA = (2 * DEPTH) ** 0.25
LN_EPS = 1e-5

LANES = 128
VMEM_LIMIT_BYTES = 56 * 1024 * 1024

_Q0, _K0, _V0 = 0, ATTN_WIDTH, 2 * ATTN_WIDTH
_U0 = 3 * ATTN_WIDTH
_VG0 = _U0 + GMLP_WIDTH
_GA0 = _VG0 + GMLP_WIDTH
_GM0 = _GA0 + D_MODEL

BF16 = jnp.bfloat16
F32 = jnp.float32


def _layer_norm(x, g, b):
    mu = jnp.mean(x, axis=-1, keepdims=True)
    xc = x - mu
    var = jnp.mean(xc * xc, axis=-1, keepdims=True)
    return xc * lax.rsqrt(var + LN_EPS) * g + b


def _dot(a, b):
    return jnp.dot(a, b, preferred_element_type=F32)


def _swiglu(xb, wg_ref, wu_ref, wd_ref):
    g = _dot(xb, wg_ref[...])
    u = _dot(xb, wu_ref[...])
    a = g * jax.nn.sigmoid(g) * u
    return _dot(a.astype(BF16), wd_ref[...])


def _gelu(x):
    return 0.5 * x * (1.0 + lax.erf(x * math.sqrt(0.5)))


def _resident(shape):
    return pl.BlockSpec(shape, lambda *_: (0,) * len(shape), pipeline_mode=pl.Buffered(1))


def _rows(tm, width):
    return pl.BlockSpec((tm, width), lambda i: (i, 0))


def _params(n_axes):
    return pltpu.CompilerParams(dimension_semantics=("arbitrary",) * n_axes,
                                vmem_limit_bytes=VMEM_LIMIT_BYTES)


def _ffn_ln_kernel(x_ref, wg_ref, wu_ref, wd_ref, g_ref, b_ref, h_ref, hb_ref):
    x = x_ref[...]
    y = _swiglu(x.astype(BF16), wg_ref, wu_ref, wd_ref)
    h = _layer_norm(ALPHA * x + 0.5 * y, g_ref[...], b_ref[...])
    h_ref[...] = h
    hb_ref[...] = h.astype(BF16)


def _ffn_ln(x, wg, wu, wd, g, b, tm):
    t = x.shape[0]
    return pl.pallas_call(
        _ffn_ln_kernel,
        grid=(t // tm,),
        in_specs=[_rows(tm, D_MODEL), _resident((D_MODEL, D_FF)), _resident((D_MODEL, D_FF)),
                  _resident((D_FF, D_MODEL)), _resident((1, D_MODEL)), _resident((1, D_MODEL))],
        out_specs=[_rows(tm, D_MODEL), _rows(tm, D_MODEL)],
        out_shape=[jax.ShapeDtypeStruct((t, D_MODEL), F32), jax.ShapeDtypeStruct((t, D_MODEL), BF16)],
        compiler_params=_params(1),
        name="ffn_ln",
    )(x, wg, wu, wd, g, b)


def _store_residue_major(x, out_ref, scr_ref):
    d = out_ref.shape[0]
    if d == 1:
        out_ref[0] = x.astype(out_ref.dtype)
        return
    tm, width = x.shape
    n_slab = width // LANES
    for j in range(n_slab):
        scr_ref[j] = x[:, j * LANES:(j + 1) * LANES]
    for r in range(d):
        rows = [scr_ref[j, pl.ds(r, tm // d, stride=d), :] for j in range(n_slab)]
        out_ref[r] = jnp.concatenate(rows, axis=1).astype(out_ref.dtype)


def _load_token_major(in_ref, scr_ref):
    d, rows, width = in_ref.shape
    if d == 1:
        return in_ref[0].astype(F32)
    n_slab = width // LANES
    for r in range(d):
        x = in_ref[r].astype(F32)
        for j in range(n_slab):
            scr_ref[j, pl.ds(r, rows, stride=d), :] = x[:, j * LANES:(j + 1) * LANES]
    return jnp.concatenate([scr_ref[j] for j in range(n_slab)], axis=1)


def _mixer_in_kernel(hb_ref, pos_ref, invf_ref, w_ref, bg_ref, lng_ref, lnb_ref, ws_ref, bs_ref, wgb_ref,
                     q0_ref, k0_ref, v0_ref, q1_ref, k1_ref, v1_ref, q2_ref, k2_ref, v2_ref,
                     ga_ref, gm_ref, scr_ref):
    tm = hb_ref.shape[0]
    hb = hb_ref[...]

    ang = pos_ref[...].astype(F32) * invf_ref[...]
    cos = jnp.cos(ang)
    sin = jnp.sin(ang)
    lane = lax.broadcasted_iota(jnp.int32, (1, LANES), 1)
    first_half = (lane % HEAD_DIM) < (HEAD_DIM // 2)
    sin = jnp.where(first_half, -sin, sin)

    def rope(p, scale):
        c, s = cos * scale, sin * scale
        blocks = []
        for j in range(GROUP_WIDTH // LANES):
            x = p[:, j * LANES:(j + 1) * LANES]
            partner = jnp.where(first_half, pltpu.roll(x, LANES - HEAD_DIM // 2, 1),
                                pltpu.roll(x, HEAD_DIM // 2, 1))
            blocks.append(x * c + partner * s)
        return jnp.concatenate(blocks, axis=1)

    q_refs = (q0_ref, q1_ref, q2_ref)
    k_refs = (k0_ref, k1_ref, k2_ref)
    v_refs = (v0_ref, v1_ref, v2_ref)
    for gi in range(N_ATTN_GROUPS):
        c0 = gi * GROUP_WIDTH
        q = _dot(hb, w_ref[:, _Q0 + c0:_Q0 + c0 + GROUP_WIDTH])
        _store_residue_major(rope(q, HEAD_DIM ** -0.5), q_refs[gi], scr_ref.at[0])
        k = _dot(hb, w_ref[:, _K0 + c0:_K0 + c0 + GROUP_WIDTH])
        _store_residue_major(rope(k, 1.0), k_refs[gi], scr_ref.at[1])
        v = _dot(hb, w_ref[:, _V0 + c0:_V0 + c0 + GROUP_WIDTH])
        _store_residue_major(v, v_refs[gi], scr_ref.at[2])

    u = _gelu(_dot(hb, w_ref[:, _U0:_U0 + GMLP_WIDTH]))
    vg = _gelu(_dot(hb, w_ref[:, _VG0:_VG0 + GMLP_WIDTH]))
    vgn = _layer_norm(vg, lng_ref[...], lnb_ref[...]).astype(BF16)
    n_chunk = tm // GMLP_CHUNK
    gdim = GMLP_WIDTH // GMLP_GROUPS
    row = lax.broadcasted_iota(jnp.int32, (GMLP_CHUNK, GMLP_CHUNK), 0)
    col = lax.broadcasted_iota(jnp.int32, (GMLP_CHUNK, GMLP_CHUNK), 1)
    causal = col <= row
    mixed_cols = []
    for g in range(GMLP_GROUPS):
        ws = jnp.where(causal, ws_ref[g], 0.0).astype(BF16)
        rhs = jnp.concatenate(
            [vgn[c * GMLP_CHUNK:(c + 1) * GMLP_CHUNK, g * gdim:(g + 1) * gdim] for c in range(n_chunk)], axis=1)
        mixed_cols.append(_dot(ws, rhs))
    mixed = jnp.concatenate(
        [jnp.concatenate([mixed_cols[g][:, c * gdim:(c + 1) * gdim] for g in range(GMLP_GROUPS)], axis=1)
         + bs_ref[...] for c in range(n_chunk)], axis=0)
    y_gmlp = (u * mixed).astype(BF16)
    branch = _dot(y_gmlp, wgb_ref[...])
    gate_m = jax.nn.sigmoid(_dot(hb, w_ref[:, _GM0:_GM0 + D_MODEL]) + bg_ref[:, D_MODEL:])
    gm_ref[...] = (gate_m * branch).astype(BF16)
    gate_a = jax.nn.sigmoid(_dot(hb, w_ref[:, _GA0:_GA0 + D_MODEL]) + bg_ref[:, :D_MODEL])
    ga_ref[...] = gate_a.astype(BF16)


def _residue_major_spec(d, tm, width, tiles_per_seq):
    return pl.BlockSpec((None, d, tm // d, width), lambda i: (i // tiles_per_seq, 0, i % tiles_per_seq, 0))


def _mixer_in(hb, pos, invf, w_in, b_gates, ln_g, ln_b, w_s, b_s, w_gb, batch, tm):
    t = hb.shape[0]
    seq = t // batch
    in_width = w_in.shape[1]
    qkv_specs, qkv_shapes = [], []
    for _, d in ATTN_PATTERNS:
        qkv_specs += [_residue_major_spec(d, tm, GROUP_WIDTH, seq // tm)] * 3
        qkv_shapes += [jax.ShapeDtypeStruct((batch, d, seq // d, GROUP_WIDTH), BF16)] * 3
    return pl.pallas_call(
        _mixer_in_kernel,
        grid=(t // tm,),
        in_specs=[_rows(tm, D_MODEL), _rows(tm, 1), _resident((1, LANES)), _resident((D_MODEL, in_width)),
                  _resident((1, 2 * D_MODEL)), _resident((1, GMLP_WIDTH)), _resident((1, GMLP_WIDTH)),
                  _resident((GMLP_GROUPS, GMLP_CHUNK, GMLP_CHUNK)), _resident((GMLP_CHUNK, GMLP_WIDTH)),
                  _resident((GMLP_WIDTH, D_MODEL))],
        out_specs=qkv_specs + [_rows(tm, D_MODEL), _rows(tm, D_MODEL)],
        out_shape=qkv_shapes + [jax.ShapeDtypeStruct((t, D_MODEL), BF16)] * 2,
        scratch_shapes=[pltpu.VMEM((3, GROUP_WIDTH // LANES, tm, LANES), F32)],
        compiler_params=_params(1),
        name="mixer_in",
    )(hb, pos, invf, w_in, b_gates, ln_g, ln_b, w_s, b_s, w_gb)


def _attn_kernel(q_ref, k_ref, v_ref, o_ref, lse_ref, *, n_blk):
    n_items = q_ref.shape[0] // ATTN_BLOCK
    n_pairs = GROUP_WIDTH // LANES
    has_prev = n_blk > 1
    row = lax.broadcasted_iota(jnp.int32, (2 * ATTN_BLOCK, ATTN_BLOCK), 0) % ATTN_BLOCK
    col = lax.broadcasted_iota(jnp.int32, (2 * ATTN_BLOCK, ATTN_BLOCK), 1)
    neg_inf = jnp.float32(-jnp.inf)
    bias_cur = jnp.where(col <= row, 0.0, neg_inf)
    bias_prev = jnp.where(col >= row, 0.0, neg_inf)
    lane = lax.broadcasted_iota(jnp.int32, (ATTN_BLOCK, LANES), 1)
    low_head = lane < HEAD_DIM
    ones = jnp.ones((2 * ATTN_BLOCK if has_prev else ATTN_BLOCK, LANES), BF16)
    contract_last = (((1,), (1,)), ((), ()))

    def item(j, carry):
        r0 = pl.multiple_of(j * ATTN_BLOCK, ATTN_BLOCK)
        if has_prev:
            p0 = pl.multiple_of(jnp.maximum(j - 1, 0) * ATTN_BLOCK, ATTN_BLOCK)
            prev_bias = bias_prev + jnp.where(j % n_blk > 0, 0.0, neg_inf)
        scores = []
        for pair in range(n_pairs):
            sl = slice(pair * LANES, (pair + 1) * LANES)
            q = q_ref[pl.ds(r0, ATTN_BLOCK), sl]
            zero = jnp.zeros_like(q)
            q2 = jnp.concatenate([jnp.where(low_head, q, zero), jnp.where(low_head, zero, q)], axis=0)
            keys = k_ref[pl.ds(r0, ATTN_BLOCK), sl]
            if has_prev:
                keys = jnp.concatenate([k_ref[pl.ds(p0, ATTN_BLOCK), sl], keys], axis=0)
            s = lax.dot_general(q2, keys, contract_last, preferred_element_type=F32)
            if has_prev:
                s = s + jnp.concatenate([prev_bias, bias_cur], axis=1)
            else:
                s = s + bias_cur
            scores.append(s)
        maxes = [jnp.max(s, axis=1, keepdims=True) for s in scores]
        probs = [jnp.exp(s - m).astype(BF16) for s, m in zip(scores, maxes)]
        lse_tile = jnp.zeros((ATTN_BLOCK, LANES), F32)
        for pair in range(n_pairs):
            sl = slice(pair * LANES, (pair + 1) * LANES)
            vals = v_ref[pl.ds(r0, ATTN_BLOCK), sl]
            if has_prev:
                vals = jnp.concatenate([v_ref[pl.ds(p0, ATTN_BLOCK), sl], vals], axis=0)
            o2 = _dot(probs[pair], vals)
            l2 = _dot(probs[pair], ones)
            o2 = o2 / l2
            lse2 = maxes[pair] + jnp.log(l2)
            o_ref[pl.ds(r0, ATTN_BLOCK), sl] = jnp.where(low_head, o2[:ATTN_BLOCK], o2[ATTN_BLOCK:]).astype(BF16)
            lse_tile = jnp.where(lane == 2 * pair, lse2[:ATTN_BLOCK], lse_tile)
            lse_tile = jnp.where(lane == 2 * pair + 1, lse2[ATTN_BLOCK:], lse_tile)
        lse_ref[pl.ds(r0, ATTN_BLOCK), :] = lse_tile
        return carry

    lax.fori_loop(0, n_items, item, 0)


def _attention(q, k, v):
    b, d, l, _ = q.shape
    n_blk = l // ATTN_BLOCK
    seq = pl.BlockSpec((None, d * l, GROUP_WIDTH), lambda bi: (bi, 0, 0))
    lse_spec = pl.BlockSpec((None, d * l, LANES), lambda bi: (bi, 0, 0))
    q, k, v = (a.reshape(b, d * l, GROUP_WIDTH) for a in (q, k, v))
    o, lse = pl.pallas_call(
        functools.partial(_attn_kernel, n_blk=n_blk),
        grid=(b,),
        in_specs=[seq, seq, seq],
        out_specs=[seq, lse_spec],
        out_shape=[jax.ShapeDtypeStruct(q.shape, BF16), jax.ShapeDtypeStruct((b, d * l, LANES), F32)],
        compiler_params=_params(1),
        name=f"band_attention_d{d}",
    )(q, k, v)
    return o.reshape(b, d, l, GROUP_WIDTH), lse.reshape(b, d, l, LANES)


def _mixer_out_kernel(o0_ref, o1_ref, o2_ref, l0_ref, l1_ref, l2_ref, ga_ref, gm_ref, h1_ref, expand_ref,
                      wab_ref, wout_ref, g2_ref, b2_ref, wg_ref, wu_ref, wd_ref, g3_ref, b3_ref, out_ref,
                      oscr_ref, lscr_ref):
    lses = [_load_token_major(l_ref, lscr_ref.at[i]) for i, l_ref in enumerate((l0_ref, l1_ref, l2_ref))]
    m = jnp.maximum(jnp.maximum(lses[0], lses[1]), lses[2])
    es = [jnp.exp(x - m) for x in lses]
    inv_den = 1.0 / (es[0] + es[1] + es[2])
    y = None
    for i, (e, o_ref) in enumerate(zip(es, (o0_ref, o1_ref, o2_ref))):
        w = e * inv_den
        w_hi = w.astype(BF16)
        w_lo = (w - w_hi.astype(F32)).astype(BF16)
        w_wide = _dot(w_hi, expand_ref[...]) + _dot(w_lo, expand_ref[...])
        term = w_wide * _load_token_major(o_ref, oscr_ref.at[i])
        y = term if y is None else y + term
    branch_a = _dot(y.astype(BF16), wab_ref[...])
    merged = ga_ref[...].astype(F32) * branch_a + gm_ref[...].astype(F32)
    mix = _dot(merged.astype(BF16), wout_ref[...])
    h2 = _layer_norm(ALPHA * h1_ref[...] + mix, g2_ref[...], b2_ref[...])
    ffn = _swiglu(h2.astype(BF16), wg_ref, wu_ref, wd_ref)
    out_ref[...] = _layer_norm(ALPHA * h2 + 0.5 * ffn, g3_ref[...], b3_ref[...])


def _mixer_out(os, lses, ga, gm, h1, expand, wab, wout, g2, b2, wg, wu, wd, g3, b3, tm):
    t = h1.shape[0]
    tiles_per_seq = os[0].shape[1] * os[0].shape[2] // tm
    o_specs = [_residue_major_spec(o.shape[1], tm, GROUP_WIDTH, tiles_per_seq) for o in os]
    lse_specs = [_residue_major_spec(l.shape[1], tm, LANES, tiles_per_seq) for l in lses]
    return pl.pallas_call(
        _mixer_out_kernel,
        grid=(t // tm,),
        scratch_shapes=[pltpu.VMEM((3, GROUP_WIDTH // LANES, tm, LANES), F32), pltpu.VMEM((3, 1, tm, LANES), F32)],
        in_specs=o_specs + lse_specs
                 + [_rows(tm, D_MODEL)] * 3
                 + [_resident((LANES, GROUP_WIDTH)), _resident((GROUP_WIDTH, D_MODEL)),
                    _resident((D_MODEL, D_MODEL)), _resident((1, D_MODEL)), _resident((1, D_MODEL)),
                    _resident((D_MODEL, D_FF)), _resident((D_MODEL, D_FF)), _resident((D_FF, D_MODEL)),
                    _resident((1, D_MODEL)), _resident((1, D_MODEL))],
        out_specs=_rows(tm, D_MODEL),
        out_shape=jax.ShapeDtypeStruct((t, D_MODEL), F32),
        compiler_params=_params(1),
        name="mixer_out_ffn",
    )(*os, *lses, ga, gm, h1, expand, wab, wout, g2, b2, wg, wu, wd, g3, b3)


def kernel(x, positions, ffn1_w_gate, ffn1_w_up, ffn1_w_down, ln1_g, ln1_b, w_in, b_gates, gmlp_ln_g, gmlp_ln_b, gmlp_w_s, gmlp_b_s, w_attn_branch, w_gmlp_branch, w_out, ln2_g, ln2_b, ffn2_w_gate, ffn2_w_up, ffn2_w_down, ln3_g, ln3_b):
    b, s, d_model = x.shape
    assert d_model == D_MODEL and ln1_g.shape[0] == DEPTH == 1
    t = b * s
    tm = 512
    assert t % tm == 0 and tm % GMLP_CHUNK == 0 and s % tm == 0

    lane = jnp.arange(LANES) % (HEAD_DIM // 2)
    invf = (ROPE_THETA ** (-(2 * lane).astype(F32) / HEAD_DIM)).reshape(1, LANES)
    head_of_lane = jnp.arange(GROUP_WIDTH) // HEAD_DIM
    expand = (jnp.arange(LANES)[:, None] == head_of_lane[None, :]).astype(BF16)
    b_s = jnp.repeat(gmlp_b_s[0].T, GMLP_WIDTH // GMLP_GROUPS, axis=1)

    h = x.reshape(t, D_MODEL)
    h1, h1b = _ffn_ln(h, ffn1_w_gate[0].astype(BF16), ffn1_w_up[0].astype(BF16), ffn1_w_down[0].astype(BF16),
                      ln1_g, ln1_b, tm)
    outs = _mixer_in(h1b, positions.reshape(t, 1), invf, w_in[0].astype(BF16), b_gates, gmlp_ln_g, gmlp_ln_b,
                     gmlp_w_s[0], b_s, w_gmlp_branch[0].astype(BF16), b, tm)
    ga, gm = outs[9], outs[10]
    os, lses = [], []
    for gi in range(N_ATTN_GROUPS):
        o, lse = _attention(*outs[3 * gi:3 * gi + 3])
        os.append(o)
        lses.append(lse)
    out = _mixer_out(os, lses, ga, gm, h1, expand, w_attn_branch[0].astype(BF16), w_out[0].astype(BF16),
                     ln2_g, ln2_b, ffn2_w_gate[0].astype(BF16), ffn2_w_up[0].astype(BF16),
                     ffn2_w_down[0].astype(BF16), ln3_g, ln3_b, tm)
    return out.reshape(b, s, D_MODEL)
```

```python
import functools
import math

import jax
import jax.numpy as jnp
from jax import lax
from jax.experimental import pallas as pl
from jax.experimental.pallas import tpu as pltpu

D_MODEL = 1024
HEAD_DIM = 64
HEADS_PER_GROUP = 8
ATTN_PATTERNS = ((128, 1), (512, 4), (2048, 16))
N_ATTN_GROUPS = len(ATTN_PATTERNS)
GROUP_WIDTH = HEADS_PER_GROUP * HEAD_DIM
ATTN_WIDTH = N_ATTN_GROUPS * GROUP_WIDTH
ATTN_BLOCK = 128
ROPE_THETA = 10000.0
GMLP_CHUNK = 128
GMLP_GROUPS = 8
GMLP_WIDTH = D_MODEL
D_FF = 2816
DEPTH = 1
ALPHA = (2 * DEPTH) ** 0.25
LN_EPS = 1e-5

LANES = 128
VMEM_LIMIT_BYTES = 60 * 1024 * 1024
N_SLAB = GROUP_WIDTH // LANES

_Q0, _K0, _V0 = 0, ATTN_WIDTH, 2 * ATTN_WIDTH
_U0 = 3 * ATTN_WIDTH
_VG0 = _U0 + GMLP_WIDTH
_GA0 = _VG0 + GMLP_WIDTH
_GM0 = _GA0 + D_MODEL

_DEN_SHIFT = 8


def _stat_lane(h):
    return HEAD_DIM + h if h % 2 == 0 else h


BF16 = jnp.bfloat16
F32 = jnp.float32


def _layer_norm(x, g, b):
    mu = jnp.mean(x, axis=-1, keepdims=True)
    xc = x - mu
    var = jnp.mean(xc * xc, axis=-1, keepdims=True)
    return xc * lax.rsqrt(var + LN_EPS) * g + b


def _dot(a, b):
    return jnp.dot(a, b, preferred_element_type=F32)


def _swiglu(xb, wg_ref, wu_ref, wd_ref):
    g = _dot(xb, wg_ref[...])
    u = _dot(xb, wu_ref[...])
    a = g * jax.nn.sigmoid(g) * u
    return _dot(a.astype(BF16), wd_ref[...])


def _gelu(x):
    return 0.5 * x * (1.0 + lax.erf(x * math.sqrt(0.5)))


def _resident(shape):
    return pl.BlockSpec(shape, lambda *_: (0,) * len(shape), pipeline_mode=pl.Buffered(1))


def _rows(tm, width):
    return pl.BlockSpec((tm, width), lambda i: (i, 0))


def _residue_major_spec(d, tm, width, tiles_per_seq):
    return pl.BlockSpec((None, d, tm // d, width), lambda i: (i // tiles_per_seq, 0, i % tiles_per_seq, 0))


def _params(n_axes):
    return pltpu.CompilerParams(dimension_semantics=("arbitrary",) * n_axes,
                                vmem_limit_bytes=VMEM_LIMIT_BYTES)


def _stage(index):
    del index

    def run(fn):
        fn()
    return run


def _ffn_ln_kernel(x_ref, wg_ref, wu_ref, wd_ref, g_ref, b_ref, h_ref, hb_ref):
    x = x_ref[...]
    y = _swiglu(x.astype(BF16), wg_ref, wu_ref, wd_ref)
    h = _layer_norm(ALPHA * x + 0.5 * y, g_ref[...], b_ref[...])
    h_ref[...] = h
    hb_ref[...] = h.astype(BF16)


def _ffn_ln(x, wg, wu, wd, g, b, tm):
    t = x.shape[0]
    return pl.pallas_call(
        _ffn_ln_kernel,
        grid=(t // tm,),
        in_specs=[_rows(tm, D_MODEL), _resident((D_MODEL, D_FF)), _resident((D_MODEL, D_FF)),
                  _resident((D_FF, D_MODEL)), _resident((1, D_MODEL)), _resident((1, D_MODEL))],
        out_specs=[_rows(tm, D_MODEL), _rows(tm, D_MODEL)],
        out_shape=[jax.ShapeDtypeStruct((t, D_MODEL), F32), jax.ShapeDtypeStruct((t, D_MODEL), BF16)],
        compiler_params=_params(1),
        name="ffn_ln",
    )(x, wg, wu, wd, g, b)


def _project_slabs(hb_ref, w_ref, col0, dst_ref):
    res = _dot(hb_ref[...], w_ref[:, col0:col0 + GROUP_WIDTH])
    for j in range(N_SLAB):
        dst_ref[j] = res[:, j * LANES:(j + 1) * LANES]


def _emit_residue_major(p_ref, out_ref, cos_ref=None, sin_ref=None):
    d, rows, _ = out_ref.shape
    lane = lax.broadcasted_iota(jnp.int32, (1, LANES), 1)
    first_half = (lane % HEAD_DIM) < (HEAD_DIM // 2)
    for r in range(d):
        rs = pl.ds(r, rows, stride=d) if d > 1 else slice(None)
        if cos_ref is not None:
            c, s = cos_ref[rs, :], sin_ref[rs, :]
        blocks = []
        for j in range(N_SLAB):
            x = p_ref[j, rs, :]
            if cos_ref is not None:
                partner = jnp.where(first_half, pltpu.roll(x, LANES - HEAD_DIM // 2, 1),
                                    pltpu.roll(x, HEAD_DIM // 2, 1))
                x = x * c + partner * s
            blocks.append(x)
        out_ref[r] = jnp.concatenate(blocks, axis=1).astype(out_ref.dtype)


def _mixer_in_kernel(hb_ref, pos_ref, invf_ref, w_ref, bg_ref, lng_ref, lnb_ref, ws_ref, bs_ref, wgb_ref,
                     q0_ref, k0_ref, v0_ref, q1_ref, k1_ref, v1_ref, q2_ref, k2_ref, v2_ref,
                     ga_ref, gm_ref, a_ref, g_ref, p_ref, cs_ref, vgn_ref):
    tm = hb_ref.shape[0]
    qkv_refs = ((q0_ref, k0_ref, v0_ref), (q1_ref, k1_ref, v1_ref), (q2_ref, k2_ref, v2_ref))

    def project_group(gi, slot):
        c0 = gi * GROUP_WIDTH
        for i, base in enumerate((_Q0, _K0, _V0)):
            _project_slabs(hb_ref, w_ref, base + c0, p_ref.at[slot + i])

    def emit_group(gi, slot):
        q_ref, k_ref, v_ref = qkv_refs[gi]
        _emit_residue_major(p_ref.at[slot], q_ref, cs_ref.at[2], cs_ref.at[3])
        _emit_residue_major(p_ref.at[slot + 1], k_ref, cs_ref.at[0], cs_ref.at[1])
        _emit_residue_major(p_ref.at[slot + 2], v_ref)

    @_stage(0)
    def _():
        hb = hb_ref[...]
        a_ref[0] = _dot(hb, w_ref[:, _U0:_U0 + GMLP_WIDTH])
        a_ref[1] = _dot(hb, w_ref[:, _VG0:_VG0 + GMLP_WIDTH])
        ang = pos_ref[...].astype(F32) * invf_ref[...]
        lane = lax.broadcasted_iota(jnp.int32, (1, LANES), 1)
        first_half = (lane % HEAD_DIM) < (HEAD_DIM // 2)
        cos = jnp.cos(ang)
        sin = jnp.sin(ang)
        sin = jnp.where(first_half, -sin, sin)
        cs_ref[0] = cos
        cs_ref[1] = sin
        cs_ref[2] = cos * HEAD_DIM ** -0.5
        cs_ref[3] = sin * HEAD_DIM ** -0.5

    @_stage(1)
    def _():
        hb = hb_ref[...]
        g_ref[0] = _dot(hb, w_ref[:, _GM0:_GM0 + D_MODEL]) + bg_ref[:, D_MODEL:]
        g_ref[1] = _dot(hb, w_ref[:, _GA0:_GA0 + D_MODEL]) + bg_ref[:, :D_MODEL]
        vgn_ref[...] = _layer_norm(_gelu(a_ref[1]), lng_ref[...], lnb_ref[...]).astype(BF16)

    @_stage(2)
    def _():
        project_group(0, 0)
        n_chunk = tm // GMLP_CHUNK
        gdim = GMLP_WIDTH // GMLP_GROUPS
        row = lax.broadcasted_iota(jnp.int32, (GMLP_CHUNK, GMLP_CHUNK), 0)
        col = lax.broadcasted_iota(jnp.int32, (GMLP_CHUNK, GMLP_CHUNK), 1)
        causal = col <= row
        for g in range(GMLP_GROUPS):
            ws = jnp.where(causal, ws_ref[g], 0.0).astype(BF16)
            rhs = jnp.concatenate(
                [vgn_ref[c * GMLP_CHUNK:(c + 1) * GMLP_CHUNK, g * gdim:(g + 1) * gdim] for c in range(n_chunk)],
                axis=1)
            mixed = _dot(ws, rhs)
            for c in range(n_chunk):
                a_ref[1, c * GMLP_CHUNK:(c + 1) * GMLP_CHUNK, g * gdim:(g + 1) * gdim] = (
                    mixed[:, c * gdim:(c + 1) * gdim] + bs_ref[:, g * gdim:(g + 1) * gdim])
        a_ref[0] = _gelu(a_ref[0])

    @_stage(3)
    def _():
        project_group(1, 3)
        vgn_ref[...] = (a_ref[0] * a_ref[1]).astype(BF16)
        emit_group(0, 0)
        ga_ref[...] = jax.nn.sigmoid(g_ref[1]).astype(BF16)

    @_stage(4)
    def _():
        project_group(2, 0)
        emit_group(1, 3)
        g_ref[0] = jax.nn.sigmoid(g_ref[0])

    @_stage(5)
    def _():
        gm_ref[...] = (g_ref[0] * _dot(vgn_ref[...], wgb_ref[...])).astype(BF16)
        emit_group(2, 0)


def _mixer_in(hb, pos, invf, w_in, b_gates, ln_g, ln_b, w_s, b_s, w_gb, batch, tm):
    t = hb.shape[0]
    seq = t // batch
    in_width = w_in.shape[1]
    qkv_specs, qkv_shapes = [], []
    for _, d in ATTN_PATTERNS:
        qkv_specs += [_residue_major_spec(d, tm, GROUP_WIDTH, seq // tm)] * 3
        qkv_shapes += [jax.ShapeDtypeStruct((batch, d, seq // d, GROUP_WIDTH), BF16)] * 3
    return pl.pallas_call(
        _mixer_in_kernel,
        grid=(t // tm,),
        in_specs=[_rows(tm, D_MODEL), _rows(tm, 1), _resident((1, LANES)), _resident((D_MODEL, in_width)),
                  _resident((1, 2 * D_MODEL)), _resident((1, GMLP_WIDTH)), _resident((1, GMLP_WIDTH)),
                  _resident((GMLP_GROUPS, GMLP_CHUNK, GMLP_CHUNK)), _resident((GMLP_CHUNK, GMLP_WIDTH)),
                  _resident((GMLP_WIDTH, D_MODEL))],
        out_specs=qkv_specs + [_rows(tm, D_MODEL), _rows(tm, D_MODEL)],
        out_shape=qkv_shapes + [jax.ShapeDtypeStruct((t, D_MODEL), BF16)] * 2,
        scratch_shapes=[pltpu.VMEM((2, tm, GMLP_WIDTH), F32), pltpu.VMEM((2, tm, D_MODEL), F32),
                        pltpu.VMEM((6, N_SLAB, tm, LANES), F32), pltpu.VMEM((4, tm, LANES), F32),
                        pltpu.VMEM((tm, GMLP_WIDTH), BF16)],
        compiler_params=_params(1),
        name="mixer_in",
    )(hb, pos, invf, w_in, b_gates, ln_g, ln_b, w_s, b_s, w_gb)


def _attn_kernel(q_ref, k_ref, v_ref, acc_ref, stat_ref, *, n_blk):
    n_items = q_ref.shape[0] // ATTN_BLOCK
    has_prev = n_blk > 1
    row = lax.broadcasted_iota(jnp.int32, (2 * ATTN_BLOCK, ATTN_BLOCK), 0) % ATTN_BLOCK
    col = lax.broadcasted_iota(jnp.int32, (2 * ATTN_BLOCK, ATTN_BLOCK), 1)
    neg_inf = jnp.float32(-jnp.inf)
    bias_cur = jnp.where(col <= row, 0.0, neg_inf)
    bias_prev = jnp.where(col >= row, 0.0, neg_inf)
    lane = lax.broadcasted_iota(jnp.int32, (ATTN_BLOCK, LANES), 1)
    low_head = lane < HEAD_DIM
    n_keys = 2 * ATTN_BLOCK if has_prev else ATTN_BLOCK
    low_head_keys = lax.broadcasted_iota(jnp.int32, (n_keys, LANES), 1) < HEAD_DIM
    contract_last = (((1,), (1,)), ((), ()))

    def item(j, carry):
        r0 = pl.multiple_of(j * ATTN_BLOCK, ATTN_BLOCK)
        if has_prev:
            p0 = pl.multiple_of(jnp.maximum(j - 1, 0) * ATTN_BLOCK, ATTN_BLOCK)
            prev_bias = bias_prev + jnp.where(j % n_blk > 0, 0.0, neg_inf)
        scores = []
        for pair in range(N_SLAB):
            sl = slice(pair * LANES, (pair + 1) * LANES)
            q = q_ref[pl.ds(r0, ATTN_BLOCK), sl]
            zero = jnp.zeros_like(q)
            q2 = jnp.concatenate([jnp.where(low_head, q, zero), jnp.where(low_head, zero, q)], axis=0)
            keys = k_ref[pl.ds(r0, ATTN_BLOCK), sl]
            if has_prev:
                keys = jnp.concatenate([k_ref[pl.ds(p0, ATTN_BLOCK), sl], keys], axis=0)
            s = lax.dot_general(q2, keys, contract_last, preferred_element_type=F32)
            if has_prev:
                s = s + jnp.concatenate([prev_bias, bias_cur], axis=1)
            else:
                s = s + bias_cur
            scores.append(s)
        maxes = [jnp.max(s, axis=1, keepdims=True) for s in scores]
        probs = [jnp.exp(s - m).astype(BF16) for s, m in zip(scores, maxes)]
        stat = jnp.zeros((ATTN_BLOCK, LANES), F32)
        for pair in range(N_SLAB):
            sl = slice(pair * LANES, (pair + 1) * LANES)
            vals = v_ref[pl.ds(r0, ATTN_BLOCK), sl]
            if has_prev:
                vals = jnp.concatenate([v_ref[pl.ds(p0, ATTN_BLOCK), sl], vals], axis=0)
            one = jnp.ones_like(vals)
            r_even = _dot(probs[pair][:ATTN_BLOCK], jnp.where(low_head_keys, vals, one))
            r_odd = _dot(probs[pair][ATTN_BLOCK:], jnp.where(low_head_keys, one, vals))
            acc_ref[pl.ds(r0, ATTN_BLOCK), sl] = jnp.where(low_head, r_even, r_odd).astype(BF16)
            m_even, m_odd = maxes[pair][:ATTN_BLOCK], maxes[pair][ATTN_BLOCK:]
            for h, m_h, r_h in ((2 * pair, m_even, r_even), (2 * pair + 1, m_odd, r_odd)):
                stat = jnp.where(lane == _stat_lane(h), m_h, stat)
                stat = jnp.where(lane == _stat_lane(h) + _DEN_SHIFT, r_h, stat)
        stat_ref[pl.ds(r0, ATTN_BLOCK), :] = stat
        return carry

    lax.fori_loop(0, n_items, item, 0, unroll=4)


def _attention(q, k, v):
    b, d, l, _ = q.shape
    n_blk = l // ATTN_BLOCK
    seq = pl.BlockSpec((None, d * l, GROUP_WIDTH), lambda bi: (bi, 0, 0))
    stat_spec = pl.BlockSpec((None, d * l, LANES), lambda bi: (bi, 0, 0))
    q, k, v = (a.reshape(b, d * l, GROUP_WIDTH) for a in (q, k, v))
    acc, stat = pl.pallas_call(
        functools.partial(_attn_kernel, n_blk=n_blk),
        grid=(b,),
        in_specs=[seq, seq, seq],
        out_specs=[seq, stat_spec],
        out_shape=[jax.ShapeDtypeStruct(q.shape, BF16), jax.ShapeDtypeStruct((b, d * l, LANES), F32)],
        compiler_params=_params(1),
        name=f"band_attention_d{d}",
    )(q, k, v)
    return acc.reshape(b, d, l, GROUP_WIDTH), stat.reshape(b, d, l, LANES)


def _load_token_major(in_ref, scr_ref):
    d, rows, width = in_ref.shape
    if d == 1:
        return in_ref[0].astype(F32)
    n_slab = width // LANES
    for r in range(d):
        x = in_ref[r].astype(F32)
        for j in range(n_slab):
            scr_ref[j, pl.ds(r, rows, stride=d), :] = x[:, j * LANES:(j + 1) * LANES]
    return jnp.concatenate([scr_ref[j] for j in range(n_slab)], axis=1)


def _mixer_out_kernel(a0_ref, a1_ref, a2_ref, s0_ref, s1_ref, s2_ref, ga_ref, gm_ref, h1_ref, expand_ref,
                      wab_ref, wout_ref, g2_ref, b2_ref, wg_ref, wu_ref, wd_ref, g3_ref, b3_ref, out_ref,
                      ascr_ref, sscr_ref):
    stats = [_load_token_major(s_ref, sscr_ref.at[i]) for i, s_ref in enumerate((s0_ref, s1_ref, s2_ref))]
    m = jnp.maximum(jnp.maximum(stats[0], stats[1]), stats[2])
    es = [jnp.exp(x - m) for x in stats]
    dens = [pltpu.roll(x, LANES - _DEN_SHIFT, 1) for x in stats]
    lane = lax.broadcasted_iota(jnp.int32, (1, LANES), 1)
    is_max_lane = functools.reduce(jnp.logical_or, [lane == _stat_lane(h) for h in range(HEADS_PER_GROUP)])
    den = es[0] * dens[0] + es[1] * dens[1] + es[2] * dens[2]
    inv_den = 1.0 / jnp.where(is_max_lane, den, 1.0)
    y = None
    for i, (e, a_ref) in enumerate(zip(es, (a0_ref, a1_ref, a2_ref))):
        w = e * inv_den
        w_hi = w.astype(BF16)
        w_lo = (w - w_hi.astype(F32)).astype(BF16)
        w_wide = _dot(w_hi, expand_ref[...]) + _dot(w_lo, expand_ref[...])
        term = w_wide * _load_token_major(a_ref, ascr_ref.at[i])
        y = term if y is None else y + term
    branch_a = _dot(y.astype(BF16), wab_ref[...])
    merged = ga_ref[...].astype(F32) * branch_a + gm_ref[...].astype(F32)
    mix = _dot(merged.astype(BF16), wout_ref[...])
    h2 = _layer_norm(ALPHA * h1_ref[...] + mix, g2_ref[...], b2_ref[...])
    ffn = _swiglu(h2.astype(BF16), wg_ref, wu_ref, wd_ref)
    out_ref[...] = _layer_norm(ALPHA * h2 + 0.5 * ffn, g3_ref[...], b3_ref[...])


def _mixer_out(accs, stats, ga, gm, h1, expand, wab, wout, g2, b2, wg, wu, wd, g3, b3, tm):
    t = h1.shape[0]
    tiles_per_seq = accs[0].shape[1] * accs[0].shape[2] // tm
    acc_specs = [_residue_major_spec(a.shape[1], tm, GROUP_WIDTH, tiles_per_seq) for a in accs]
    stat_specs = [_residue_major_spec(s.shape[1], tm, LANES, tiles_per_seq) for s in stats]
    return pl.pallas_call(
        _mixer_out_kernel,
        grid=(t // tm,),
        scratch_shapes=[pltpu.VMEM((3, N_SLAB, tm, LANES), F32), pltpu.VMEM((3, 1, tm, LANES), F32)],
        in_specs=acc_specs + stat_specs
                 + [_rows(tm, D_MODEL)] * 3
                 + [_resident((LANES, GROUP_WIDTH)), _resident((GROUP_WIDTH, D_MODEL)),
                    _resident((D_MODEL, D_MODEL)), _resident((1, D_MODEL)), _resident((1, D_MODEL)),
                    _resident((D_MODEL, D_FF)), _resident((D_MODEL, D_FF)), _resident((D_FF, D_MODEL)),
                    _resident((1, D_MODEL)), _resident((1, D_MODEL))],
        out_specs=_rows(tm, D_MODEL),
        out_shape=jax.ShapeDtypeStruct((t, D_MODEL), F32),
        compiler_params=_params(1),
        name="mixer_out_ffn",
    )(*accs, *stats, ga, gm, h1, expand, wab, wout, g2, b2, wg, wu, wd, g3, b3)


def kernel(x, positions, ffn1_w_gate, ffn1_w_up, ffn1_w_down, ln1_g, ln1_b, w_in, b_gates, gmlp_ln_g, gmlp_ln_b, gmlp_w_s, gmlp_b_s, w_attn_branch, w_gmlp_branch, w_out, ln2_g, ln2_b, ffn2_w_gate, ffn2_w_up, ffn2_w_down, ln3_g, ln3_b):
    b, s, d_model = x.shape
    assert d_model == D_MODEL and ln1_g.shape[0] == DEPTH == 1
    t = b * s
    tm = 512
    assert t % tm == 0 and tm % GMLP_CHUNK == 0 and s % tm == 0

    lane = jnp.arange(LANES) % (HEAD_DIM // 2)
    invf = (ROPE_THETA ** (-(2 * lane).astype(F32) / HEAD_DIM)).reshape(1, LANES)
    stat_lane_of_col = jnp.array([_stat_lane(c // HEAD_DIM) for c in range(GROUP_WIDTH)])
    expand = (jnp.arange(LANES)[:, None] == stat_lane_of_col[None, :]).astype(BF16)
    b_s = jnp.repeat(gmlp_b_s[0].T, GMLP_WIDTH // GMLP_GROUPS, axis=1)

    h = x.reshape(t, D_MODEL)
    h1, h1b = _ffn_ln(h, ffn1_w_gate[0].astype(BF16), ffn1_w_up[0].astype(BF16), ffn1_w_down[0].astype(BF16),
                      ln1_g, ln1_b, tm)
    outs = _mixer_in(h1b, positions.reshape(t, 1), invf, w_in[0].astype(BF16), b_gates, gmlp_ln_g, gmlp_ln_b,
                     gmlp_w_s[0], b_s, w_gmlp_branch[0].astype(BF16), b, tm)
    ga, gm = outs[9], outs[10]
    accs, stats = [], []
    for gi in range(N_ATTN_GROUPS):
        acc, stat = _attention(*outs[3 * gi:3 * gi + 3])
        accs.append(acc)
        stats.append(stat)
    out = _mixer_out(accs, stats, ga, gm, h1, expand, w_attn_branch[0].astype(BF16), w_out[0].astype(BF16),
                     ln2_g, ln2_b, ffn2_w_gate[0].astype(BF16), ffn2_w_up[0].astype(BF16),
                     ffn2_w_down[0].astype(BF16), ln3_g, ln3_b, tm)
    return out.reshape(b, s, D_MODEL)
```

```python
import functools
import math

import jax
import jax.numpy as jnp
from jax import lax
from jax.experimental import pallas as pl
from jax.experimental.pallas import tpu as pltpu

D_MODEL = 1024
HEAD_DIM = 64
HEADS_PER_GROUP = 8
ATTN_PATTERNS = ((128, 1), (512, 4), (2048, 16))
N_ATTN_GROUPS = len(ATTN_PATTERNS)
GROUP_WIDTH = HEADS_PER_GROUP * HEAD_DIM
ATTN_WIDTH = N_ATTN_GROUPS * GROUP_WIDTH
ATTN_BLOCK = 128
ROPE_THETA = 10000.0
GMLP_CHUNK = 128
GMLP_GROUPS = 8
GMLP_WIDTH = D_MODEL
D_FF = 2816
DEPTH = 1
ALPHA = (2 * DEPTH) ** 0.25
LN_EPS = 1e-5

LANES = 128
VMEM_LIMIT_BYTES = 60 * 1024 * 1024
N_SLAB = GROUP_WIDTH // LANES

_Q0, _K0, _V0 = 0, ATTN_WIDTH, 2 * ATTN_WIDTH
_U0 = 3 * ATTN_WIDTH

_DEN_SHIFT = 8


def _stat_lane(h):
    return HEAD_DIM + h if h % 2 == 0 else h


BF16 = jnp.bfloat16
F32 = jnp.float32


def _layer_norm(x, g, b):
    mu = jnp.mean(x, axis=-1, keepdims=True)
    xc = x - mu
    var = jnp.mean(xc * xc, axis=-1, keepdims=True)
    return xc * lax.rsqrt(var + LN_EPS) * g + b


def _dot(a, b):
    return jnp.dot(a, b, preferred_element_type=F32)


def _swiglu(xb, wg_ref, wu_ref, wd_ref):
    g = _dot(xb, wg_ref[...])
    u = _dot(xb, wu_ref[...])
    a = g * jax.nn.sigmoid(g) * u
    return _dot(a.astype(BF16), wd_ref[...])


def _gelu(x):
    return 0.5 * x * (1.0 + lax.erf(x * math.sqrt(0.5)))


def _resident(shape):
    return pl.BlockSpec(shape, lambda *_: (0,) * len(shape), pipeline_mode=pl.Buffered(1))


def _rows(tm, width):
    return pl.BlockSpec((tm, width), lambda i: (i, 0))


def _residue_major_spec(d, tm, width, tiles_per_seq):
    return pl.BlockSpec((None, d, tm // d, width), lambda i: (i // tiles_per_seq, 0, i % tiles_per_seq, 0))


def _params(n_axes):
    return pltpu.CompilerParams(dimension_semantics=("arbitrary",) * n_axes,
                                vmem_limit_bytes=VMEM_LIMIT_BYTES)


def _ffn_ln_kernel(x_ref, wg_ref, wu_ref, wd_ref, g_ref, b_ref, h_ref):
    x = x_ref[...]
    y = _swiglu(x.astype(BF16), wg_ref, wu_ref, wd_ref)
    h_ref[...] = _layer_norm(ALPHA * x + 0.5 * y, g_ref[...], b_ref[...])


def _ffn_ln(x, wg, wu, wd, g, b, tm):
    t = x.shape[0]
    return pl.pallas_call(
        _ffn_ln_kernel,
        grid=(t // tm,),
        in_specs=[_rows(tm, D_MODEL), _resident((D_MODEL, D_FF)), _resident((D_MODEL, D_FF)),
                  _resident((D_FF, D_MODEL)), _resident((1, D_MODEL)), _resident((1, D_MODEL))],
        out_specs=_rows(tm, D_MODEL),
        out_shape=jax.ShapeDtypeStruct((t, D_MODEL), F32),
        compiler_params=_params(1),
        name="ffn_ln",
    )(x, wg, wu, wd, g, b)


MXU_COLS = 256


def _project_tiles(hb_ref, w_ref, col0, width):
    return [_dot(hb_ref[...], w_ref[:, c:c + MXU_COLS]) for c in range(col0, col0 + width, MXU_COLS)]


def _project_group(hb_ref, w_ref, gi, dst_ref):
    width = 3 * GROUP_WIDTH
    slabs_per_tile = MXU_COLS // LANES
    for t, res in enumerate(_project_tiles(hb_ref, w_ref, gi * width, width)):
        for s in range(slabs_per_tile):
            i, j = divmod(t * slabs_per_tile + s, N_SLAB)
            dst_ref[i, j] = res[:, s * LANES:(s + 1) * LANES]


def _emit_residue_major(p_ref, out_ref, cos_ref=None, sin_ref=None):
    d, rows, _ = out_ref.shape
    lane = lax.broadcasted_iota(jnp.int32, (1, LANES), 1)
    first_half = (lane % HEAD_DIM) < (HEAD_DIM // 2)
    for r in range(d):
        rs = pl.ds(r, rows, stride=d) if d > 1 else slice(None)
        if cos_ref is not None:
            c, s = cos_ref[rs, :], sin_ref[rs, :]
        blocks = []
        for j in range(N_SLAB):
            x = p_ref[j, rs, :]
            if cos_ref is not None:
                partner = jnp.where(first_half, pltpu.roll(x, LANES - HEAD_DIM // 2, 1),
                                    pltpu.roll(x, HEAD_DIM // 2, 1))
                x = x * c + partner * s
            blocks.append(x)
        out_ref[r] = jnp.concatenate(blocks, axis=1).astype(out_ref.dtype)


def _mixer_in_kernel(h_ref, pos_ref, invf_ref, w_ref, bg_ref, lng_ref, lnb_ref, ws_ref, bs_ref, wgb_ref,
                     q0_ref, k0_ref, v0_ref, q1_ref, k1_ref, v1_ref, q2_ref, k2_ref, v2_ref,
                     ga_ref, gm_ref, a_ref, g_ref, p_ref, cs_ref, vgn_ref, hb_ref):
    tm = h_ref.shape[0]
    hb_ref[...] = h_ref[...].astype(BF16)
    qkv_refs = ((q0_ref, k0_ref, v0_ref), (q1_ref, k1_ref, v1_ref), (q2_ref, k2_ref, v2_ref))
    tail0 = N_ATTN_GROUPS * 3 * GROUP_WIDTH

    def emit_group(gi, slot):
        q_ref, k_ref, v_ref = qkv_refs[gi]
        _emit_residue_major(p_ref.at[slot, 0], q_ref, cs_ref.at[2], cs_ref.at[3])
        _emit_residue_major(p_ref.at[slot, 1], k_ref, cs_ref.at[0], cs_ref.at[1])
        _emit_residue_major(p_ref.at[slot, 2], v_ref)

    tiles_per_half = GMLP_WIDTH // MXU_COLS
    for t, res in enumerate(_project_tiles(hb_ref, w_ref, tail0, 2 * GMLP_WIDTH)):
        i, c = divmod(t, tiles_per_half)
        a_ref[i, :, c * MXU_COLS:(c + 1) * MXU_COLS] = res
    ang = pos_ref[...].astype(F32) * invf_ref[...]
    lane = lax.broadcasted_iota(jnp.int32, (1, LANES), 1)
    first_half = (lane % HEAD_DIM) < (HEAD_DIM // 2)
    cos = jnp.cos(ang)
    sin = jnp.sin(ang)
    sin = jnp.where(first_half, -sin, sin)
    cs_ref[0] = cos
    cs_ref[1] = sin
    cs_ref[2] = cos * HEAD_DIM ** -0.5
    cs_ref[3] = sin * HEAD_DIM ** -0.5

    for t, res in enumerate(_project_tiles(hb_ref, w_ref, tail0 + 2 * GMLP_WIDTH, 2 * D_MODEL)):
        i, c = divmod(t, D_MODEL // MXU_COLS)
        cols = slice(c * MXU_COLS, (c + 1) * MXU_COLS)
        g_ref[i, :, cols] = res + bg_ref[:, t * MXU_COLS:(t + 1) * MXU_COLS]
    vgn_ref[...] = _layer_norm(_gelu(a_ref[1]), lng_ref[...], lnb_ref[...]).astype(BF16)

    _project_group(hb_ref, w_ref,0, p_ref.at[0])
    n_chunk = tm // GMLP_CHUNK
    gdim = GMLP_WIDTH // GMLP_GROUPS
    row = lax.broadcasted_iota(jnp.int32, (GMLP_CHUNK, GMLP_CHUNK), 0)
    col = lax.broadcasted_iota(jnp.int32, (GMLP_CHUNK, GMLP_CHUNK), 1)
    causal = col <= row
    for g in range(GMLP_GROUPS):
        ws = jnp.where(causal, ws_ref[g], 0.0).astype(BF16)
        rhs = jnp.concatenate(
            [vgn_ref[c * GMLP_CHUNK:(c + 1) * GMLP_CHUNK, g * gdim:(g + 1) * gdim] for c in range(n_chunk)],
            axis=1)
        mixed = _dot(ws, rhs)
        for c in range(n_chunk):
            a_ref[1, c * GMLP_CHUNK:(c + 1) * GMLP_CHUNK, g * gdim:(g + 1) * gdim] = (
                mixed[:, c * gdim:(c + 1) * gdim] + bs_ref[:, g * gdim:(g + 1) * gdim])
    a_ref[0] = _gelu(a_ref[0])

    _project_group(hb_ref, w_ref,1, p_ref.at[1])
    vgn_ref[...] = (a_ref[0] * a_ref[1]).astype(BF16)
    emit_group(0, 0)
    ga_ref[...] = jax.nn.sigmoid(g_ref[0]).astype(BF16)

    _project_group(hb_ref, w_ref,2, p_ref.at[0])
    emit_group(1, 1)
    g_ref[1] = jax.nn.sigmoid(g_ref[1])

    gm_ref[...] = (g_ref[1] * _dot(vgn_ref[...], wgb_ref[...])).astype(BF16)
    emit_group(2, 0)


def _mixer_in(hb, pos, invf, w_in, b_gates, ln_g, ln_b, w_s, b_s, w_gb, batch, tm):
    t = hb.shape[0]
    seq = t // batch
    in_width = w_in.shape[1]
    qkv_specs, qkv_shapes = [], []
    for _, d in ATTN_PATTERNS:
        qkv_specs += [_residue_major_spec(d, tm, GROUP_WIDTH, seq // tm)] * 3
        qkv_shapes += [jax.ShapeDtypeStruct((batch, d, seq // d, GROUP_WIDTH), BF16)] * 3
    return pl.pallas_call(
        _mixer_in_kernel,
        grid=(t // tm,),
        in_specs=[_rows(tm, D_MODEL), _rows(tm, 1), _resident((1, LANES)), _resident((D_MODEL, in_width)),
                  _resident((1, 2 * D_MODEL)), _resident((1, GMLP_WIDTH)), _resident((1, GMLP_WIDTH)),
                  _resident((GMLP_GROUPS, GMLP_CHUNK, GMLP_CHUNK)), _resident((GMLP_CHUNK, GMLP_WIDTH)),
                  _resident((GMLP_WIDTH, D_MODEL))],
        out_specs=qkv_specs + [_rows(tm, D_MODEL), _rows(tm, D_MODEL)],
        out_shape=qkv_shapes + [jax.ShapeDtypeStruct((t, D_MODEL), BF16)] * 2,
        scratch_shapes=[pltpu.VMEM((2, tm, GMLP_WIDTH), F32), pltpu.VMEM((2, tm, D_MODEL), F32),
                        pltpu.VMEM((2, 3, N_SLAB, tm, LANES), F32), pltpu.VMEM((4, tm, LANES), F32),
                        pltpu.VMEM((tm, GMLP_WIDTH), BF16), pltpu.VMEM((tm, D_MODEL), BF16)],
        compiler_params=_params(1),
        name="mixer_in",
    )(hb, pos, invf, w_in, b_gates, ln_g, ln_b, w_s, b_s, w_gb)


def _attn_kernel(q_ref, k_ref, v_ref, acc_ref, stat_ref, *, n_blk):
    n_items = q_ref.shape[0] // ATTN_BLOCK
    has_prev = n_blk > 1
    row = lax.broadcasted_iota(jnp.int32, (2 * ATTN_BLOCK, ATTN_BLOCK), 0) % ATTN_BLOCK
    col = lax.broadcasted_iota(jnp.int32, (2 * ATTN_BLOCK, ATTN_BLOCK), 1)
    neg_inf = jnp.float32(-jnp.inf)
    bias_cur = jnp.where(col <= row, 0.0, neg_inf)
    bias_prev = jnp.where(col >= row, 0.0, neg_inf)
    lane = lax.broadcasted_iota(jnp.int32, (ATTN_BLOCK, LANES), 1)
    low_head = lane < HEAD_DIM
    n_keys = 2 * ATTN_BLOCK if has_prev else ATTN_BLOCK
    low_head_keys = lax.broadcasted_iota(jnp.int32, (n_keys, LANES), 1) < HEAD_DIM
    contract_last = (((1,), (1,)), ((), ()))

    def item(j, carry):
        r0 = pl.multiple_of(j * ATTN_BLOCK, ATTN_BLOCK)
        if has_prev:
            p0 = pl.multiple_of(jnp.maximum(j - 1, 0) * ATTN_BLOCK, ATTN_BLOCK)
            prev_bias = bias_prev + jnp.where(j % n_blk > 0, 0.0, neg_inf)
        scores = []
        for pair in range(N_SLAB):
            sl = slice(pair * LANES, (pair + 1) * LANES)
            q = q_ref[pl.ds(r0, ATTN_BLOCK), sl]
            zero = jnp.zeros_like(q)
            q2 = jnp.concatenate([jnp.where(low_head, q, zero), jnp.where(low_head, zero, q)], axis=0)
            keys = k_ref[pl.ds(r0, ATTN_BLOCK), sl]
            if has_prev:
                keys = jnp.concatenate([k_ref[pl.ds(p0, ATTN_BLOCK), sl], keys], axis=0)
            s = lax.dot_general(q2, keys, contract_last, preferred_element_type=F32)
            if has_prev:
                s = s + jnp.concatenate([prev_bias, bias_cur], axis=1)
            else:
                s = s + bias_cur
            scores.append(s)
        maxes = [jnp.max(s, axis=1, keepdims=True) for s in scores]
        probs = [jnp.exp(s - m).astype(BF16) for s, m in zip(scores, maxes)]
        stat = jnp.zeros((ATTN_BLOCK, LANES), F32)
        for pair in range(N_SLAB):
            sl = slice(pair * LANES, (pair + 1) * LANES)
            vals = v_ref[pl.ds(r0, ATTN_BLOCK), sl]
            if has_prev:
                vals = jnp.concatenate([v_ref[pl.ds(p0, ATTN_BLOCK), sl], vals], axis=0)
            one = jnp.ones_like(vals)
            r_even = _dot(probs[pair][:ATTN_BLOCK], jnp.where(low_head_keys, vals, one))
            r_odd = _dot(probs[pair][ATTN_BLOCK:], jnp.where(low_head_keys, one, vals))
            acc_ref[pl.ds(r0, ATTN_BLOCK), sl] = jnp.where(low_head, r_even, r_odd).astype(BF16)
            m_even, m_odd = maxes[pair][:ATTN_BLOCK], maxes[pair][ATTN_BLOCK:]
            for h, m_h, r_h in ((2 * pair, m_even, r_even), (2 * pair + 1, m_odd, r_odd)):
                stat = jnp.where(lane == _stat_lane(h), m_h, stat)
                stat = jnp.where(lane == _stat_lane(h) + _DEN_SHIFT, r_h, stat)
        stat_ref[pl.ds(r0, ATTN_BLOCK), :] = stat
        return carry

    lax.fori_loop(0, n_items, item, 0, unroll=4)


def _attention(q, k, v):
    b, d, l, _ = q.shape
    n_blk = l // ATTN_BLOCK
    seq = pl.BlockSpec((None, d * l, GROUP_WIDTH), lambda bi: (bi, 0, 0))
    stat_spec = pl.BlockSpec((None, d * l, LANES), lambda bi: (bi, 0, 0))
    q, k, v = (a.reshape(b, d * l, GROUP_WIDTH) for a in (q, k, v))
    acc, stat = pl.pallas_call(
        functools.partial(_attn_kernel, n_blk=n_blk),
        grid=(b,),
        in_specs=[seq, seq, seq],
        out_specs=[seq, stat_spec],
        out_shape=[jax.ShapeDtypeStruct(q.shape, BF16), jax.ShapeDtypeStruct((b, d * l, LANES), F32)],
        compiler_params=_params(1),
        name=f"band_attention_d{d}",
    )(q, k, v)
    return acc.reshape(b, d, l, GROUP_WIDTH), stat.reshape(b, d, l, LANES)


def _load_token_major(in_ref, scr_ref):
    d, rows, width = in_ref.shape
    if d == 1:
        return in_ref[0].astype(F32)
    n_slab = width // LANES
    for r in range(d):
        x = in_ref[r].astype(F32)
        for j in range(n_slab):
            scr_ref[j, pl.ds(r, rows, stride=d), :] = x[:, j * LANES:(j + 1) * LANES]
    return jnp.concatenate([scr_ref[j] for j in range(n_slab)], axis=1)


def _mixer_out_kernel(a0_ref, a1_ref, a2_ref, s0_ref, s1_ref, s2_ref, ga_ref, gm_ref, h1_ref, expand_ref,
                      wab_ref, wout_ref, g2_ref, b2_ref, wg_ref, wu_ref, wd_ref, g3_ref, b3_ref, out_ref,
                      ascr_ref, sscr_ref):
    stats = [_load_token_major(s_ref, sscr_ref.at[i]) for i, s_ref in enumerate((s0_ref, s1_ref, s2_ref))]
    m = jnp.maximum(jnp.maximum(stats[0], stats[1]), stats[2])
    es = [jnp.exp(x - m) for x in stats]
    dens = [pltpu.roll(x, LANES - _DEN_SHIFT, 1) for x in stats]
    lane = lax.broadcasted_iota(jnp.int32, (1, LANES), 1)
    is_max_lane = functools.reduce(jnp.logical_or, [lane == _stat_lane(h) for h in range(HEADS_PER_GROUP)])
    den = es[0] * dens[0] + es[1] * dens[1] + es[2] * dens[2]
    inv_den = 1.0 / jnp.where(is_max_lane, den, 1.0)
    y = None
    for i, (e, a_ref) in enumerate(zip(es, (a0_ref, a1_ref, a2_ref))):
        w = e * inv_den
        w_hi = w.astype(BF16)
        w_lo = (w - w_hi.astype(F32)).astype(BF16)
        w_wide = _dot(jnp.concatenate([w_hi, w_lo], axis=1), expand_ref[...])
        term = w_wide * _load_token_major(a_ref, ascr_ref.at[i])
        y = term if y is None else y + term
    branch_a = _dot(y.astype(BF16), wab_ref[...])
    merged = ga_ref[...].astype(F32) * branch_a + gm_ref[...].astype(F32)
    mix = _dot(merged.astype(BF16), wout_ref[...])
    h2 = _layer_norm(ALPHA * h1_ref[...] + mix, g2_ref[...], b2_ref[...])
    ffn = _swiglu(h2.astype(BF16), wg_ref, wu_ref, wd_ref)
    out_ref[...] = _layer_norm(ALPHA * h2 + 0.5 * ffn, g3_ref[...], b3_ref[...])


def _mixer_out(accs, stats, ga, gm, h1, expand, wab, wout, g2, b2, wg, wu, wd, g3, b3, tm):
    t = h1.shape[0]
    tiles_per_seq = accs[0].shape[1] * accs[0].shape[2] // tm
    acc_specs = [_residue_major_spec(a.shape[1], tm, GROUP_WIDTH, tiles_per_seq) for a in accs]
    stat_specs = [_residue_major_spec(s.shape[1], tm, LANES, tiles_per_seq) for s in stats]
    return pl.pallas_call(
        _mixer_out_kernel,
        grid=(t // tm,),
        scratch_shapes=[pltpu.VMEM((3, N_SLAB, tm, LANES), F32), pltpu.VMEM((3, 1, tm, LANES), F32)],
        in_specs=acc_specs + stat_specs
                 + [_rows(tm, D_MODEL)] * 3
                 + [_resident((2 * LANES, GROUP_WIDTH)), _resident((GROUP_WIDTH, D_MODEL)),
                    _resident((D_MODEL, D_MODEL)), _resident((1, D_MODEL)), _resident((1, D_MODEL)),
                    _resident((D_MODEL, D_FF)), _resident((D_MODEL, D_FF)), _resident((D_FF, D_MODEL)),
                    _resident((1, D_MODEL)), _resident((1, D_MODEL))],
        out_specs=_rows(tm, D_MODEL),
        out_shape=jax.ShapeDtypeStruct((t, D_MODEL), F32),
        compiler_params=_params(1),
        name="mixer_out_ffn",
    )(*accs, *stats, ga, gm, h1, expand, wab, wout, g2, b2, wg, wu, wd, g3, b3)


def kernel(x, positions, ffn1_w_gate, ffn1_w_up, ffn1_w_down, ln1_g, ln1_b, w_in, b_gates, gmlp_ln_g, gmlp_ln_b, gmlp_w_s, gmlp_b_s, w_attn_branch, w_gmlp_branch, w_out, ln2_g, ln2_b, ffn2_w_gate, ffn2_w_up, ffn2_w_down, ln3_g, ln3_b):
    b, s, d_model = x.shape
    assert d_model == D_MODEL and ln1_g.shape[0] == DEPTH == 1
    t = b * s
    tm = 512
    assert t % tm == 0 and tm % GMLP_CHUNK == 0 and s % tm == 0

    lane = jnp.arange(LANES) % (HEAD_DIM // 2)
    invf = (ROPE_THETA ** (-(2 * lane).astype(F32) / HEAD_DIM)).reshape(1, LANES)
    stat_lane_of_col = jnp.array([_stat_lane(c // HEAD_DIM) for c in range(GROUP_WIDTH)])
    expand = (jnp.arange(LANES)[:, None] == stat_lane_of_col[None, :]).astype(BF16)
    expand = jnp.concatenate([expand, expand], axis=0)
    b_s = jnp.repeat(gmlp_b_s[0].T, GMLP_WIDTH // GMLP_GROUPS, axis=1)

    qkv_cols = [w_in[0][:, base + gi * GROUP_WIDTH:base + (gi + 1) * GROUP_WIDTH]
                for gi in range(N_ATTN_GROUPS) for base in (_Q0, _K0, _V0)]
    w_in_b = jnp.concatenate(qkv_cols + [w_in[0][:, _U0:]], axis=1).astype(BF16)

    h = x.reshape(t, D_MODEL)
    h1 = _ffn_ln(h, ffn1_w_gate[0].astype(BF16), ffn1_w_up[0].astype(BF16), ffn1_w_down[0].astype(BF16),
                 ln1_g, ln1_b, tm)
    outs = _mixer_in(h1, positions.reshape(t, 1), invf, w_in_b, b_gates, gmlp_ln_g, gmlp_ln_b,
                     gmlp_w_s[0], b_s, w_gmlp_branch[0].astype(BF16), b, tm)
    ga, gm = outs[9], outs[10]
    accs, stats = [], []
    for gi in range(N_ATTN_GROUPS):
        acc, stat = _attention(*outs[3 * gi:3 * gi + 3])
        accs.append(acc)
        stats.append(stat)
    out = _mixer_out(accs, stats, ga, gm, h1, expand, w_attn_branch[0].astype(BF16), w_out[0].astype(BF16),
                     ln2_g, ln2_b, ffn2_w_gate[0].astype(BF16), ffn2_w_up[0].astype(BF16),
                     ffn2_w_down[0].astype(BF16), ln3_g, ln3_b, tm)
    return out.reshape(b, s, D_MODEL)
```

```python
import functools
import math

import jax
import jax.numpy as jnp
from jax import lax
from jax.experimental import pallas as pl
from jax.experimental.pallas import tpu as pltpu

D_MODEL = 1024
HEAD_DIM = 64
HEADS_PER_GROUP = 8
ATTN_PATTERNS = ((128, 1), (512, 4), (2048, 16))
N_ATTN_GROUPS = len(ATTN_PATTERNS)
GROUP_WIDTH = HEADS_PER_GROUP * HEAD_DIM
ATTN_WIDTH = N_ATTN_GROUPS * GROUP_WIDTH
ATTN_BLOCK = 128
ROPE_THETA = 10000.0
GMLP_CHUNK = 128
GMLP_GROUPS = 8
GMLP_WIDTH = D_MODEL
D_FF = 2816
DEPTH = 1
ALPHA = (2 * DEPTH) ** 0.25
LN_EPS = 1e-5

LANES = 128
VMEM_LIMIT_BYTES = 60 * 1024 * 1024
N_SLAB = GROUP_WIDTH // LANES

_Q0, _K0, _V0 = 0, ATTN_WIDTH, 2 * ATTN_WIDTH
_U0 = 3 * ATTN_WIDTH
_VG0 = _U0 + GMLP_WIDTH
_GA0 = _VG0 + GMLP_WIDTH
_GM0 = _GA0 + D_MODEL

_Q_SCALE = HEAD_DIM ** -0.5 * math.log2(math.e)

_DEN_SHIFT = 8


def _stat_lane(h):
    return HEAD_DIM + h if h % 2 == 0 else h


BF16 = jnp.bfloat16
F32 = jnp.float32


def _layer_norm(x, g, b):
    mu = jnp.mean(x, axis=-1, keepdims=True)
    xc = x - mu
    var = jnp.mean(xc * xc, axis=-1, keepdims=True)
    return xc * lax.rsqrt(var + LN_EPS) * g + b


def _dot(a, b):
    return jnp.dot(a, b, preferred_element_type=F32)


def _swiglu(xb, wg_ref, wu_ref, wd_ref):
    g = _dot(xb, wg_ref[...])
    u = _dot(xb, wu_ref[...])
    a = g * jax.nn.sigmoid(g) * u
    return _dot(a.astype(BF16), wd_ref[...])


def _gelu(x):
    return 0.5 * x * (1.0 + lax.erf(x * math.sqrt(0.5)))


def _resident(shape):
    return pl.BlockSpec(shape, lambda *_: (0,) * len(shape), pipeline_mode=pl.Buffered(1))


def _rows(tm, width):
    return pl.BlockSpec((tm, width), lambda i: (i, 0))


def _residue_major_spec(d, tm, width, tiles_per_seq):
    return pl.BlockSpec((None, d, tm // d, width), lambda i: (i // tiles_per_seq, 0, i % tiles_per_seq, 0))


def _params(n_axes, allow_input_fusion=None):
    return pltpu.CompilerParams(dimension_semantics=("arbitrary",) * n_axes,
                                vmem_limit_bytes=VMEM_LIMIT_BYTES, allow_input_fusion=allow_input_fusion)


def _ffn_ln_kernel(x_ref, wg_ref, wu_ref, wd_ref, g_ref, b_ref, h_ref, hb_ref):
    x = x_ref[...]
    y = _swiglu(x.astype(BF16), wg_ref, wu_ref, wd_ref)
    h = _layer_norm(ALPHA * x + 0.5 * y, g_ref[...], b_ref[...])
    h_ref[...] = h
    hb_ref[...] = h.astype(BF16)


def _ffn_ln(x, wg, wu, wd, g, b, tm):
    t = x.shape[0]
    return pl.pallas_call(
        _ffn_ln_kernel,
        grid=(t // tm,),
        in_specs=[_rows(tm, D_MODEL), _resident((D_MODEL, D_FF)), _resident((D_MODEL, D_FF)),
                  _resident((D_FF, D_MODEL)), _resident((1, D_MODEL)), _resident((1, D_MODEL))],
        out_specs=[_rows(tm, D_MODEL), _rows(tm, D_MODEL)],
        out_shape=[jax.ShapeDtypeStruct((t, D_MODEL), F32), jax.ShapeDtypeStruct((t, D_MODEL), BF16)],
        compiler_params=_params(1, allow_input_fusion=[False, True, True, True, False, False]),
        name="ffn_ln",
    )(x, wg, wu, wd, g, b)


def _project_slabs(hb_ref, w_ref, col0, dst_ref):
    res = _dot(hb_ref[...], w_ref[:, col0:col0 + GROUP_WIDTH])
    for j in range(N_SLAB):
        dst_ref[j] = res[:, j * LANES:(j + 1) * LANES]


def _emit_residue_major(p_ref, out_ref, cos_ref=None, sin_ref=None):
    d, rows, _ = out_ref.shape
    lane = lax.broadcasted_iota(jnp.int32, (1, LANES), 1)
    first_half = (lane % HEAD_DIM) < (HEAD_DIM // 2)
    for r in range(d):
        rs = pl.ds(r, rows, stride=d) if d > 1 else slice(None)
        if cos_ref is not None:
            c, s = cos_ref[rs, :], sin_ref[rs, :]
        blocks = []
        for j in range(N_SLAB):
            x = p_ref[j, rs, :]
            if cos_ref is not None:
                partner = jnp.where(first_half, pltpu.roll(x, LANES - HEAD_DIM // 2, 1),
                                    pltpu.roll(x, HEAD_DIM // 2, 1))
                x = x * c + partner * s
            blocks.append(x)
        out_ref[r] = jnp.concatenate(blocks, axis=1).astype(out_ref.dtype)


def _mixer_in_kernel(hb_ref, pos_ref, invf_ref, w_ref, bg_ref, lng_ref, lnb_ref, ws_ref, bs_ref, wgb_ref,
                     q0_ref, k0_ref, v0_ref, q1_ref, k1_ref, v1_ref, q2_ref, k2_ref, v2_ref,
                     ga_ref, gm_ref, a_ref, g_ref, p_ref, cs_ref, vgn_ref):
    tm = hb_ref.shape[0]
    qkv_refs = ((q0_ref, k0_ref, v0_ref), (q1_ref, k1_ref, v1_ref), (q2_ref, k2_ref, v2_ref))

    def project_group(gi, slot):
        c0 = gi * GROUP_WIDTH
        for i, base in enumerate((_Q0, _K0, _V0)):
            _project_slabs(hb_ref, w_ref, base + c0, p_ref.at[slot + i])

    def emit_group(gi, slot):
        q_ref, k_ref, v_ref = qkv_refs[gi]
        _emit_residue_major(p_ref.at[slot], q_ref, cs_ref.at[2], cs_ref.at[3])
        _emit_residue_major(p_ref.at[slot + 1], k_ref, cs_ref.at[0], cs_ref.at[1])
        _emit_residue_major(p_ref.at[slot + 2], v_ref)

    hb = hb_ref[...]
    a_ref[0] = _dot(hb, w_ref[:, _U0:_U0 + GMLP_WIDTH])
    a_ref[1] = _dot(hb, w_ref[:, _VG0:_VG0 + GMLP_WIDTH])
    ang = pos_ref[...].astype(F32) * invf_ref[...]
    lane = lax.broadcasted_iota(jnp.int32, (1, LANES), 1)
    first_half = (lane % HEAD_DIM) < (HEAD_DIM // 2)
    cos = jnp.cos(ang)
    sin = jnp.sin(ang)
    sin = jnp.where(first_half, -sin, sin)
    cs_ref[0] = cos
    cs_ref[1] = sin
    cs_ref[2] = cos * _Q_SCALE
    cs_ref[3] = sin * _Q_SCALE

    hb = hb_ref[...]
    g_ref[0] = _dot(hb, w_ref[:, _GM0:_GM0 + D_MODEL]) + bg_ref[:, D_MODEL:]
    g_ref[1] = _dot(hb, w_ref[:, _GA0:_GA0 + D_MODEL]) + bg_ref[:, :D_MODEL]
    vgn_ref[...] = _layer_norm(_gelu(a_ref[1]), lng_ref[...], lnb_ref[...]).astype(BF16)

    project_group(0, 0)
    n_chunk = tm // GMLP_CHUNK
    gdim = GMLP_WIDTH // GMLP_GROUPS
    row = lax.broadcasted_iota(jnp.int32, (GMLP_CHUNK, GMLP_CHUNK), 0)
    col = lax.broadcasted_iota(jnp.int32, (GMLP_CHUNK, GMLP_CHUNK), 1)
    causal = col <= row
    for g in range(GMLP_GROUPS):
        ws = jnp.where(causal, ws_ref[g], 0.0).astype(BF16)
        rhs = jnp.concatenate(
            [vgn_ref[c * GMLP_CHUNK:(c + 1) * GMLP_CHUNK, g * gdim:(g + 1) * gdim] for c in range(n_chunk)],
            axis=1)
        mixed = _dot(ws, rhs)
        for c in range(n_chunk):
            a_ref[1, c * GMLP_CHUNK:(c + 1) * GMLP_CHUNK, g * gdim:(g + 1) * gdim] = (
                mixed[:, c * gdim:(c + 1) * gdim] + bs_ref[:, g * gdim:(g + 1) * gdim])
    a_ref[0] = _gelu(a_ref[0])

    project_group(1, 3)
    vgn_ref[...] = (a_ref[0] * a_ref[1]).astype(BF16)
    emit_group(0, 0)
    ga_ref[...] = jax.nn.sigmoid(g_ref[1]).astype(BF16)

    project_group(2, 0)
    emit_group(1, 3)
    g_ref[0] = jax.nn.sigmoid(g_ref[0])

    gm_ref[...] = (g_ref[0] * _dot(vgn_ref[...], wgb_ref[...])).astype(BF16)
    emit_group(2, 0)


def _mixer_in(hb, pos, invf, w_in, b_gates, ln_g, ln_b, w_s, b_s, w_gb, batch, tm):
    t = hb.shape[0]
    seq = t // batch
    in_width = w_in.shape[1]
    qkv_specs, qkv_shapes = [], []
    for _, d in ATTN_PATTERNS:
        qkv_specs += [_residue_major_spec(d, tm, GROUP_WIDTH, seq // tm)] * 3
        qkv_shapes += [jax.ShapeDtypeStruct((batch, d, seq // d, GROUP_WIDTH), BF16)] * 3
    return pl.pallas_call(
        _mixer_in_kernel,
        grid=(t // tm,),
        in_specs=[_rows(tm, D_MODEL), _rows(tm, 1), _resident((1, LANES)), _resident((D_MODEL, in_width)),
                  _resident((1, 2 * D_MODEL)), _resident((1, GMLP_WIDTH)), _resident((1, GMLP_WIDTH)),
                  _resident((GMLP_GROUPS, GMLP_CHUNK, GMLP_CHUNK)), _resident((GMLP_CHUNK, GMLP_WIDTH)),
                  _resident((GMLP_WIDTH, D_MODEL))],
        out_specs=qkv_specs + [_rows(tm, D_MODEL), _rows(tm, D_MODEL)],
        out_shape=qkv_shapes + [jax.ShapeDtypeStruct((t, D_MODEL), BF16)] * 2,
        scratch_shapes=[pltpu.VMEM((2, tm, GMLP_WIDTH), F32), pltpu.VMEM((2, tm, D_MODEL), F32),
                        pltpu.VMEM((6, N_SLAB, tm, LANES), F32), pltpu.VMEM((4, tm, LANES), F32),
                        pltpu.VMEM((tm, GMLP_WIDTH), BF16)],
        compiler_params=_params(1, allow_input_fusion=[False, False, False, True, False, False, False, False,
                                                       False, True]),
        name="mixer_in",
    )(hb, pos, invf, w_in, b_gates, ln_g, ln_b, w_s, b_s, w_gb)


def _attn_kernel(q_ref, k_ref, v_ref, acc_ref, stat_ref, *, n_blk):
    n_items = q_ref.shape[0] // ATTN_BLOCK
    has_prev = n_blk > 1
    row = lax.broadcasted_iota(jnp.int32, (2 * ATTN_BLOCK, ATTN_BLOCK), 0) % ATTN_BLOCK
    col = lax.broadcasted_iota(jnp.int32, (2 * ATTN_BLOCK, ATTN_BLOCK), 1)
    neg_inf = jnp.float32(-jnp.inf)
    bias_cur = jnp.where(col <= row, 0.0, neg_inf)
    bias_prev = jnp.where(col >= row, 0.0, neg_inf)
    lane = lax.broadcasted_iota(jnp.int32, (ATTN_BLOCK, LANES), 1)
    low_head = lane < HEAD_DIM
    n_keys = 2 * ATTN_BLOCK if has_prev else ATTN_BLOCK
    low_head_keys = lax.broadcasted_iota(jnp.int32, (n_keys, LANES), 1) < HEAD_DIM
    contract_last = (((1,), (1,)), ((), ()))

    def item(j, carry):
        r0 = pl.multiple_of(j * ATTN_BLOCK, ATTN_BLOCK)
        if has_prev:
            p0 = pl.multiple_of(jnp.maximum(j - 1, 0) * ATTN_BLOCK, ATTN_BLOCK)
            prev_bias = bias_prev + jnp.where(j % n_blk > 0, 0.0, neg_inf)
        scores = []
        for pair in range(N_SLAB):
            sl = slice(pair * LANES, (pair + 1) * LANES)
            q = q_ref[pl.ds(r0, ATTN_BLOCK), sl]
            zero = jnp.zeros_like(q)
            q2 = jnp.concatenate([jnp.where(low_head, q, zero), jnp.where(low_head, zero, q)], axis=0)
            keys = k_ref[pl.ds(r0, ATTN_BLOCK), sl]
            if has_prev:
                keys = jnp.concatenate([k_ref[pl.ds(p0, ATTN_BLOCK), sl], keys], axis=0)
            s = lax.dot_general(q2, keys, contract_last, preferred_element_type=F32)
            if has_prev:
                s = s + jnp.concatenate([prev_bias, bias_cur], axis=1)
            else:
                s = s + bias_cur
            scores.append(s)
        maxes = [jnp.max(s, axis=1, keepdims=True) for s in scores]
        probs = [jnp.exp2(s - m).astype(BF16) for s, m in zip(scores, maxes)]
        stat = jnp.zeros((ATTN_BLOCK, LANES), F32)
        for pair in range(N_SLAB):
            sl = slice(pair * LANES, (pair + 1) * LANES)
            vals = v_ref[pl.ds(r0, ATTN_BLOCK), sl]
            if has_prev:
                vals = jnp.concatenate([v_ref[pl.ds(p0, ATTN_BLOCK), sl], vals], axis=0)
            one = jnp.ones_like(vals)
            r_even = _dot(probs[pair][:ATTN_BLOCK], jnp.where(low_head_keys, vals, one))
            r_odd = _dot(probs[pair][ATTN_BLOCK:], jnp.where(low_head_keys, one, vals))
            acc_ref[pl.ds(r0, ATTN_BLOCK), sl] = jnp.where(low_head, r_even, r_odd).astype(BF16)
            m_even, m_odd = maxes[pair][:ATTN_BLOCK], maxes[pair][ATTN_BLOCK:]
            for h, m_h, r_h in ((2 * pair, m_even, r_even), (2 * pair + 1, m_odd, r_odd)):
                stat = jnp.where(lane == _stat_lane(h), m_h, stat)
                stat = jnp.where(lane == _stat_lane(h) + _DEN_SHIFT, r_h, stat)
        stat_ref[pl.ds(r0, ATTN_BLOCK), :] = stat
        return carry

    lax.fori_loop(0, n_items, item, 0, unroll=4)


def _attention(q, k, v):
    b, d, l, _ = q.shape
    n_blk = l // ATTN_BLOCK
    seq = pl.BlockSpec((None, d * l, GROUP_WIDTH), lambda bi: (bi, 0, 0))
    stat_spec = pl.BlockSpec((None, d * l, LANES), lambda bi: (bi, 0, 0))
    q, k, v = (a.reshape(b, d * l, GROUP_WIDTH) for a in (q, k, v))
    acc, stat = pl.pallas_call(
        functools.partial(_attn_kernel, n_blk=n_blk),
        grid=(b,),
        in_specs=[seq, seq, seq],
        out_specs=[seq, stat_spec],
        out_shape=[jax.ShapeDtypeStruct(q.shape, BF16), jax.ShapeDtypeStruct((b, d * l, LANES), F32)],
        compiler_params=_params(1),
        name=f"band_attention_d{d}",
    )(q, k, v)
    return acc.reshape(b, d, l, GROUP_WIDTH), stat.reshape(b, d, l, LANES)


def _load_token_major(in_ref, scr_ref):
    d, rows, width = in_ref.shape
    if d == 1:
        return in_ref[0].astype(F32)
    n_slab = width // LANES
    for r in range(d):
        x = in_ref[r].astype(F32)
        for j in range(n_slab):
            scr_ref[j, pl.ds(r, rows, stride=d), :] = x[:, j * LANES:(j + 1) * LANES]
    return jnp.concatenate([scr_ref[j] for j in range(n_slab)], axis=1)


def _mixer_out_kernel(a0_ref, a1_ref, a2_ref, s0_ref, s1_ref, s2_ref, ga_ref, gm_ref, h1_ref, expand_ref,
                      wab_ref, wout_ref, g2_ref, b2_ref, wg_ref, wu_ref, wd_ref, g3_ref, b3_ref, out_ref,
                      ascr_ref, sscr_ref):
    stats = [_load_token_major(s_ref, sscr_ref.at[i]) for i, s_ref in enumerate((s0_ref, s1_ref, s2_ref))]
    m = jnp.maximum(jnp.maximum(stats[0], stats[1]), stats[2])
    es = [jnp.exp2(x - m) for x in stats]
    dens = [pltpu.roll(x, LANES - _DEN_SHIFT, 1) for x in stats]
    lane = lax.broadcasted_iota(jnp.int32, (1, LANES), 1)
    is_max_lane = functools.reduce(jnp.logical_or, [lane == _stat_lane(h) for h in range(HEADS_PER_GROUP)])
    den = es[0] * dens[0] + es[1] * dens[1] + es[2] * dens[2]
    inv_den = 1.0 / jnp.where(is_max_lane, den, 1.0)
    y = None
    for i, (e, a_ref) in enumerate(zip(es, (a0_ref, a1_ref, a2_ref))):
        w = e * inv_den
        w_hi = w.astype(BF16)
        w_lo = (w - w_hi.astype(F32)).astype(BF16)
        w_wide = _dot(jnp.concatenate([w_hi, w_lo], axis=1), expand_ref[...])
        term = w_wide * _load_token_major(a_ref, ascr_ref.at[i])
        y = term if y is None else y + term
    branch_a = _dot(y.astype(BF16), wab_ref[...])
    merged = ga_ref[...].astype(F32) * branch_a + gm_ref[...].astype(F32)
    mix = _dot(merged.astype(BF16), wout_ref[...])
    h2 = _layer_norm(ALPHA * h1_ref[...] + mix, g2_ref[...], b2_ref[...])
    ffn = _swiglu(h2.astype(BF16), wg_ref, wu_ref, wd_ref)
    out_ref[...] = _layer_norm(ALPHA * h2 + 0.5 * ffn, g3_ref[...], b3_ref[...])


def _mixer_out(accs, stats, ga, gm, h1, expand, wab, wout, g2, b2, wg, wu, wd, g3, b3, tm):
    t = h1.shape[0]
    tiles_per_seq = accs[0].shape[1] * accs[0].shape[2] // tm
    acc_specs = [_residue_major_spec(a.shape[1], tm, GROUP_WIDTH, tiles_per_seq) for a in accs]
    stat_specs = [_residue_major_spec(s.shape[1], tm, LANES, tiles_per_seq) for s in stats]
    return pl.pallas_call(
        _mixer_out_kernel,
        grid=(t // tm,),
        scratch_shapes=[pltpu.VMEM((3, N_SLAB, tm, LANES), F32), pltpu.VMEM((3, 1, tm, LANES), F32)],
        in_specs=acc_specs + stat_specs
                 + [_rows(tm, D_MODEL)] * 3
                 + [_resident((2 * LANES, GROUP_WIDTH)), _resident((GROUP_WIDTH, D_MODEL)),
                    _resident((D_MODEL, D_MODEL)), _resident((1, D_MODEL)), _resident((1, D_MODEL)),
                    _resident((D_MODEL, D_FF)), _resident((D_MODEL, D_FF)), _resident((D_FF, D_MODEL)),
                    _resident((1, D_MODEL)), _resident((1, D_MODEL))],
        out_specs=_rows(tm, D_MODEL),
        out_shape=jax.ShapeDtypeStruct((t, D_MODEL), F32),
        compiler_params=_params(1, allow_input_fusion=[False] * 10 + [True, True, False, False, True, True, True,
                                                                      False, False]),
        name="mixer_out_ffn",
    )(*accs, *stats, ga, gm, h1, expand, wab, wout, g2, b2, wg, wu, wd, g3, b3)


def kernel(x, positions, ffn1_w_gate, ffn1_w_up, ffn1_w_down, ln1_g, ln1_b, w_in, b_gates, gmlp_ln_g, gmlp_ln_b, gmlp_w_s, gmlp_b_s, w_attn_branch, w_gmlp_branch, w_out, ln2_g, ln2_b, ffn2_w_gate, ffn2_w_up, ffn2_w_down, ln3_g, ln3_b):
    b, s, d_model = x.shape
    assert d_model == D_MODEL and ln1_g.shape[0] == DEPTH == 1
    t = b * s
    tm = 512
    assert t % tm == 0 and tm % GMLP_CHUNK == 0 and s % tm == 0

    lane = jnp.arange(LANES) % (HEAD_DIM // 2)
    invf = (ROPE_THETA ** (-(2 * lane).astype(F32) / HEAD_DIM)).reshape(1, LANES)
    stat_lane_of_col = jnp.array([_stat_lane(c // HEAD_DIM) for c in range(GROUP_WIDTH)])
    expand = (jnp.arange(LANES)[:, None] == stat_lane_of_col[None, :]).astype(BF16)
    expand = jnp.concatenate([expand, expand], axis=0)
    b_s = jnp.repeat(gmlp_b_s[0].T, GMLP_WIDTH // GMLP_GROUPS, axis=1)

    h = x.reshape(t, D_MODEL)
    h1, h1b = _ffn_ln(h, ffn1_w_gate[0].astype(BF16), ffn1_w_up[0].astype(BF16), ffn1_w_down[0].astype(BF16),
                      ln1_g, ln1_b, tm)
    outs = _mixer_in(h1b, positions.reshape(t, 1), invf, w_in[0].astype(BF16), b_gates, gmlp_ln_g, gmlp_ln_b,
                     gmlp_w_s[0], b_s, w_gmlp_branch[0].astype(BF16), b, tm)
    ga, gm = outs[9], outs[10]
    accs, stats = [], []
    for gi in range(N_ATTN_GROUPS):
        acc, stat = _attention(*outs[3 * gi:3 * gi + 3])
        accs.append(acc)
        stats.append(stat)
    out = _mixer_out(accs, stats, ga, gm, h1, expand, w_attn_branch[0].astype(BF16), w_out[0].astype(BF16),
                     ln2_g, ln2_b, ffn2_w_gate[0].astype(BF16), ffn2_w_up[0].astype(BF16),
                     ffn2_w_down[0].astype(BF16), ln3_g, ln3_b, tm)
    return out.reshape(b, s, D_MODEL)
```

```python
import functools
import math

import jax
import jax.numpy as jnp
from jax import lax
from jax.experimental import pallas as pl
from jax.experimental.pallas import tpu as pltpu

D_MODEL = 1024
HEAD_DIM = 64
HEADS_PER_GROUP = 8
ATTN_PATTERNS = ((128, 1), (512, 4), (2048, 16))
N_ATTN_GROUPS = len(ATTN_PATTERNS)
GROUP_WIDTH = HEADS_PER_GROUP * HEAD_DIM
ATTN_WIDTH = N_ATTN_GROUPS * GROUP_WIDTH
ATTN_BLOCK = 128
ROPE_THETA = 10000.0
GMLP_CHUNK = 128
GMLP_GROUPS = 8
GMLP_WIDTH = D_MODEL
D_FF = 2816
DEPTH = 1
ALPHA = (2 * DEPTH) ** 0.25
LN_EPS = 1e-5

LANES = 128
VMEM_LIMIT_BYTES = 60 * 1024 * 1024
N_SLAB = GROUP_WIDTH // LANES

_Q0, _K0, _V0 = 0, ATTN_WIDTH, 2 * ATTN_WIDTH
_U0 = 3 * ATTN_WIDTH
_VG0 = _U0 + GMLP_WIDTH
_GA0 = _VG0 + GMLP_WIDTH
_GM0 = _GA0 + D_MODEL

_Q_SCALE = HEAD_DIM ** -0.5 * math.log2(math.e)

_DEN_SHIFT = 8


def _stat_lane(h):
    return HEAD_DIM + h if h % 2 == 0 else h


BF16 = jnp.bfloat16
F32 = jnp.float32


def _layer_norm(x, g, b):
    mu = jnp.mean(x, axis=-1, keepdims=True)
    xc = x - mu
    var = jnp.mean(xc * xc, axis=-1, keepdims=True)
    return xc * lax.rsqrt(var + LN_EPS) * g + b


def _dot(a, b):
    return jnp.dot(a, b, preferred_element_type=F32)


def _swiglu(xb, wg_ref, wu_ref, wd_ref):
    g = _dot(xb, wg_ref[...])
    u = _dot(xb, wu_ref[...])
    a = g * jax.nn.sigmoid(g) * u
    return _dot(a.astype(BF16), wd_ref[...])


def _gelu(x):
    return 0.5 * x * (1.0 + lax.erf(x * math.sqrt(0.5)))


def _resident(shape):
    return pl.BlockSpec(shape, lambda *_: (0,) * len(shape), pipeline_mode=pl.Buffered(1))


def _rows(tm, width):
    return pl.BlockSpec((tm, width), lambda i: (i, 0))


def _residue_major_spec(d, tm, width, tiles_per_seq):
    return pl.BlockSpec((None, d, tm // d, width), lambda i: (i // tiles_per_seq, 0, i % tiles_per_seq, 0))


def _params(n_axes):
    return pltpu.CompilerParams(dimension_semantics=("arbitrary",) * n_axes,
                                vmem_limit_bytes=VMEM_LIMIT_BYTES)


_HBM = pl.BlockSpec(memory_space=pl.ANY)


def _fill_bf16(w_hbm, w_vmem, stage, sem, rows):
    k, n = w_hbm.shape
    assert k % rows == 0 and rows <= stage.shape[1] and n <= stage.shape[2]
    n_blk = k // rows

    def copy(i):
        dst = stage.at[i % 2, pl.ds(0, rows), pl.ds(0, n)]
        return pltpu.make_async_copy(w_hbm.at[pl.ds(i * rows, rows), :], dst, sem.at[i % 2])

    copy(0).start()
    for i in range(n_blk):
        if i + 1 < n_blk:
            copy(i + 1).start()
        copy(i).wait()
        w_vmem[i * rows:(i + 1) * rows, :] = stage[i % 2, 0:rows, 0:n].astype(BF16)


def _weight_scratch(shapes, stage_rows, stage_cols):
    return ([pltpu.VMEM(s, BF16) for s in shapes]
            + [pltpu.VMEM((2, stage_rows, stage_cols), F32), pltpu.SemaphoreType.DMA((2,))])


FFN_FILL_ROWS = 128


def _ffn_ln_kernel(x_ref, wg_hbm, wu_hbm, wd_hbm, g_ref, b_ref, h_ref, hb_ref,
                   wg_ref, wu_ref, wd_ref, stage_ref, sem):
    @pl.when(pl.program_id(0) == 0)
    def _():
        for w_hbm, w_ref in ((wg_hbm, wg_ref), (wu_hbm, wu_ref), (wd_hbm, wd_ref)):
            _fill_bf16(w_hbm, w_ref, stage_ref, sem, FFN_FILL_ROWS)

    x = x_ref[...]
    y = _swiglu(x.astype(BF16), wg_ref, wu_ref, wd_ref)
    h = _layer_norm(ALPHA * x + 0.5 * y, g_ref[...], b_ref[...])
    h_ref[...] = h
    hb_ref[...] = h.astype(BF16)


def _ffn_ln(x, wg, wu, wd, g, b, tm):
    t = x.shape[0]
    return pl.pallas_call(
        _ffn_ln_kernel,
        grid=(t // tm,),
        in_specs=[_rows(tm, D_MODEL), _HBM, _HBM, _HBM, _resident((1, D_MODEL)), _resident((1, D_MODEL))],
        out_specs=[_rows(tm, D_MODEL), _rows(tm, D_MODEL)],
        out_shape=[jax.ShapeDtypeStruct((t, D_MODEL), F32), jax.ShapeDtypeStruct((t, D_MODEL), BF16)],
        scratch_shapes=_weight_scratch([(D_MODEL, D_FF), (D_MODEL, D_FF), (D_FF, D_MODEL)], FFN_FILL_ROWS, D_FF),
        compiler_params=_params(1),
        name="ffn_ln",
    )(x, wg, wu, wd, g, b)


def _project_slabs(hb_ref, w_ref, col0, dst_ref):
    res = _dot(hb_ref[...], w_ref[:, col0:col0 + GROUP_WIDTH])
    for j in range(N_SLAB):
        dst_ref[j] = res[:, j * LANES:(j + 1) * LANES]


def _emit_residue_major(p_ref, out_ref, cos_ref=None, sin_ref=None):
    d, rows, _ = out_ref.shape
    lane = lax.broadcasted_iota(jnp.int32, (1, LANES), 1)
    first_half = (lane % HEAD_DIM) < (HEAD_DIM // 2)
    for r in range(d):
        rs = pl.ds(r, rows, stride=d) if d > 1 else slice(None)
        if cos_ref is not None:
            c, s = cos_ref[rs, :], sin_ref[rs, :]
        blocks = []
        for j in range(N_SLAB):
            x = p_ref[j, rs, :]
            if cos_ref is not None:
                partner = jnp.where(first_half, pltpu.roll(x, LANES - HEAD_DIM // 2, 1),
                                    pltpu.roll(x, HEAD_DIM // 2, 1))
                x = x * c + partner * s
            blocks.append(x)
        out_ref[r] = jnp.concatenate(blocks, axis=1).astype(out_ref.dtype)


IN_FILL_ROWS = 32


def _mixer_in_kernel(hb_ref, pos_ref, invf_ref, w_hbm, bg_ref, lng_ref, lnb_ref, ws_ref, bs_ref, wgb_hbm,
                     q0_ref, k0_ref, v0_ref, q1_ref, k1_ref, v1_ref, q2_ref, k2_ref, v2_ref,
                     ga_ref, gm_ref, a_ref, g_ref, p_ref, cs_ref, vgn_ref, w_ref, wgb_ref, stage_ref, sem):
    tm = hb_ref.shape[0]
    qkv_refs = ((q0_ref, k0_ref, v0_ref), (q1_ref, k1_ref, v1_ref), (q2_ref, k2_ref, v2_ref))

    @pl.when(pl.program_id(0) == 0)
    def _():
        _fill_bf16(w_hbm, w_ref, stage_ref, sem, IN_FILL_ROWS)
        _fill_bf16(wgb_hbm, wgb_ref, stage_ref, sem, IN_FILL_ROWS)

    def project_group(gi, slot):
        c0 = gi * GROUP_WIDTH
        for i, base in enumerate((_Q0, _K0, _V0)):
            _project_slabs(hb_ref, w_ref, base + c0, p_ref.at[slot + i])

    def emit_group(gi, slot):
        q_ref, k_ref, v_ref = qkv_refs[gi]
        _emit_residue_major(p_ref.at[slot], q_ref, cs_ref.at[2], cs_ref.at[3])
        _emit_residue_major(p_ref.at[slot + 1], k_ref, cs_ref.at[0], cs_ref.at[1])
        _emit_residue_major(p_ref.at[slot + 2], v_ref)

    hb = hb_ref[...]
    a_ref[0] = _dot(hb, w_ref[:, _U0:_U0 + GMLP_WIDTH])
    a_ref[1] = _dot(hb, w_ref[:, _VG0:_VG0 + GMLP_WIDTH])
    ang = pos_ref[...].astype(F32) * invf_ref[...]
    lane = lax.broadcasted_iota(jnp.int32, (1, LANES), 1)
    first_half = (lane % HEAD_DIM) < (HEAD_DIM // 2)
    cos = jnp.cos(ang)
    sin = jnp.sin(ang)
    sin = jnp.where(first_half, -sin, sin)
    cs_ref[0] = cos
    cs_ref[1] = sin
    cs_ref[2] = cos * _Q_SCALE
    cs_ref[3] = sin * _Q_SCALE

    hb = hb_ref[...]
    g_ref[0] = _dot(hb, w_ref[:, _GM0:_GM0 + D_MODEL]) + bg_ref[:, D_MODEL:]
    g_ref[1] = _dot(hb, w_ref[:, _GA0:_GA0 + D_MODEL]) + bg_ref[:, :D_MODEL]
    vgn_ref[...] = _layer_norm(_gelu(a_ref[1]), lng_ref[...], lnb_ref[...]).astype(BF16)

    project_group(0, 0)
    n_chunk = tm // GMLP_CHUNK
    gdim = GMLP_WIDTH // GMLP_GROUPS
    row = lax.broadcasted_iota(jnp.int32, (GMLP_CHUNK, GMLP_CHUNK), 0)
    col = lax.broadcasted_iota(jnp.int32, (GMLP_CHUNK, GMLP_CHUNK), 1)
    causal = col <= row
    for g in range(GMLP_GROUPS):
        ws = jnp.where(causal, ws_ref[g], 0.0).astype(BF16)
        rhs = jnp.concatenate(
            [vgn_ref[c * GMLP_CHUNK:(c + 1) * GMLP_CHUNK, g * gdim:(g + 1) * gdim] for c in range(n_chunk)],
            axis=1)
        mixed = _dot(ws, rhs)
        for c in range(n_chunk):
            a_ref[1, c * GMLP_CHUNK:(c + 1) * GMLP_CHUNK, g * gdim:(g + 1) * gdim] = (
                mixed[:, c * gdim:(c + 1) * gdim] + bs_ref[:, g * gdim:(g + 1) * gdim])
    a_ref[0] = _gelu(a_ref[0])

    project_group(1, 3)
    vgn_ref[...] = (a_ref[0] * a_ref[1]).astype(BF16)
    emit_group(0, 0)
    ga_ref[...] = jax.nn.sigmoid(g_ref[1]).astype(BF16)

    project_group(2, 0)
    emit_group(1, 3)
    g_ref[0] = jax.nn.sigmoid(g_ref[0])

    gm_ref[...] = (g_ref[0] * _dot(vgn_ref[...], wgb_ref[...])).astype(BF16)
    emit_group(2, 0)


def _mixer_in(hb, pos, invf, w_in, b_gates, ln_g, ln_b, w_s, b_s, w_gb, batch, tm):
    t = hb.shape[0]
    seq = t // batch
    in_width = w_in.shape[1]
    qkv_specs, qkv_shapes = [], []
    for _, d in ATTN_PATTERNS:
        qkv_specs += [_residue_major_spec(d, tm, GROUP_WIDTH, seq // tm)] * 3
        qkv_shapes += [jax.ShapeDtypeStruct((batch, d, seq // d, GROUP_WIDTH), BF16)] * 3
    return pl.pallas_call(
        _mixer_in_kernel,
        grid=(t // tm,),
        in_specs=[_rows(tm, D_MODEL), _rows(tm, 1), _resident((1, LANES)), _HBM,
                  _resident((1, 2 * D_MODEL)), _resident((1, GMLP_WIDTH)), _resident((1, GMLP_WIDTH)),
                  _resident((GMLP_GROUPS, GMLP_CHUNK, GMLP_CHUNK)), _resident((GMLP_CHUNK, GMLP_WIDTH)),
                  _HBM],
        out_specs=qkv_specs + [_rows(tm, D_MODEL), _rows(tm, D_MODEL)],
        out_shape=qkv_shapes + [jax.ShapeDtypeStruct((t, D_MODEL), BF16)] * 2,
        scratch_shapes=[pltpu.VMEM((2, tm, GMLP_WIDTH), F32), pltpu.VMEM((2, tm, D_MODEL), F32),
                        pltpu.VMEM((6, N_SLAB, tm, LANES), F32), pltpu.VMEM((4, tm, LANES), F32),
                        pltpu.VMEM((tm, GMLP_WIDTH), BF16)]
                       + _weight_scratch([(D_MODEL, in_width), (GMLP_WIDTH, D_MODEL)], IN_FILL_ROWS, in_width),
        compiler_params=_params(1),
        name="mixer_in",
    )(hb, pos, invf, w_in, b_gates, ln_g, ln_b, w_s, b_s, w_gb)


def _attn_kernel(q_ref, k_ref, v_ref, acc_ref, stat_ref, *, n_blk):
    n_items = q_ref.shape[0] // ATTN_BLOCK
    has_prev = n_blk > 1
    row = lax.broadcasted_iota(jnp.int32, (2 * ATTN_BLOCK, ATTN_BLOCK), 0) % ATTN_BLOCK
    col = lax.broadcasted_iota(jnp.int32, (2 * ATTN_BLOCK, ATTN_BLOCK), 1)
    neg_inf = jnp.float32(-jnp.inf)
    bias_cur = jnp.where(col <= row, 0.0, neg_inf)
    bias_prev = jnp.where(col >= row, 0.0, neg_inf)
    lane = lax.broadcasted_iota(jnp.int32, (ATTN_BLOCK, LANES), 1)
    low_head = lane < HEAD_DIM
    n_keys = 2 * ATTN_BLOCK if has_prev else ATTN_BLOCK
    low_head_keys = lax.broadcasted_iota(jnp.int32, (n_keys, LANES), 1) < HEAD_DIM
    contract_last = (((1,), (1,)), ((), ()))

    def item(j, carry):
        r0 = pl.multiple_of(j * ATTN_BLOCK, ATTN_BLOCK)
        if has_prev:
            p0 = pl.multiple_of(jnp.maximum(j - 1, 0) * ATTN_BLOCK, ATTN_BLOCK)
            prev_bias = bias_prev + jnp.where(j % n_blk > 0, 0.0, neg_inf)
        scores = []
        for pair in range(N_SLAB):
            sl = slice(pair * LANES, (pair + 1) * LANES)
            q = q_ref[pl.ds(r0, ATTN_BLOCK), sl]
            zero = jnp.zeros_like(q)
            q2 = jnp.concatenate([jnp.where(low_head, q, zero), jnp.where(low_head, zero, q)], axis=0)
            keys = k_ref[pl.ds(r0, ATTN_BLOCK), sl]
            if has_prev:
                keys = jnp.concatenate([k_ref[pl.ds(p0, ATTN_BLOCK), sl], keys], axis=0)
            s = lax.dot_general(q2, keys, contract_last, preferred_element_type=F32)
            if has_prev:
                s = s + jnp.concatenate([prev_bias, bias_cur], axis=1)
            else:
                s = s + bias_cur
            scores.append(s)
        maxes = [jnp.max(s, axis=1, keepdims=True) for s in scores]
        probs = [jnp.exp2(s - m).astype(BF16) for s, m in zip(scores, maxes)]
        stat = jnp.zeros((ATTN_BLOCK, LANES), F32)
        for pair in range(N_SLAB):
            sl = slice(pair * LANES, (pair + 1) * LANES)
            vals = v_ref[pl.ds(r0, ATTN_BLOCK), sl]
            if has_prev:
                vals = jnp.concatenate([v_ref[pl.ds(p0, ATTN_BLOCK), sl], vals], axis=0)
            one = jnp.ones_like(vals)
            r_even = _dot(probs[pair][:ATTN_BLOCK], jnp.where(low_head_keys, vals, one))
            r_odd = _dot(probs[pair][ATTN_BLOCK:], jnp.where(low_head_keys, one, vals))
            acc_ref[pl.ds(r0, ATTN_BLOCK), sl] = jnp.where(low_head, r_even, r_odd).astype(BF16)
            m_even, m_odd = maxes[pair][:ATTN_BLOCK], maxes[pair][ATTN_BLOCK:]
            for h, m_h, r_h in ((2 * pair, m_even, r_even), (2 * pair + 1, m_odd, r_odd)):
                stat = jnp.where(lane == _stat_lane(h), m_h, stat)
                stat = jnp.where(lane == _stat_lane(h) + _DEN_SHIFT, r_h, stat)
        stat_ref[pl.ds(r0, ATTN_BLOCK), :] = stat
        return carry

    lax.fori_loop(0, n_items, item, 0, unroll=4)


def _attention(q, k, v):
    b, d, l, _ = q.shape
    n_blk = l // ATTN_BLOCK
    seq = pl.BlockSpec((None, d * l, GROUP_WIDTH), lambda bi: (bi, 0, 0))
    stat_spec = pl.BlockSpec((None, d * l, LANES), lambda bi: (bi, 0, 0))
    q, k, v = (a.reshape(b, d * l, GROUP_WIDTH) for a in (q, k, v))
    acc, stat = pl.pallas_call(
        functools.partial(_attn_kernel, n_blk=n_blk),
        grid=(b,),
        in_specs=[seq, seq, seq],
        out_specs=[seq, stat_spec],
        out_shape=[jax.ShapeDtypeStruct(q.shape, BF16), jax.ShapeDtypeStruct((b, d * l, LANES), F32)],
        compiler_params=_params(1),
        name=f"band_attention_d{d}",
    )(q, k, v)
    return acc.reshape(b, d, l, GROUP_WIDTH), stat.reshape(b, d, l, LANES)


def _load_token_major(in_ref, scr_ref):
    d, rows, width = in_ref.shape
    if d == 1:
        return in_ref[0].astype(F32)
    n_slab = width // LANES
    for r in range(d):
        x = in_ref[r].astype(F32)
        for j in range(n_slab):
            scr_ref[j, pl.ds(r, rows, stride=d), :] = x[:, j * LANES:(j + 1) * LANES]
    return jnp.concatenate([scr_ref[j] for j in range(n_slab)], axis=1)


def _mixer_out_kernel(a0_ref, a1_ref, a2_ref, s0_ref, s1_ref, s2_ref, ga_ref, gm_ref, h1_ref, expand_ref,
                      wab_hbm, wout_hbm, g2_ref, b2_ref, wg_hbm, wu_hbm, wd_hbm, g3_ref, b3_ref, out_ref,
                      ascr_ref, sscr_ref, wab_ref, wout_ref, wg_ref, wu_ref, wd_ref, stage_ref, sem):
    @pl.when(pl.program_id(0) == 0)
    def _():
        for w_hbm, w_ref in ((wab_hbm, wab_ref), (wout_hbm, wout_ref), (wg_hbm, wg_ref), (wu_hbm, wu_ref),
                             (wd_hbm, wd_ref)):
            _fill_bf16(w_hbm, w_ref, stage_ref, sem, FFN_FILL_ROWS)

    tm = h1_ref.shape[0]
    stats = [_load_token_major(s_ref, sscr_ref.at[i]) for i, s_ref in enumerate((s0_ref, s1_ref, s2_ref))]
    accs = [_load_token_major(a_ref, ascr_ref.at[i]) for i, a_ref in enumerate((a0_ref, a1_ref, a2_ref))]
    lane = lax.broadcasted_iota(jnp.int32, (1, LANES), 1)
    is_max_lane = functools.reduce(jnp.logical_or, [lane == _stat_lane(h) for h in range(HEADS_PER_GROUP)])

    def mix_rows(rows):
        st = [x[rows] for x in stats]
        m = jnp.maximum(jnp.maximum(st[0], st[1]), st[2])
        es = [jnp.exp2(x - m) for x in st]
        dens = [pltpu.roll(x, LANES - _DEN_SHIFT, 1) for x in st]
        den = es[0] * dens[0] + es[1] * dens[1] + es[2] * dens[2]
        inv_den = 1.0 / jnp.where(is_max_lane, den, 1.0)
        y = None
        for e, acc in zip(es, accs):
            w = e * inv_den
            w_hi = w.astype(BF16)
            w_lo = (w - w_hi.astype(F32)).astype(BF16)
            w_wide = _dot(jnp.concatenate([w_hi, w_lo], axis=1), expand_ref[...])
            term = w_wide * acc[rows]
            y = term if y is None else y + term
        branch_a = _dot(y.astype(BF16), wab_ref[...])
        merged = ga_ref[rows, :].astype(F32) * branch_a + gm_ref[rows, :].astype(F32)
        mix = _dot(merged.astype(BF16), wout_ref[...])
        return _layer_norm(ALPHA * h1_ref[rows, :] + mix, g2_ref[...], b2_ref[...])

    h2 = mix_rows(slice(0, tm))
    ffn = _swiglu(h2.astype(BF16), wg_ref, wu_ref, wd_ref)
    out_ref[...] = _layer_norm(ALPHA * h2 + 0.5 * ffn, g3_ref[...], b3_ref[...])


def _mixer_out(accs, stats, ga, gm, h1, expand, wab, wout, g2, b2, wg, wu, wd, g3, b3, tm):
    t = h1.shape[0]
    tiles_per_seq = accs[0].shape[1] * accs[0].shape[2] // tm
    acc_specs = [_residue_major_spec(a.shape[1], tm, GROUP_WIDTH, tiles_per_seq) for a in accs]
    stat_specs = [_residue_major_spec(s.shape[1], tm, LANES, tiles_per_seq) for s in stats]
    return pl.pallas_call(
        _mixer_out_kernel,
        grid=(t // tm,),
        scratch_shapes=[pltpu.VMEM((3, N_SLAB, tm, LANES), F32), pltpu.VMEM((3, 1, tm, LANES), F32)]
                       + _weight_scratch([(GROUP_WIDTH, D_MODEL), (D_MODEL, D_MODEL), (D_MODEL, D_FF),
                                          (D_MODEL, D_FF), (D_FF, D_MODEL)], FFN_FILL_ROWS, D_FF),
        in_specs=acc_specs + stat_specs
                 + [_rows(tm, D_MODEL)] * 3
                 + [_resident((2 * LANES, GROUP_WIDTH)), _HBM, _HBM, _resident((1, D_MODEL)),
                    _resident((1, D_MODEL)), _HBM, _HBM, _HBM, _resident((1, D_MODEL)), _resident((1, D_MODEL))],
        out_specs=_rows(tm, D_MODEL),
        out_shape=jax.ShapeDtypeStruct((t, D_MODEL), F32),
        compiler_params=_params(1),
        name="mixer_out_ffn",
    )(*accs, *stats, ga, gm, h1, expand, wab, wout, g2, b2, wg, wu, wd, g3, b3)


def kernel(x, positions, ffn1_w_gate, ffn1_w_up, ffn1_w_down, ln1_g, ln1_b, w_in, b_gates, gmlp_ln_g, gmlp_ln_b, gmlp_w_s, gmlp_b_s, w_attn_branch, w_gmlp_branch, w_out, ln2_g, ln2_b, ffn2_w_gate, ffn2_w_up, ffn2_w_down, ln3_g, ln3_b):
    b, s, d_model = x.shape
    assert d_model == D_MODEL and ln1_g.shape[0] == DEPTH == 1
    t = b * s
    tm = 512
    assert t % tm == 0 and tm % GMLP_CHUNK == 0 and s % tm == 0

    lane = jnp.arange(LANES) % (HEAD_DIM // 2)
    invf = (ROPE_THETA ** (-(2 * lane).astype(F32) / HEAD_DIM)).reshape(1, LANES)
    stat_lane_of_col = jnp.array([_stat_lane(c // HEAD_DIM) for c in range(GROUP_WIDTH)])
    expand = (jnp.arange(LANES)[:, None] == stat_lane_of_col[None, :]).astype(BF16)
    expand = jnp.concatenate([expand, expand], axis=0)
    b_s = jnp.repeat(gmlp_b_s[0].T, GMLP_WIDTH // GMLP_GROUPS, axis=1)

    h = x.reshape(t, D_MODEL)
    h1, h1b = _ffn_ln(h, ffn1_w_gate[0], ffn1_w_up[0], ffn1_w_down[0], ln1_g, ln1_b, tm)
    outs = _mixer_in(h1b, positions.reshape(t, 1), invf, w_in[0], b_gates, gmlp_ln_g, gmlp_ln_b,
                     gmlp_w_s[0], b_s, w_gmlp_branch[0], b, tm)
    ga, gm = outs[9], outs[10]
    accs, stats = [], []
    for gi in range(N_ATTN_GROUPS):
        acc, stat = _attention(*outs[3 * gi:3 * gi + 3])
        accs.append(acc)
        stats.append(stat)
    out = _mixer_out(accs, stats, ga, gm, h1, expand, w_attn_branch[0], w_out[0],
                     ln2_g, ln2_b, ffn2_w_gate[0], ffn2_w_up[0], ffn2_w_down[0], ln3_g, ln3_b, tm)
    return out.reshape(b, s, D_MODEL)
```

```python
import functools
import math

import jax
import jax.numpy as jnp
from jax import lax
from jax.experimental import pallas as pl
from jax.experimental.pallas import tpu as pltpu

D_MODEL = 1024
HEAD_DIM = 64
HEADS_PER_GROUP = 8
ATTN_PATTERNS = ((128, 1), (512, 4), (2048, 16))
N_ATTN_GROUPS = len(ATTN_PATTERNS)
GROUP_WIDTH = HEADS_PER_GROUP * HEAD_DIM
ATTN_WIDTH = N_ATTN_GROUPS * GROUP_WIDTH
ATTN_BLOCK = 128
ROPE_THETA = 10000.0
GMLP_CHUNK = 128
GMLP_GROUPS = 8
GMLP_WIDTH = D_MODEL
D_FF = 2816
DEPTH = 1
ALPHA = (2 * DEPTH) ** 0.25
LN_EPS = 1e-5

LANES = 128
VMEM_LIMIT_BYTES = 60 * 1024 * 1024
N_SLAB = GROUP_WIDTH // LANES

_Q0, _K0, _V0 = 0, ATTN_WIDTH, 2 * ATTN_WIDTH
_U0 = 3 * ATTN_WIDTH
_VG0 = _U0 + GMLP_WIDTH
_GA0 = _VG0 + GMLP_WIDTH
_GM0 = _GA0 + D_MODEL

_Q_SCALE = HEAD_DIM ** -0.5 * math.log2(math.e)

_DEN_SHIFT = 8


def _stat_lane(h):
    return HEAD_DIM + h if h % 2 == 0 else h


BF16 = jnp.bfloat16
F32 = jnp.float32


def _layer_norm(x, g, b):
    mu = jnp.mean(x, axis=-1, keepdims=True)
    xc = x - mu
    var = jnp.mean(xc * xc, axis=-1, keepdims=True)
    return xc * lax.rsqrt(var + LN_EPS) * g + b


def _dot(a, b):
    return jnp.dot(a, b, preferred_element_type=F32)


def _swiglu(xb, wg_ref, wu_ref, wd_ref):
    g = _dot(xb, wg_ref[...])
    u = _dot(xb, wu_ref[...])
    a = g * jax.nn.sigmoid(g) * u
    return _dot(a.astype(BF16), wd_ref[...])


def _gelu(x):
    return 0.5 * x * (1.0 + lax.erf(x * math.sqrt(0.5)))


def _resident(shape):
    return pl.BlockSpec(shape, lambda *_: (0,) * len(shape), pipeline_mode=pl.Buffered(1))


def _rows(tm, width):
    return pl.BlockSpec((tm, width), lambda i: (i, 0))


def _residue_major_spec(d, tm, width, tiles_per_seq):
    return pl.BlockSpec((None, d, tm // d, width), lambda i: (i // tiles_per_seq, 0, i % tiles_per_seq, 0))


def _params(n_axes):
    return pltpu.CompilerParams(dimension_semantics=("arbitrary",) * n_axes,
                                vmem_limit_bytes=VMEM_LIMIT_BYTES)


_HBM = pl.BlockSpec(memory_space=pl.ANY)


FILL_SLOTS = 4


def _fill_bf16(w_hbm, w_vmem, stage, sem, rows):
    k, n = w_hbm.shape
    slots = min(stage.shape[0], sem.shape[0])
    assert k % rows == 0 and rows <= stage.shape[1] and n <= stage.shape[2]
    n_blk = k // rows

    def copy(i):
        dst = stage.at[i % slots, pl.ds(0, rows), pl.ds(0, n)]
        return pltpu.make_async_copy(w_hbm.at[pl.ds(i * rows, rows), :], dst, sem.at[i % slots])

    for i in range(min(slots - 1, n_blk)):
        copy(i).start()
    for i in range(n_blk):
        if i + slots - 1 < n_blk:
            copy(i + slots - 1).start()
        copy(i).wait()
        w_vmem[i * rows:(i + 1) * rows, :] = stage[i % slots, 0:rows, 0:n].astype(BF16)


def _weight_scratch(shapes, stage_rows, stage_cols):
    return ([pltpu.VMEM(s, BF16) for s in shapes]
            + [pltpu.VMEM((FILL_SLOTS, stage_rows, stage_cols), F32), pltpu.SemaphoreType.DMA((FILL_SLOTS,))])


FFN_FILL_ROWS = 128


def _ffn_ln_kernel(x_ref, wg_hbm, wu_hbm, wd_hbm, g_ref, b_ref, h_ref, hb_ref,
                   wg_ref, wu_ref, wd_ref, stage_ref, sem):
    @pl.when(pl.program_id(0) == 0)
    def _():
        for w_hbm, w_ref in ((wg_hbm, wg_ref), (wu_hbm, wu_ref), (wd_hbm, wd_ref)):
            _fill_bf16(w_hbm, w_ref, stage_ref, sem, FFN_FILL_ROWS)

    x = x_ref[...]
    y = _swiglu(x.astype(BF16), wg_ref, wu_ref, wd_ref)
    h = _layer_norm(ALPHA * x + 0.5 * y, g_ref[...], b_ref[...])
    h_ref[...] = h
    hb_ref[...] = h.astype(BF16)


def _ffn_ln(x, wg, wu, wd, g, b, tm):
    t = x.shape[0]
    return pl.pallas_call(
        _ffn_ln_kernel,
        grid=(t // tm,),
        in_specs=[_rows(tm, D_MODEL), _HBM, _HBM, _HBM, _resident((1, D_MODEL)), _resident((1, D_MODEL))],
        out_specs=[_rows(tm, D_MODEL), _rows(tm, D_MODEL)],
        out_shape=[jax.ShapeDtypeStruct((t, D_MODEL), F32), jax.ShapeDtypeStruct((t, D_MODEL), BF16)],
        scratch_shapes=_weight_scratch([(D_MODEL, D_FF), (D_MODEL, D_FF), (D_FF, D_MODEL)], FFN_FILL_ROWS, D_FF),
        compiler_params=_params(1),
        name="ffn_ln",
    )(x, wg, wu, wd, g, b)


def _project_slabs(hb_ref, w_ref, col0, dst_ref):
    res = _dot(hb_ref[...], w_ref[:, col0:col0 + GROUP_WIDTH])
    for j in range(N_SLAB):
        dst_ref[j] = res[:, j * LANES:(j + 1) * LANES]


def _emit_residue_major(p_ref, out_ref, cos_ref=None, sin_ref=None):
    d, rows, _ = out_ref.shape
    lane = lax.broadcasted_iota(jnp.int32, (1, LANES), 1)
    first_half = (lane % HEAD_DIM) < (HEAD_DIM // 2)
    for r in range(d):
        rs = pl.ds(r, rows, stride=d) if d > 1 else slice(None)
        if cos_ref is not None:
            c, s = cos_ref[rs, :], sin_ref[rs, :]
        blocks = []
        for j in range(N_SLAB):
            x = p_ref[j, rs, :]
            if cos_ref is not None:
                partner = jnp.where(first_half, pltpu.roll(x, LANES - HEAD_DIM // 2, 1),
                                    pltpu.roll(x, HEAD_DIM // 2, 1))
                x = x * c + partner * s
            blocks.append(x)
        out_ref[r] = jnp.concatenate(blocks, axis=1).astype(out_ref.dtype)


IN_FILL_ROWS = 32


def _mixer_in_kernel(hb_ref, pos_ref, invf_ref, w_hbm, bg_ref, lng_ref, lnb_ref, ws_ref, bs_ref, wgb_hbm,
                     q0_ref, k0_ref, v0_ref, q1_ref, k1_ref, v1_ref, q2_ref, k2_ref, v2_ref,
                     ga_ref, gm_ref, a_ref, g_ref, p_ref, cs_ref, vgn_ref, w_ref, wgb_ref, stage_ref, sem):
    tm = hb_ref.shape[0]
    qkv_refs = ((q0_ref, k0_ref, v0_ref), (q1_ref, k1_ref, v1_ref), (q2_ref, k2_ref, v2_ref))

    @pl.when(pl.program_id(0) == 0)
    def _():
        _fill_bf16(w_hbm, w_ref, stage_ref, sem, IN_FILL_ROWS)
        _fill_bf16(wgb_hbm, wgb_ref, a_ref, sem, tm)

    def project_group(gi, slot):
        c0 = gi * GROUP_WIDTH
        for i, base in enumerate((_Q0, _K0, _V0)):
            _project_slabs(hb_ref, w_ref, base + c0, p_ref.at[slot + i])

    def emit_group(gi, slot):
        q_ref, k_ref, v_ref = qkv_refs[gi]
        _emit_residue_major(p_ref.at[slot], q_ref, cs_ref.at[2], cs_ref.at[3])
        _emit_residue_major(p_ref.at[slot + 1], k_ref, cs_ref.at[0], cs_ref.at[1])
        _emit_residue_major(p_ref.at[slot + 2], v_ref)

    hb = hb_ref[...]
    a_ref[0] = _dot(hb, w_ref[:, _U0:_U0 + GMLP_WIDTH])
    a_ref[1] = _dot(hb, w_ref[:, _VG0:_VG0 + GMLP_WIDTH])
    ang = pos_ref[...].astype(F32) * invf_ref[...]
    lane = lax.broadcasted_iota(jnp.int32, (1, LANES), 1)
    first_half = (lane % HEAD_DIM) < (HEAD_DIM // 2)
    cos = jnp.cos(ang)
    sin = jnp.sin(ang)
    sin = jnp.where(first_half, -sin, sin)
    cs_ref[0] = cos
    cs_ref[1] = sin
    cs_ref[2] = cos * _Q_SCALE
    cs_ref[3] = sin * _Q_SCALE

    hb = hb_ref[...]
    g_ref[0] = _dot(hb, w_ref[:, _GM0:_GM0 + D_MODEL]) + bg_ref[:, D_MODEL:]
    g_ref[1] = _dot(hb, w_ref[:, _GA0:_GA0 + D_MODEL]) + bg_ref[:, :D_MODEL]
    vgn_ref[...] = _layer_norm(_gelu(a_ref[1]), lng_ref[...], lnb_ref[...]).astype(BF16)

    project_group(0, 0)
    n_chunk = tm // GMLP_CHUNK
    gdim = GMLP_WIDTH // GMLP_GROUPS
    row = lax.broadcasted_iota(jnp.int32, (GMLP_CHUNK, GMLP_CHUNK), 0)
    col = lax.broadcasted_iota(jnp.int32, (GMLP_CHUNK, GMLP_CHUNK), 1)
    causal = col <= row
    for g in range(GMLP_GROUPS):
        ws = jnp.where(causal, ws_ref[g], 0.0).astype(BF16)
        rhs = jnp.concatenate(
            [vgn_ref[c * GMLP_CHUNK:(c + 1) * GMLP_CHUNK, g * gdim:(g + 1) * gdim] for c in range(n_chunk)],
            axis=1)
        mixed = _dot(ws, rhs)
        for c in range(n_chunk):
            a_ref[1, c * GMLP_CHUNK:(c + 1) * GMLP_CHUNK, g * gdim:(g + 1) * gdim] = (
                mixed[:, c * gdim:(c + 1) * gdim] + bs_ref[:, g * gdim:(g + 1) * gdim])
    a_ref[0] = _gelu(a_ref[0])

    project_group(1, 3)
    vgn_ref[...] = (a_ref[0] * a_ref[1]).astype(BF16)
    emit_group(0, 0)
    ga_ref[...] = jax.nn.sigmoid(g_ref[1]).astype(BF16)

    project_group(2, 0)
    emit_group(1, 3)
    g_ref[0] = jax.nn.sigmoid(g_ref[0])

    gm_ref[...] = (g_ref[0] * _dot(vgn_ref[...], wgb_ref[...])).astype(BF16)
    emit_group(2, 0)


def _mixer_in(hb, pos, invf, w_in, b_gates, ln_g, ln_b, w_s, b_s, w_gb, batch, tm):
    t = hb.shape[0]
    seq = t // batch
    in_width = w_in.shape[1]
    qkv_specs, qkv_shapes = [], []
    for _, d in ATTN_PATTERNS:
        qkv_specs += [_residue_major_spec(d, tm, GROUP_WIDTH, seq // tm)] * 3
        qkv_shapes += [jax.ShapeDtypeStruct((batch, d, seq // d, GROUP_WIDTH), BF16)] * 3
    return pl.pallas_call(
        _mixer_in_kernel,
        grid=(t // tm,),
        in_specs=[_rows(tm, D_MODEL), _rows(tm, 1), _resident((1, LANES)), _HBM,
                  _resident((1, 2 * D_MODEL)), _resident((1, GMLP_WIDTH)), _resident((1, GMLP_WIDTH)),
                  _resident((GMLP_GROUPS, GMLP_CHUNK, GMLP_CHUNK)), _resident((GMLP_CHUNK, GMLP_WIDTH)),
                  _HBM],
        out_specs=qkv_specs + [_rows(tm, D_MODEL), _rows(tm, D_MODEL)],
        out_shape=qkv_shapes + [jax.ShapeDtypeStruct((t, D_MODEL), BF16)] * 2,
        scratch_shapes=[pltpu.VMEM((2, tm, GMLP_WIDTH), F32), pltpu.VMEM((2, tm, D_MODEL), F32),
                        pltpu.VMEM((6, N_SLAB, tm, LANES), F32), pltpu.VMEM((4, tm, LANES), F32),
                        pltpu.VMEM((tm, GMLP_WIDTH), BF16)]
                       + _weight_scratch([(D_MODEL, in_width), (GMLP_WIDTH, D_MODEL)], IN_FILL_ROWS, in_width),
        compiler_params=_params(1),
        name="mixer_in",
    )(hb, pos, invf, w_in, b_gates, ln_g, ln_b, w_s, b_s, w_gb)


def _attn_kernel(q_ref, k_ref, v_ref, acc_ref, stat_ref, *, n_blk):
    n_items = q_ref.shape[0] // ATTN_BLOCK
    has_prev = n_blk > 1
    row = lax.broadcasted_iota(jnp.int32, (2 * ATTN_BLOCK, ATTN_BLOCK), 0) % ATTN_BLOCK
    col = lax.broadcasted_iota(jnp.int32, (2 * ATTN_BLOCK, ATTN_BLOCK), 1)
    neg_inf = jnp.float32(-jnp.inf)
    bias_cur = jnp.where(col <= row, 0.0, neg_inf)
    bias_prev = jnp.where(col >= row, 0.0, neg_inf)
    lane = lax.broadcasted_iota(jnp.int32, (ATTN_BLOCK, LANES), 1)
    low_head = lane < HEAD_DIM
    n_keys = 2 * ATTN_BLOCK if has_prev else ATTN_BLOCK
    low_head_keys = lax.broadcasted_iota(jnp.int32, (n_keys, LANES), 1) < HEAD_DIM
    contract_last = (((1,), (1,)), ((), ()))

    def item(j, carry):
        r0 = pl.multiple_of(j * ATTN_BLOCK, ATTN_BLOCK)
        if has_prev:
            p0 = pl.multiple_of(jnp.maximum(j - 1, 0) * ATTN_BLOCK, ATTN_BLOCK)
            prev_bias = bias_prev + jnp.where(j % n_blk > 0, 0.0, neg_inf)
        scores = []
        for pair in range(N_SLAB):
            sl = slice(pair * LANES, (pair + 1) * LANES)
            q = q_ref[pl.ds(r0, ATTN_BLOCK), sl]
            zero = jnp.zeros_like(q)
            q2 = jnp.concatenate([jnp.where(low_head, q, zero), jnp.where(low_head, zero, q)], axis=0)
            keys = k_ref[pl.ds(r0, ATTN_BLOCK), sl]
            if has_prev:
                keys = jnp.concatenate([k_ref[pl.ds(p0, ATTN_BLOCK), sl], keys], axis=0)
            s = lax.dot_general(q2, keys, contract_last, preferred_element_type=F32)
            if has_prev:
                s = s + jnp.concatenate([prev_bias, bias_cur], axis=1)
            else:
                s = s + bias_cur
            scores.append(s)
        maxes = [jnp.max(s, axis=1, keepdims=True) for s in scores]
        probs = [jnp.exp2(s - m).astype(BF16) for s, m in zip(scores, maxes)]
        stat = jnp.zeros((ATTN_BLOCK, LANES), F32)
        for pair in range(N_SLAB):
            sl = slice(pair * LANES, (pair + 1) * LANES)
            vals = v_ref[pl.ds(r0, ATTN_BLOCK), sl]
            if has_prev:
                vals = jnp.concatenate([v_ref[pl.ds(p0, ATTN_BLOCK), sl], vals], axis=0)
            one = jnp.ones_like(vals)
            r_even = _dot(probs[pair][:ATTN_BLOCK], jnp.where(low_head_keys, vals, one))
            r_odd = _dot(probs[pair][ATTN_BLOCK:], jnp.where(low_head_keys, one, vals))
            acc_ref[pl.ds(r0, ATTN_BLOCK), sl] = jnp.where(low_head, r_even, r_odd).astype(BF16)
            m_even, m_odd = maxes[pair][:ATTN_BLOCK], maxes[pair][ATTN_BLOCK:]
            for h, m_h, r_h in ((2 * pair, m_even, r_even), (2 * pair + 1, m_odd, r_odd)):
                stat = jnp.where(lane == _stat_lane(h), m_h, stat)
                stat = jnp.where(lane == _stat_lane(h) + _DEN_SHIFT, r_h, stat)
        stat_ref[pl.ds(r0, ATTN_BLOCK), :] = stat
        return carry

    lax.fori_loop(0, n_items, item, 0, unroll=4)


def _attention(q, k, v):
    b, d, l, _ = q.shape
    n_blk = l // ATTN_BLOCK
    seq = pl.BlockSpec((None, d * l, GROUP_WIDTH), lambda bi: (bi, 0, 0))
    stat_spec = pl.BlockSpec((None, d * l, LANES), lambda bi: (bi, 0, 0))
    q, k, v = (a.reshape(b, d * l, GROUP_WIDTH) for a in (q, k, v))
    acc, stat = pl.pallas_call(
        functools.partial(_attn_kernel, n_blk=n_blk),
        grid=(b,),
        in_specs=[seq, seq, seq],
        out_specs=[seq, stat_spec],
        out_shape=[jax.ShapeDtypeStruct(q.shape, BF16), jax.ShapeDtypeStruct((b, d * l, LANES), F32)],
        compiler_params=_params(1),
        name=f"band_attention_d{d}",
    )(q, k, v)
    return acc.reshape(b, d, l, GROUP_WIDTH), stat.reshape(b, d, l, LANES)


def _load_token_major(in_ref, scr_ref):
    d, rows, width = in_ref.shape
    if d == 1:
        return in_ref[0].astype(F32)
    n_slab = width // LANES
    for r in range(d):
        x = in_ref[r].astype(F32)
        for j in range(n_slab):
            scr_ref[j, pl.ds(r, rows, stride=d), :] = x[:, j * LANES:(j + 1) * LANES]
    return jnp.concatenate([scr_ref[j] for j in range(n_slab)], axis=1)


def _mixer_out_kernel(a0_ref, a1_ref, a2_ref, s0_ref, s1_ref, s2_ref, ga_ref, gm_ref, h1_ref, expand_ref,
                      wab_hbm, wout_hbm, g2_ref, b2_ref, wg_hbm, wu_hbm, wd_hbm, g3_ref, b3_ref, out_ref,
                      ascr_ref, sscr_ref, wab_ref, wout_ref, wg_ref, wu_ref, wd_ref, stage_ref, sem):
    @pl.when(pl.program_id(0) == 0)
    def _():
        for w_hbm, w_ref in ((wab_hbm, wab_ref), (wout_hbm, wout_ref), (wg_hbm, wg_ref), (wu_hbm, wu_ref),
                             (wd_hbm, wd_ref)):
            _fill_bf16(w_hbm, w_ref, stage_ref, sem, FFN_FILL_ROWS)

    tm = h1_ref.shape[0]
    stats = [_load_token_major(s_ref, sscr_ref.at[i]) for i, s_ref in enumerate((s0_ref, s1_ref, s2_ref))]
    accs = [_load_token_major(a_ref, ascr_ref.at[i]) for i, a_ref in enumerate((a0_ref, a1_ref, a2_ref))]
    lane = lax.broadcasted_iota(jnp.int32, (1, LANES), 1)
    is_max_lane = functools.reduce(jnp.logical_or, [lane == _stat_lane(h) for h in range(HEADS_PER_GROUP)])

    def mix_rows(rows):
        st = [x[rows] for x in stats]
        m = jnp.maximum(jnp.maximum(st[0], st[1]), st[2])
        es = [jnp.exp2(x - m) for x in st]
        dens = [pltpu.roll(x, LANES - _DEN_SHIFT, 1) for x in st]
        den = es[0] * dens[0] + es[1] * dens[1] + es[2] * dens[2]
        inv_den = 1.0 / jnp.where(is_max_lane, den, 1.0)
        y = None
        for e, acc in zip(es, accs):
            w = e * inv_den
            w_hi = w.astype(BF16)
            w_lo = (w - w_hi.astype(F32)).astype(BF16)
            w_wide = _dot(jnp.concatenate([w_hi, w_lo], axis=1), expand_ref[...])
            term = w_wide * acc[rows]
            y = term if y is None else y + term
        branch_a = _dot(y.astype(BF16), wab_ref[...])
        merged = ga_ref[rows, :].astype(F32) * branch_a + gm_ref[rows, :].astype(F32)
        mix = _dot(merged.astype(BF16), wout_ref[...])
        return _layer_norm(ALPHA * h1_ref[rows, :] + mix, g2_ref[...], b2_ref[...])

    h2 = mix_rows(slice(0, tm))
    ffn = _swiglu(h2.astype(BF16), wg_ref, wu_ref, wd_ref)
    out_ref[...] = _layer_norm(ALPHA * h2 + 0.5 * ffn, g3_ref[...], b3_ref[...])


def _mixer_out(accs, stats, ga, gm, h1, expand, wab, wout, g2, b2, wg, wu, wd, g3, b3, tm):
    t = h1.shape[0]
    tiles_per_seq = accs[0].shape[1] * accs[0].shape[2] // tm
    acc_specs = [_residue_major_spec(a.shape[1], tm, GROUP_WIDTH, tiles_per_seq) for a in accs]
    stat_specs = [_residue_major_spec(s.shape[1], tm, LANES, tiles_per_seq) for s in stats]
    return pl.pallas_call(
        _mixer_out_kernel,
        grid=(t // tm,),
        scratch_shapes=[pltpu.VMEM((3, N_SLAB, tm, LANES), F32), pltpu.VMEM((3, 1, tm, LANES), F32)]
                       + _weight_scratch([(GROUP_WIDTH, D_MODEL), (D_MODEL, D_MODEL), (D_MODEL, D_FF),
                                          (D_MODEL, D_FF), (D_FF, D_MODEL)], FFN_FILL_ROWS, D_FF),
        in_specs=acc_specs + stat_specs
                 + [_rows(tm, D_MODEL)] * 3
                 + [_resident((2 * LANES, GROUP_WIDTH)), _HBM, _HBM, _resident((1, D_MODEL)),
                    _resident((1, D_MODEL)), _HBM, _HBM, _HBM, _resident((1, D_MODEL)), _resident((1, D_MODEL))],
        out_specs=_rows(tm, D_MODEL),
        out_shape=jax.ShapeDtypeStruct((t, D_MODEL), F32),
        compiler_params=_params(1),
        name="mixer_out_ffn",
    )(*accs, *stats, ga, gm, h1, expand, wab, wout, g2, b2, wg, wu, wd, g3, b3)


def kernel(x, positions, ffn1_w_gate, ffn1_w_up, ffn1_w_down, ln1_g, ln1_b, w_in, b_gates, gmlp_ln_g, gmlp_ln_b, gmlp_w_s, gmlp_b_s, w_attn_branch, w_gmlp_branch, w_out, ln2_g, ln2_b, ffn2_w_gate, ffn2_w_up, ffn2_w_down, ln3_g, ln3_b):
    b, s, d_model = x.shape
    assert d_model == D_MODEL and ln1_g.shape[0] == DEPTH == 1
    t = b * s
    tm = 512
    assert t % tm == 0 and tm % GMLP_CHUNK == 0 and s % tm == 0

    lane = jnp.arange(LANES) % (HEAD_DIM // 2)
    invf = (ROPE_THETA ** (-(2 * lane).astype(F32) / HEAD_DIM)).reshape(1, LANES)
    stat_lane_of_col = jnp.array([_stat_lane(c // HEAD_DIM) for c in range(GROUP_WIDTH)])
    expand = (jnp.arange(LANES)[:, None] == stat_lane_of_col[None, :]).astype(BF16)
    expand = jnp.concatenate([expand, expand], axis=0)
    b_s = jnp.repeat(gmlp_b_s[0].T, GMLP_WIDTH // GMLP_GROUPS, axis=1)

    h = x.reshape(t, D_MODEL)
    h1, h1b = _ffn_ln(h, ffn1_w_gate[0], ffn1_w_up[0], ffn1_w_down[0], ln1_g, ln1_b, tm)
    outs = _mixer_in(h1b, positions.reshape(t, 1), invf, w_in[0], b_gates, gmlp_ln_g, gmlp_ln_b,
                     gmlp_w_s[0], b_s, w_gmlp_branch[0], b, tm)
    ga, gm = outs[9], outs[10]
    accs, stats = [], []
    for gi in range(N_ATTN_GROUPS):
        acc, stat = _attention(*outs[3 * gi:3 * gi + 3])
        accs.append(acc)
        stats.append(stat)
    out = _mixer_out(accs, stats, ga, gm, h1, expand, w_attn_branch[0], w_out[0],
                     ln2_g, ln2_b, ffn2_w_gate[0], ffn2_w_up[0], ffn2_w_down[0], ln3_g, ln3_b, tm)
    return out.reshape(b, s, D_MODEL)
```

```python
import functools
import math

import jax
import jax.numpy as jnp
from jax import lax
from jax.experimental import pallas as pl
from jax.experimental.pallas import tpu as pltpu

D_MODEL = 1024
HEAD_DIM = 64
HEADS_PER_GROUP = 8
ATTN_PATTERNS = ((128, 1), (512, 4), (2048, 16))
N_ATTN_GROUPS = len(ATTN_PATTERNS)
GROUP_WIDTH = HEADS_PER_GROUP * HEAD_DIM
ATTN_WIDTH = N_ATTN_GROUPS * GROUP_WIDTH
ATTN_BLOCK = 128
ROPE_THETA = 10000.0
GMLP_CHUNK = 128
GMLP_GROUPS = 8
GMLP_WIDTH = D_MODEL
D_FF = 2816
DEPTH = 1
ALPHA = (2 * DEPTH) ** 0.25
LN_EPS = 1e-5

LANES = 128
VMEM_LIMIT_BYTES = 60 * 1024 * 1024
N_SLAB = GROUP_WIDTH // LANES

_Q0, _K0, _V0 = 0, ATTN_WIDTH, 2 * ATTN_WIDTH
_U0 = 3 * ATTN_WIDTH
_VG0 = _U0 + GMLP_WIDTH
_GA0 = _VG0 + GMLP_WIDTH
_GM0 = _GA0 + D_MODEL

_Q_SCALE = HEAD_DIM ** -0.5 * math.log2(math.e)

_DEN_SHIFT = 8


def _stat_lane(h):
    return HEAD_DIM + h if h % 2 == 0 else h


BF16 = jnp.bfloat16
F32 = jnp.float32


def _layer_norm(x, g, b):
    mu = jnp.mean(x, axis=-1, keepdims=True)
    xc = x - mu
    var = jnp.mean(xc * xc, axis=-1, keepdims=True)
    return xc * lax.rsqrt(var + LN_EPS) * g + b


def _dot(a, b):
    return jnp.dot(a, b, preferred_element_type=F32)


def _swiglu(xb, wg_ref, wu_ref, wd_ref):
    g = _dot(xb, wg_ref[...])
    u = _dot(xb, wu_ref[...])
    a = g * jax.nn.sigmoid(g) * u
    return _dot(a.astype(BF16), wd_ref[...])


def _gelu(x):
    return 0.5 * x * (1.0 + lax.erf(x * math.sqrt(0.5)))


def _resident(shape):
    return pl.BlockSpec(shape, lambda *_: (0,) * len(shape), pipeline_mode=pl.Buffered(1))


def _rows(tm, width):
    return pl.BlockSpec((tm, width), lambda i: (i, 0))


def _residue_major_spec(d, tm, width, tiles_per_seq):
    return pl.BlockSpec((None, d, tm // d, width), lambda i: (i // tiles_per_seq, 0, i % tiles_per_seq, 0))


def _params(n_axes):
    return pltpu.CompilerParams(dimension_semantics=("arbitrary",) * n_axes,
                                vmem_limit_bytes=VMEM_LIMIT_BYTES)


_HBM = pl.BlockSpec(memory_space=pl.ANY)


FILL_SLOTS = 4


def _fill_bf16(w_hbm, w_vmem, stage, sem, rows, cols=None):
    k, n = w_hbm.shape
    cols = n if cols is None else cols
    slots = min(stage.shape[0], sem.shape[0])
    assert k % rows == 0 and rows <= stage.shape[1] and cols <= stage.shape[2]
    blocks = [(r, c, min(cols, n - c)) for r in range(0, k, rows) for c in range(0, n, cols)]

    def copy(i):
        r, c, width = blocks[i]
        dst = stage.at[i % slots, pl.ds(0, rows), pl.ds(0, width)]
        return pltpu.make_async_copy(w_hbm.at[pl.ds(r, rows), pl.ds(c, width)], dst, sem.at[i % slots])

    for i in range(min(slots - 1, len(blocks))):
        copy(i).start()
    for i, (r, c, width) in enumerate(blocks):
        if i + slots - 1 < len(blocks):
            copy(i + slots - 1).start()
        copy(i).wait()
        w_vmem[r:r + rows, c:c + width] = stage[i % slots, 0:rows, 0:width].astype(BF16)


def _weight_scratch(shapes, stage_shape=None):
    stage = [] if stage_shape is None else [pltpu.VMEM(stage_shape, F32)]
    return [pltpu.VMEM(s, BF16) for s in shapes] + stage + [pltpu.SemaphoreType.DMA((FILL_SLOTS,))]


FFN_FILL_ROWS = 256


def _ffn_ln_kernel(x_ref, wg_hbm, wu_hbm, wd_hbm, g_ref, b_ref, h_ref, hb_ref,
                   wg_ref, wu_ref, wd_ref, stage_ref, sem):
    @pl.when(pl.program_id(0) == 0)
    def _():
        for w_hbm, w_ref in ((wg_hbm, wg_ref), (wu_hbm, wu_ref), (wd_hbm, wd_ref)):
            _fill_bf16(w_hbm, w_ref, stage_ref, sem, FFN_FILL_ROWS)

    x = x_ref[...]
    y = _swiglu(x.astype(BF16), wg_ref, wu_ref, wd_ref)
    h = _layer_norm(ALPHA * x + 0.5 * y, g_ref[...], b_ref[...])
    h_ref[...] = h
    hb_ref[...] = h.astype(BF16)


def _ffn_ln(x, wg, wu, wd, g, b, tm):
    t = x.shape[0]
    return pl.pallas_call(
        _ffn_ln_kernel,
        grid=(t // tm,),
        in_specs=[_rows(tm, D_MODEL), _HBM, _HBM, _HBM, _resident((1, D_MODEL)), _resident((1, D_MODEL))],
        out_specs=[_rows(tm, D_MODEL), _rows(tm, D_MODEL)],
        out_shape=[jax.ShapeDtypeStruct((t, D_MODEL), F32), jax.ShapeDtypeStruct((t, D_MODEL), BF16)],
        scratch_shapes=_weight_scratch([(D_MODEL, D_FF), (D_MODEL, D_FF), (D_FF, D_MODEL)],
                                       (FILL_SLOTS, FFN_FILL_ROWS, D_FF)),
        compiler_params=_params(1),
        name="ffn_ln",
    )(x, wg, wu, wd, g, b)


def _project_slabs(hb_ref, w_ref, col0, dst_ref):
    res = _dot(hb_ref[...], w_ref[:, col0:col0 + GROUP_WIDTH])
    for j in range(N_SLAB):
        dst_ref[j] = res[:, j * LANES:(j + 1) * LANES]


def _emit_residue_major(p_ref, out_ref, cos_ref=None, sin_ref=None):
    d, rows, _ = out_ref.shape
    lane = lax.broadcasted_iota(jnp.int32, (1, LANES), 1)
    first_half = (lane % HEAD_DIM) < (HEAD_DIM // 2)
    for r in range(d):
        rs = pl.ds(r, rows, stride=d) if d > 1 else slice(None)
        if cos_ref is not None:
            c, s = cos_ref[rs, :], sin_ref[rs, :]
        blocks = []
        for j in range(N_SLAB):
            x = p_ref[j, rs, :]
            if cos_ref is not None:
                partner = jnp.where(first_half, pltpu.roll(x, LANES - HEAD_DIM // 2, 1),
                                    pltpu.roll(x, HEAD_DIM // 2, 1))
                x = x * c + partner * s
            blocks.append(x)
        out_ref[r] = jnp.concatenate(blocks, axis=1).astype(out_ref.dtype)


def _mixer_in_kernel(hb_ref, pos_ref, invf_ref, w_hbm, bg_ref, lng_ref, lnb_ref, ws_ref, bs_ref, wgb_hbm,
                     q0_ref, k0_ref, v0_ref, q1_ref, k1_ref, v1_ref, q2_ref, k2_ref, v2_ref,
                     ga_ref, gm_ref, ag_ref, p_ref, cs_ref, vgn_ref, w_ref, wgb_ref, sem):
    tm = hb_ref.shape[0]
    a_ref, g_ref = ag_ref.at[pl.ds(0, 2)], ag_ref.at[pl.ds(2, 2)]
    qkv_refs = ((q0_ref, k0_ref, v0_ref), (q1_ref, k1_ref, v1_ref), (q2_ref, k2_ref, v2_ref))

    @pl.when(pl.program_id(0) == 0)
    def _():
        _fill_bf16(w_hbm, w_ref, ag_ref, sem, tm, ag_ref.shape[2])
        _fill_bf16(wgb_hbm, wgb_ref, ag_ref, sem, tm)

    def project_group(gi, slot):
        c0 = gi * GROUP_WIDTH
        for i, base in enumerate((_Q0, _K0, _V0)):
            _project_slabs(hb_ref, w_ref, base + c0, p_ref.at[slot + i])

    def emit_group(gi, slot):
        q_ref, k_ref, v_ref = qkv_refs[gi]
        _emit_residue_major(p_ref.at[slot], q_ref, cs_ref.at[2], cs_ref.at[3])
        _emit_residue_major(p_ref.at[slot + 1], k_ref, cs_ref.at[0], cs_ref.at[1])
        _emit_residue_major(p_ref.at[slot + 2], v_ref)

    hb = hb_ref[...]
    a_ref[0] = _dot(hb, w_ref[:, _U0:_U0 + GMLP_WIDTH])
    a_ref[1] = _dot(hb, w_ref[:, _VG0:_VG0 + GMLP_WIDTH])
    ang = pos_ref[...].astype(F32) * invf_ref[...]
    lane = lax.broadcasted_iota(jnp.int32, (1, LANES), 1)
    first_half = (lane % HEAD_DIM) < (HEAD_DIM // 2)
    cos = jnp.cos(ang)
    sin = jnp.sin(ang)
    sin = jnp.where(first_half, -sin, sin)
    cs_ref[0] = cos
    cs_ref[1] = sin
    cs_ref[2] = cos * _Q_SCALE
    cs_ref[3] = sin * _Q_SCALE

    hb = hb_ref[...]
    g_ref[0] = _dot(hb, w_ref[:, _GM0:_GM0 + D_MODEL]) + bg_ref[:, D_MODEL:]
    g_ref[1] = _dot(hb, w_ref[:, _GA0:_GA0 + D_MODEL]) + bg_ref[:, :D_MODEL]
    vgn_ref[...] = _layer_norm(_gelu(a_ref[1]), lng_ref[...], lnb_ref[...]).astype(BF16)

    project_group(0, 0)
    n_chunk = tm // GMLP_CHUNK
    gdim = GMLP_WIDTH // GMLP_GROUPS
    row = lax.broadcasted_iota(jnp.int32, (GMLP_CHUNK, GMLP_CHUNK), 0)
    col = lax.broadcasted_iota(jnp.int32, (GMLP_CHUNK, GMLP_CHUNK), 1)
    causal = col <= row
    for g in range(GMLP_GROUPS):
        ws = jnp.where(causal, ws_ref[g], 0.0).astype(BF16)
        rhs = jnp.concatenate(
            [vgn_ref[c * GMLP_CHUNK:(c + 1) * GMLP_CHUNK, g * gdim:(g + 1) * gdim] for c in range(n_chunk)],
            axis=1)
        mixed = _dot(ws, rhs)
        for c in range(n_chunk):
            a_ref[1, c * GMLP_CHUNK:(c + 1) * GMLP_CHUNK, g * gdim:(g + 1) * gdim] = (
                mixed[:, c * gdim:(c + 1) * gdim] + bs_ref[:, g * gdim:(g + 1) * gdim])
    a_ref[0] = _gelu(a_ref[0])

    project_group(1, 3)
    vgn_ref[...] = (a_ref[0] * a_ref[1]).astype(BF16)
    emit_group(0, 0)
    ga_ref[...] = jax.nn.sigmoid(g_ref[1]).astype(BF16)

    project_group(2, 0)
    emit_group(1, 3)
    g_ref[0] = jax.nn.sigmoid(g_ref[0])

    gm_ref[...] = (g_ref[0] * _dot(vgn_ref[...], wgb_ref[...])).astype(BF16)
    emit_group(2, 0)


def _mixer_in(hb, pos, invf, w_in, b_gates, ln_g, ln_b, w_s, b_s, w_gb, batch, tm):
    t = hb.shape[0]
    seq = t // batch
    in_width = w_in.shape[1]
    qkv_specs, qkv_shapes = [], []
    for _, d in ATTN_PATTERNS:
        qkv_specs += [_residue_major_spec(d, tm, GROUP_WIDTH, seq // tm)] * 3
        qkv_shapes += [jax.ShapeDtypeStruct((batch, d, seq // d, GROUP_WIDTH), BF16)] * 3
    return pl.pallas_call(
        _mixer_in_kernel,
        grid=(t // tm,),
        in_specs=[_rows(tm, D_MODEL), _rows(tm, 1), _resident((1, LANES)), _HBM,
                  _resident((1, 2 * D_MODEL)), _resident((1, GMLP_WIDTH)), _resident((1, GMLP_WIDTH)),
                  _resident((GMLP_GROUPS, GMLP_CHUNK, GMLP_CHUNK)), _resident((GMLP_CHUNK, GMLP_WIDTH)),
                  _HBM],
        out_specs=qkv_specs + [_rows(tm, D_MODEL), _rows(tm, D_MODEL)],
        out_shape=qkv_shapes + [jax.ShapeDtypeStruct((t, D_MODEL), BF16)] * 2,
        scratch_shapes=[pltpu.VMEM((FILL_SLOTS, tm, D_MODEL), F32),
                        pltpu.VMEM((6, N_SLAB, tm, LANES), F32), pltpu.VMEM((4, tm, LANES), F32),
                        pltpu.VMEM((tm, GMLP_WIDTH), BF16)]
                       + _weight_scratch([(D_MODEL, in_width), (GMLP_WIDTH, D_MODEL)]),
        compiler_params=_params(1),
        name="mixer_in",
    )(hb, pos, invf, w_in, b_gates, ln_g, ln_b, w_s, b_s, w_gb)


def _attn_kernel(q_ref, k_ref, v_ref, acc_ref, stat_ref, *, n_blk):
    n_items = q_ref.shape[0] // ATTN_BLOCK
    has_prev = n_blk > 1
    row = lax.broadcasted_iota(jnp.int32, (2 * ATTN_BLOCK, ATTN_BLOCK), 0) % ATTN_BLOCK
    col = lax.broadcasted_iota(jnp.int32, (2 * ATTN_BLOCK, ATTN_BLOCK), 1)
    neg_inf = jnp.float32(-jnp.inf)
    bias_cur = jnp.where(col <= row, 0.0, neg_inf)
    bias_prev = jnp.where(col >= row, 0.0, neg_inf)
    lane = lax.broadcasted_iota(jnp.int32, (ATTN_BLOCK, LANES), 1)
    low_head = lane < HEAD_DIM
    n_keys = 2 * ATTN_BLOCK if has_prev else ATTN_BLOCK
    low_head_keys = lax.broadcasted_iota(jnp.int32, (n_keys, LANES), 1) < HEAD_DIM
    contract_last = (((1,), (1,)), ((), ()))

    def item(j, carry):
        r0 = pl.multiple_of(j * ATTN_BLOCK, ATTN_BLOCK)
        if has_prev:
            p0 = pl.multiple_of(jnp.maximum(j - 1, 0) * ATTN_BLOCK, ATTN_BLOCK)
            prev_bias = bias_prev + jnp.where(j % n_blk > 0, 0.0, neg_inf)
        scores = []
        for pair in range(N_SLAB):
            sl = slice(pair * LANES, (pair + 1) * LANES)
            q = q_ref[pl.ds(r0, ATTN_BLOCK), sl]
            zero = jnp.zeros_like(q)
            q2 = jnp.concatenate([jnp.where(low_head, q, zero), jnp.where(low_head, zero, q)], axis=0)
            keys = k_ref[pl.ds(r0, ATTN_BLOCK), sl]
            if has_prev:
                keys = jnp.concatenate([k_ref[pl.ds(p0, ATTN_BLOCK), sl], keys], axis=0)
            s = lax.dot_general(q2, keys, contract_last, preferred_element_type=F32)
            if has_prev:
                s = s + jnp.concatenate([prev_bias, bias_cur], axis=1)
            else:
                s = s + bias_cur
            scores.append(s)
        maxes = [jnp.max(s, axis=1, keepdims=True) for s in scores]
        probs = [jnp.exp2(s - m).astype(BF16) for s, m in zip(scores, maxes)]
        stat = jnp.zeros((ATTN_BLOCK, LANES), F32)
        for pair in range(N_SLAB):
            sl = slice(pair * LANES, (pair + 1) * LANES)
            vals = v_ref[pl.ds(r0, ATTN_BLOCK), sl]
            if has_prev:
                vals = jnp.concatenate([v_ref[pl.ds(p0, ATTN_BLOCK), sl], vals], axis=0)
            one = jnp.ones_like(vals)
            r_even = _dot(probs[pair][:ATTN_BLOCK], jnp.where(low_head_keys, vals, one))
            r_odd = _dot(probs[pair][ATTN_BLOCK:], jnp.where(low_head_keys, one, vals))
            acc_ref[pl.ds(r0, ATTN_BLOCK), sl] = jnp.where(low_head, r_even, r_odd).astype(BF16)
            m_even, m_odd = maxes[pair][:ATTN_BLOCK], maxes[pair][ATTN_BLOCK:]
            for h, m_h, r_h in ((2 * pair, m_even, r_even), (2 * pair + 1, m_odd, r_odd)):
                stat = jnp.where(lane == _stat_lane(h), m_h, stat)
                stat = jnp.where(lane == _stat_lane(h) + _DEN_SHIFT, r_h, stat)
        stat_ref[pl.ds(r0, ATTN_BLOCK), :] = stat
        return carry

    lax.fori_loop(0, n_items, item, 0, unroll=4)


def _attention(q, k, v):
    b, d, l, _ = q.shape
    n_blk = l // ATTN_BLOCK
    seq = pl.BlockSpec((None, d * l, GROUP_WIDTH), lambda bi: (bi, 0, 0))
    stat_spec = pl.BlockSpec((None, d * l, LANES), lambda bi: (bi, 0, 0))
    q, k, v = (a.reshape(b, d * l, GROUP_WIDTH) for a in (q, k, v))
    acc, stat = pl.pallas_call(
        functools.partial(_attn_kernel, n_blk=n_blk),
        grid=(b,),
        in_specs=[seq, seq, seq],
        out_specs=[seq, stat_spec],
        out_shape=[jax.ShapeDtypeStruct(q.shape, BF16), jax.ShapeDtypeStruct((b, d * l, LANES), F32)],
        compiler_params=_params(1),
        name=f"band_attention_d{d}",
    )(q, k, v)
    return acc.reshape(b, d, l, GROUP_WIDTH), stat.reshape(b, d, l, LANES)


def _load_token_major(in_ref, scr_ref):
    d, rows, width = in_ref.shape
    if d == 1:
        return in_ref[0].astype(F32)
    n_slab = width // LANES
    for r in range(d):
        x = in_ref[r].astype(F32)
        for j in range(n_slab):
            scr_ref[j, pl.ds(r, rows, stride=d), :] = x[:, j * LANES:(j + 1) * LANES]
    return jnp.concatenate([scr_ref[j] for j in range(n_slab)], axis=1)


def _mixer_out_kernel(a0_ref, a1_ref, a2_ref, s0_ref, s1_ref, s2_ref, ga_ref, gm_ref, h1_ref, expand_ref,
                      wab_hbm, wout_hbm, g2_ref, b2_ref, wg_hbm, wu_hbm, wd_hbm, g3_ref, b3_ref, out_ref,
                      ascr_ref, sscr_ref, wab_ref, wout_ref, wg_ref, wu_ref, wd_ref, stage_ref, sem):
    @pl.when(pl.program_id(0) == 0)
    def _():
        for w_hbm, w_ref in ((wab_hbm, wab_ref), (wout_hbm, wout_ref), (wg_hbm, wg_ref), (wu_hbm, wu_ref),
                             (wd_hbm, wd_ref)):
            _fill_bf16(w_hbm, w_ref, stage_ref, sem, FFN_FILL_ROWS)

    tm = h1_ref.shape[0]
    stats = [_load_token_major(s_ref, sscr_ref.at[i]) for i, s_ref in enumerate((s0_ref, s1_ref, s2_ref))]
    accs = [_load_token_major(a_ref, ascr_ref.at[i]) for i, a_ref in enumerate((a0_ref, a1_ref, a2_ref))]
    lane = lax.broadcasted_iota(jnp.int32, (1, LANES), 1)
    is_max_lane = functools.reduce(jnp.logical_or, [lane == _stat_lane(h) for h in range(HEADS_PER_GROUP)])

    def mix_rows(rows):
        st = [x[rows] for x in stats]
        m = jnp.maximum(jnp.maximum(st[0], st[1]), st[2])
        es = [jnp.exp2(x - m) for x in st]
        dens = [pltpu.roll(x, LANES - _DEN_SHIFT, 1) for x in st]
        den = es[0] * dens[0] + es[1] * dens[1] + es[2] * dens[2]
        inv_den = 1.0 / jnp.where(is_max_lane, den, 1.0)
        y = None
        for e, acc in zip(es, accs):
            w = e * inv_den
            w_hi = w.astype(BF16)
            w_lo = (w - w_hi.astype(F32)).astype(BF16)
            w_wide = _dot(jnp.concatenate([w_hi, w_lo], axis=1), expand_ref[...])
            term = w_wide * acc[rows]
            y = term if y is None else y + term
        branch_a = _dot(y.astype(BF16), wab_ref[...])
        merged = ga_ref[rows, :].astype(F32) * branch_a + gm_ref[rows, :].astype(F32)
        mix = _dot(merged.astype(BF16), wout_ref[...])
        return _layer_norm(ALPHA * h1_ref[rows, :] + mix, g2_ref[...], b2_ref[...])

    h2 = mix_rows(slice(0, tm))
    ffn = _swiglu(h2.astype(BF16), wg_ref, wu_ref, wd_ref)
    out_ref[...] = _layer_norm(ALPHA * h2 + 0.5 * ffn, g3_ref[...], b3_ref[...])


def _mixer_out(accs, stats, ga, gm, h1, expand, wab, wout, g2, b2, wg, wu, wd, g3, b3, tm):
    t = h1.shape[0]
    tiles_per_seq = accs[0].shape[1] * accs[0].shape[2] // tm
    acc_specs = [_residue_major_spec(a.shape[1], tm, GROUP_WIDTH, tiles_per_seq) for a in accs]
    stat_specs = [_residue_major_spec(s.shape[1], tm, LANES, tiles_per_seq) for s in stats]
    return pl.pallas_call(
        _mixer_out_kernel,
        grid=(t // tm,),
        scratch_shapes=[pltpu.VMEM((3, N_SLAB, tm, LANES), F32), pltpu.VMEM((3, 1, tm, LANES), F32)]
                       + _weight_scratch([(GROUP_WIDTH, D_MODEL), (D_MODEL, D_MODEL), (D_MODEL, D_FF),
                                          (D_MODEL, D_FF), (D_FF, D_MODEL)], (FILL_SLOTS - 1, FFN_FILL_ROWS, D_FF)),
        in_specs=acc_specs + stat_specs
                 + [_rows(tm, D_MODEL)] * 3
                 + [_resident((2 * LANES, GROUP_WIDTH)), _HBM, _HBM, _resident((1, D_MODEL)),
                    _resident((1, D_MODEL)), _HBM, _HBM, _HBM, _resident((1, D_MODEL)), _resident((1, D_MODEL))],
        out_specs=_rows(tm, D_MODEL),
        out_shape=jax.ShapeDtypeStruct((t, D_MODEL), F32),
        compiler_params=_params(1),
        name="mixer_out_ffn",
    )(*accs, *stats, ga, gm, h1, expand, wab, wout, g2, b2, wg, wu, wd, g3, b3)


def kernel(x, positions, ffn1_w_gate, ffn1_w_up, ffn1_w_down, ln1_g, ln1_b, w_in, b_gates, gmlp_ln_g, gmlp_ln_b, gmlp_w_s, gmlp_b_s, w_attn_branch, w_gmlp_branch, w_out, ln2_g, ln2_b, ffn2_w_gate, ffn2_w_up, ffn2_w_down, ln3_g, ln3_b):
    b, s, d_model = x.shape
    assert d_model == D_MODEL and ln1_g.shape[0] == DEPTH == 1
    t = b * s
    tm = 512
    assert t % tm == 0 and tm % GMLP_CHUNK == 0 and s % tm == 0

    lane = jnp.arange(LANES) % (HEAD_DIM // 2)
    invf = (ROPE_THETA ** (-(2 * lane).astype(F32) / HEAD_DIM)).reshape(1, LANES)
    stat_lane_of_col = jnp.array([_stat_lane(c // HEAD_DIM) for c in range(GROUP_WIDTH)])
    expand = (jnp.arange(LANES)[:, None] == stat_lane_of_col[None, :]).astype(BF16)
    expand = jnp.concatenate([expand, expand], axis=0)
    b_s = jnp.repeat(gmlp_b_s[0].T, GMLP_WIDTH // GMLP_GROUPS, axis=1)

    h = x.reshape(t, D_MODEL)
    h1, h1b = _ffn_ln(h, ffn1_w_gate[0], ffn1_w_up[0], ffn1_w_down[0], ln1_g, ln1_b, tm)
    outs = _mixer_in(h1b, positions.reshape(t, 1), invf, w_in[0], b_gates, gmlp_ln_g, gmlp_ln_b,
                     gmlp_w_s[0], b_s, w_gmlp_branch[0], b, tm)
    ga, gm = outs[9], outs[10]
    accs, stats = [], []
    for gi in range(N_ATTN_GROUPS):
        acc, stat = _attention(*outs[3 * gi:3 * gi + 3])
        accs.append(acc)
        stats.append(stat)
    out = _mixer_out(accs, stats, ga, gm, h1, expand, w_attn_branch[0], w_out[0],
                     ln2_g, ln2_b, ffn2_w_gate[0], ffn2_w_up[0], ffn2_w_down[0], ln3_g, ln3_b, tm)
    return out.reshape(b, s, D_MODEL)
```

```python
import functools
import math

import jax
import jax.numpy as jnp
from jax import lax
from jax.experimental import pallas as pl
from jax.experimental.pallas import tpu as pltpu

D_MODEL = 1024
HEAD_DIM = 64
HEADS_PER_GROUP = 8
ATTN_PATTERNS = ((128, 1), (512, 4), (2048, 16))
N_ATTN_GROUPS = len(ATTN_PATTERNS)
GROUP_WIDTH = HEADS_PER_GROUP * HEAD_DIM
ATTN_WIDTH = N_ATTN_GROUPS * GROUP_WIDTH
ATTN_BLOCK = 128
ROPE_THETA = 10000.0
GMLP_CHUNK = 128
GMLP_GROUPS = 8
GMLP_WIDTH = D_MODEL
D_FF = 2816
DEPTH = 1
ALPHA = (2 * DEPTH) ** 0.25
LN_EPS = 1e-5

LANES = 128
VMEM_LIMIT_BYTES = 60 * 1024 * 1024
N_SLAB = GROUP_WIDTH // LANES

_Q0, _K0, _V0 = 0, ATTN_WIDTH, 2 * ATTN_WIDTH
_U0 = 3 * ATTN_WIDTH
_VG0 = _U0 + GMLP_WIDTH
_GA0 = _VG0 + GMLP_WIDTH
_GM0 = _GA0 + D_MODEL

_Q_SCALE = HEAD_DIM ** -0.5 * math.log2(math.e)

_DEN_SHIFT = 8


def _stat_lane(h):
    return HEAD_DIM + h if h % 2 == 0 else h


BF16 = jnp.bfloat16
F32 = jnp.float32


def _layer_norm(x, g, b):
    mu = jnp.mean(x, axis=-1, keepdims=True)
    xc = x - mu
    var = jnp.mean(xc * xc, axis=-1, keepdims=True)
    return xc * lax.rsqrt(var + LN_EPS) * g + b


def _dot(a, b):
    return jnp.dot(a, b, preferred_element_type=F32)


def _swiglu(xb, wg_ref, wu_ref, wd_ref):
    g = _dot(xb, wg_ref[...])
    u = _dot(xb, wu_ref[...])
    a = g * jax.nn.sigmoid(g) * u
    return _dot(a.astype(BF16), wd_ref[...])


def _gelu(x):
    return 0.5 * x * (1.0 + lax.erf(x * math.sqrt(0.5)))


def _resident(shape):
    return pl.BlockSpec(shape, lambda *_: (0,) * len(shape), pipeline_mode=pl.Buffered(1))


def _rows(tm, width):
    return pl.BlockSpec((tm, width), lambda i: (i, 0))


def _residue_major_spec(d, tm, width, tiles_per_seq):
    return pl.BlockSpec((None, d, tm // d, width), lambda i: (i // tiles_per_seq, 0, i % tiles_per_seq, 0))


def _params(n_axes):
    return pltpu.CompilerParams(dimension_semantics=("arbitrary",) * n_axes,
                                vmem_limit_bytes=VMEM_LIMIT_BYTES)


_HBM = pl.BlockSpec(memory_space=pl.ANY)


FILL_SLOTS = 4


def _fill_bf16(w_hbm, w_vmem, stage, sem, rows, cols=None):
    k, n = w_hbm.shape
    cols = n if cols is None else cols
    slots = min(stage.shape[0], sem.shape[0])
    assert k % rows == 0 and rows <= stage.shape[1] and cols <= stage.shape[2]
    blocks = [(r, c, min(cols, n - c)) for r in range(0, k, rows) for c in range(0, n, cols)]

    def copy(i):
        r, c, width = blocks[i]
        dst = stage.at[i % slots, pl.ds(0, rows), pl.ds(0, width)]
        return pltpu.make_async_copy(w_hbm.at[pl.ds(r, rows), pl.ds(c, width)], dst, sem.at[i % slots])

    for i in range(min(slots - 1, len(blocks))):
        copy(i).start()
    for i, (r, c, width) in enumerate(blocks):
        if i + slots - 1 < len(blocks):
            copy(i + slots - 1).start()
        copy(i).wait()
        w_vmem[r:r + rows, c:c + width] = stage[i % slots, 0:rows, 0:width].astype(BF16)


def _weight_scratch(shapes, stage_shape=None):
    stage = [] if stage_shape is None else [pltpu.VMEM(stage_shape, F32)]
    return [pltpu.VMEM(s, BF16) for s in shapes] + stage + [pltpu.SemaphoreType.DMA((FILL_SLOTS,))]


FFN_FILL_ROWS = 256


def _ffn_ln_kernel(x_ref, wg_hbm, wu_hbm, wd_hbm, g_ref, b_ref, h_ref, hb_ref,
                   wg_ref, wu_ref, wd_ref, stage_ref, sem):
    @pl.when(pl.program_id(0) == 0)
    def _():
        for w_hbm, w_ref in ((wg_hbm, wg_ref), (wu_hbm, wu_ref), (wd_hbm, wd_ref)):
            _fill_bf16(w_hbm, w_ref, stage_ref, sem, FFN_FILL_ROWS)

    x = x_ref[...]
    y = _swiglu(x.astype(BF16), wg_ref, wu_ref, wd_ref)
    h = _layer_norm(ALPHA * x + 0.5 * y, g_ref[...], b_ref[...])
    h_ref[...] = h
    hb_ref[...] = h.astype(BF16)


def _ffn_ln(x, wg, wu, wd, g, b, tm):
    t = x.shape[0]
    return pl.pallas_call(
        _ffn_ln_kernel,
        grid=(t // tm,),
        in_specs=[_rows(tm, D_MODEL), _HBM, _HBM, _HBM, _resident((1, D_MODEL)), _resident((1, D_MODEL))],
        out_specs=[_rows(tm, D_MODEL), _rows(tm, D_MODEL)],
        out_shape=[jax.ShapeDtypeStruct((t, D_MODEL), F32), jax.ShapeDtypeStruct((t, D_MODEL), BF16)],
        scratch_shapes=_weight_scratch([(D_MODEL, D_FF), (D_MODEL, D_FF), (D_FF, D_MODEL)],
                                       (FILL_SLOTS, FFN_FILL_ROWS, D_FF)),
        compiler_params=_params(1),
        name="ffn_ln",
    )(x, wg, wu, wd, g, b)


def _project_slabs(hb_ref, w_ref, col0, dst_ref):
    res = _dot(hb_ref[...], w_ref[:, col0:col0 + GROUP_WIDTH])
    for j in range(N_SLAB):
        dst_ref[j] = res[:, j * LANES:(j + 1) * LANES]


def _emit_residue_major(p_ref, out_ref, cos_ref=None, sin_ref=None):
    d, rows, _ = out_ref.shape
    lane = lax.broadcasted_iota(jnp.int32, (1, LANES), 1)
    first_half = (lane % HEAD_DIM) < (HEAD_DIM // 2)
    for r in range(d):
        rs = pl.ds(r, rows, stride=d) if d > 1 else slice(None)
        if cos_ref is not None:
            c, s = cos_ref[rs, :], sin_ref[rs, :]
        blocks = []
        for j in range(N_SLAB):
            x = p_ref[j, rs, :]
            if cos_ref is not None:
                partner = jnp.where(first_half, pltpu.roll(x, LANES - HEAD_DIM // 2, 1),
                                    pltpu.roll(x, HEAD_DIM // 2, 1))
                x = x * c + partner * s
            blocks.append(x)
        out_ref[r] = jnp.concatenate(blocks, axis=1).astype(out_ref.dtype)


def _mixer_in_kernel(hb_ref, pos_ref, invf_ref, w_hbm, bg_ref, lng_ref, lnb_ref, ws_ref, bs_ref, wgb_hbm,
                     q0_ref, k0_ref, v0_ref, q1_ref, k1_ref, v1_ref, q2_ref, k2_ref, v2_ref,
                     ga_ref, gm_ref, ag_ref, p_ref, cs_ref, vgn_ref, w_ref, wgb_ref, sem):
    tm = hb_ref.shape[0]
    a_ref, g_ref = ag_ref.at[pl.ds(0, 2)], ag_ref.at[pl.ds(2, 2)]
    qkv_refs = ((q0_ref, k0_ref, v0_ref), (q1_ref, k1_ref, v1_ref), (q2_ref, k2_ref, v2_ref))

    @pl.when(pl.program_id(0) == 0)
    def _():
        _fill_bf16(w_hbm, w_ref, ag_ref, sem, tm, ag_ref.shape[2])
        _fill_bf16(wgb_hbm, wgb_ref, ag_ref, sem, tm)

    def project_group(gi, slot):
        c0 = gi * GROUP_WIDTH
        for i, base in enumerate((_Q0, _K0, _V0)):
            _project_slabs(hb_ref, w_ref, base + c0, p_ref.at[slot + i])

    def emit_group(gi, slot):
        q_ref, k_ref, v_ref = qkv_refs[gi]
        _emit_residue_major(p_ref.at[slot], q_ref, cs_ref.at[2], cs_ref.at[3])
        _emit_residue_major(p_ref.at[slot + 1], k_ref, cs_ref.at[0], cs_ref.at[1])
        _emit_residue_major(p_ref.at[slot + 2], v_ref)

    hb = hb_ref[...]
    a_ref[0] = _dot(hb, w_ref[:, _U0:_U0 + GMLP_WIDTH])
    a_ref[1] = _dot(hb, w_ref[:, _VG0:_VG0 + GMLP_WIDTH])
    n_freq = HEAD_DIM // 2
    per_row = LANES // n_freq
    ang = pos_ref[...].astype(F32) * invf_ref[...]
    lane = lax.broadcasted_iota(jnp.int32, (1, LANES), 1)
    first_half = (lane % HEAD_DIM) < n_freq
    lane_group = lane // n_freq
    for k, packed in enumerate((jnp.cos(ang), jnp.sin(ang))):
        rolled = [packed] + [pltpu.roll(packed, n_freq * s, 1) for s in range(1, per_row)]
        for a in range(per_row):
            spread = rolled[(0 - a) % per_row]
            for b in range(1, per_row):
                spread = jnp.where(lane_group == b, rolled[(b - a) % per_row], spread)
            if k == 1:
                spread = jnp.where(first_half, -spread, spread)
            rows = pl.ds(a, tm // per_row, stride=per_row)
            cs_ref[k, rows, :] = spread
            cs_ref[k + 2, rows, :] = spread * _Q_SCALE

    hb = hb_ref[...]
    g_ref[0] = _dot(hb, w_ref[:, _GM0:_GM0 + D_MODEL]) + bg_ref[:, D_MODEL:]
    g_ref[1] = _dot(hb, w_ref[:, _GA0:_GA0 + D_MODEL]) + bg_ref[:, :D_MODEL]
    vgn_ref[...] = _layer_norm(_gelu(a_ref[1]), lng_ref[...], lnb_ref[...]).astype(BF16)

    project_group(0, 0)
    n_chunk = tm // GMLP_CHUNK
    gdim = GMLP_WIDTH // GMLP_GROUPS
    row = lax.broadcasted_iota(jnp.int32, (GMLP_CHUNK, GMLP_CHUNK), 0)
    col = lax.broadcasted_iota(jnp.int32, (GMLP_CHUNK, GMLP_CHUNK), 1)
    causal = col <= row
    for g in range(GMLP_GROUPS):
        ws = jnp.where(causal, ws_ref[g], 0.0).astype(BF16)
        rhs = jnp.concatenate(
            [vgn_ref[c * GMLP_CHUNK:(c + 1) * GMLP_CHUNK, g * gdim:(g + 1) * gdim] for c in range(n_chunk)],
            axis=1)
        mixed = _dot(ws, rhs)
        for c in range(n_chunk):
            a_ref[1, c * GMLP_CHUNK:(c + 1) * GMLP_CHUNK, g * gdim:(g + 1) * gdim] = (
                mixed[:, c * gdim:(c + 1) * gdim] + bs_ref[:, g * gdim:(g + 1) * gdim])
    a_ref[0] = _gelu(a_ref[0])

    project_group(1, 3)
    vgn_ref[...] = (a_ref[0] * a_ref[1]).astype(BF16)
    emit_group(0, 0)
    ga_ref[...] = jax.nn.sigmoid(g_ref[1]).astype(BF16)

    project_group(2, 0)
    emit_group(1, 3)
    g_ref[0] = jax.nn.sigmoid(g_ref[0])

    gm_ref[...] = (g_ref[0] * _dot(vgn_ref[...], wgb_ref[...])).astype(BF16)
    emit_group(2, 0)


def _mixer_in(hb, pos, invf, w_in, b_gates, ln_g, ln_b, w_s, b_s, w_gb, batch, tm):
    t = hb.shape[0]
    seq = t // batch
    in_width = w_in.shape[1]
    qkv_specs, qkv_shapes = [], []
    for _, d in ATTN_PATTERNS:
        qkv_specs += [_residue_major_spec(d, tm, GROUP_WIDTH, seq // tm)] * 3
        qkv_shapes += [jax.ShapeDtypeStruct((batch, d, seq // d, GROUP_WIDTH), BF16)] * 3
    return pl.pallas_call(
        _mixer_in_kernel,
        grid=(t // tm,),
        in_specs=[_rows(tm, D_MODEL), _rows(tm * (HEAD_DIM // 2) // LANES, LANES), _resident((1, LANES)), _HBM,
                  _resident((1, 2 * D_MODEL)), _resident((1, GMLP_WIDTH)), _resident((1, GMLP_WIDTH)),
                  _resident((GMLP_GROUPS, GMLP_CHUNK, GMLP_CHUNK)), _resident((GMLP_CHUNK, GMLP_WIDTH)),
                  _HBM],
        out_specs=qkv_specs + [_rows(tm, D_MODEL), _rows(tm, D_MODEL)],
        out_shape=qkv_shapes + [jax.ShapeDtypeStruct((t, D_MODEL), BF16)] * 2,
        scratch_shapes=[pltpu.VMEM((FILL_SLOTS, tm, D_MODEL), F32),
                        pltpu.VMEM((6, N_SLAB, tm, LANES), F32), pltpu.VMEM((4, tm, LANES), F32),
                        pltpu.VMEM((tm, GMLP_WIDTH), BF16)]
                       + _weight_scratch([(D_MODEL, in_width), (GMLP_WIDTH, D_MODEL)]),
        compiler_params=_params(1),
        name="mixer_in",
    )(hb, pos, invf, w_in, b_gates, ln_g, ln_b, w_s, b_s, w_gb)


def _attn_kernel(q_ref, k_ref, v_ref, acc_ref, stat_ref, *, n_blk):
    n_items = q_ref.shape[0] // ATTN_BLOCK
    has_prev = n_blk > 1
    row = lax.broadcasted_iota(jnp.int32, (2 * ATTN_BLOCK, ATTN_BLOCK), 0) % ATTN_BLOCK
    col = lax.broadcasted_iota(jnp.int32, (2 * ATTN_BLOCK, ATTN_BLOCK), 1)
    neg_inf = jnp.float32(-jnp.inf)
    bias_cur = jnp.where(col <= row, 0.0, neg_inf)
    bias_prev = jnp.where(col >= row, 0.0, neg_inf)
    lane = lax.broadcasted_iota(jnp.int32, (ATTN_BLOCK, LANES), 1)
    low_head = lane < HEAD_DIM
    n_keys = 2 * ATTN_BLOCK if has_prev else ATTN_BLOCK
    low_head_keys = lax.broadcasted_iota(jnp.int32, (n_keys, LANES), 1) < HEAD_DIM
    contract_last = (((1,), (1,)), ((), ()))

    def item(j, carry):
        r0 = pl.multiple_of(j * ATTN_BLOCK, ATTN_BLOCK)
        if has_prev:
            p0 = pl.multiple_of(jnp.maximum(j - 1, 0) * ATTN_BLOCK, ATTN_BLOCK)
            prev_bias = bias_prev + jnp.where(j % n_blk > 0, 0.0, neg_inf)
        scores = []
        for pair in range(N_SLAB):
            sl = slice(pair * LANES, (pair + 1) * LANES)
            q = q_ref[pl.ds(r0, ATTN_BLOCK), sl]
            zero = jnp.zeros_like(q)
            q2 = jnp.concatenate([jnp.where(low_head, q, zero), jnp.where(low_head, zero, q)], axis=0)
            keys = k_ref[pl.ds(r0, ATTN_BLOCK), sl]
            if has_prev:
                keys = jnp.concatenate([k_ref[pl.ds(p0, ATTN_BLOCK), sl], keys], axis=0)
            s = lax.dot_general(q2, keys, contract_last, preferred_element_type=F32)
            if has_prev:
                s = s + jnp.concatenate([prev_bias, bias_cur], axis=1)
            else:
                s = s + bias_cur
            scores.append(s)
        maxes = [jnp.max(s, axis=1, keepdims=True) for s in scores]
        probs = [jnp.exp2(s - m).astype(BF16) for s, m in zip(scores, maxes)]
        stat = jnp.zeros((ATTN_BLOCK, LANES), F32)
        for pair in range(N_SLAB):
            sl = slice(pair * LANES, (pair + 1) * LANES)
            vals = v_ref[pl.ds(r0, ATTN_BLOCK), sl]
            if has_prev:
                vals = jnp.concatenate([v_ref[pl.ds(p0, ATTN_BLOCK), sl], vals], axis=0)
            one = jnp.ones_like(vals)
            r_even = _dot(probs[pair][:ATTN_BLOCK], jnp.where(low_head_keys, vals, one))
            r_odd = _dot(probs[pair][ATTN_BLOCK:], jnp.where(low_head_keys, one, vals))
            acc_ref[pl.ds(r0, ATTN_BLOCK), sl] = jnp.where(low_head, r_even, r_odd).astype(BF16)
            m_even, m_odd = maxes[pair][:ATTN_BLOCK], maxes[pair][ATTN_BLOCK:]
            for h, m_h, r_h in ((2 * pair, m_even, r_even), (2 * pair + 1, m_odd, r_odd)):
                stat = jnp.where(lane == _stat_lane(h), m_h, stat)
                stat = jnp.where(lane == _stat_lane(h) + _DEN_SHIFT, r_h, stat)
        stat_ref[pl.ds(r0, ATTN_BLOCK), :] = stat
        return carry

    lax.fori_loop(0, n_items, item, 0, unroll=16)


def _attention(q, k, v):
    b, d, l, _ = q.shape
    n_blk = l // ATTN_BLOCK
    seq = pl.BlockSpec((None, d * l, GROUP_WIDTH), lambda bi: (bi, 0, 0))
    stat_spec = pl.BlockSpec((None, d * l, LANES), lambda bi: (bi, 0, 0))
    q, k, v = (a.reshape(b, d * l, GROUP_WIDTH) for a in (q, k, v))
    acc, stat = pl.pallas_call(
        functools.partial(_attn_kernel, n_blk=n_blk),
        grid=(b,),
        in_specs=[seq, seq, seq],
        out_specs=[seq, stat_spec],
        out_shape=[jax.ShapeDtypeStruct(q.shape, BF16), jax.ShapeDtypeStruct((b, d * l, LANES), F32)],
        compiler_params=_params(1),
        name=f"band_attention_d{d}",
    )(q, k, v)
    return acc.reshape(b, d, l, GROUP_WIDTH), stat.reshape(b, d, l, LANES)


def _load_token_major(in_ref, scr_ref):
    d, rows, width = in_ref.shape
    if d == 1:
        return in_ref[0].astype(F32)
    n_slab = width // LANES
    for r in range(d):
        x = in_ref[r].astype(F32)
        for j in range(n_slab):
            scr_ref[j, pl.ds(r, rows, stride=d), :] = x[:, j * LANES:(j + 1) * LANES]
    return jnp.concatenate([scr_ref[j] for j in range(n_slab)], axis=1)


def _mixer_out_kernel(a0_ref, a1_ref, a2_ref, s0_ref, s1_ref, s2_ref, ga_ref, gm_ref, h1_ref, expand_ref,
                      wab_hbm, wout_hbm, g2_ref, b2_ref, wg_hbm, wu_hbm, wd_hbm, g3_ref, b3_ref, out_ref,
                      ascr_ref, sscr_ref, wab_ref, wout_ref, wg_ref, wu_ref, wd_ref, stage_ref, sem):
    @pl.when(pl.program_id(0) == 0)
    def _():
        for w_hbm, w_ref in ((wab_hbm, wab_ref), (wout_hbm, wout_ref), (wg_hbm, wg_ref), (wu_hbm, wu_ref),
                             (wd_hbm, wd_ref)):
            _fill_bf16(w_hbm, w_ref, stage_ref, sem, FFN_FILL_ROWS)

    tm = h1_ref.shape[0]
    stats = [_load_token_major(s_ref, sscr_ref.at[i]) for i, s_ref in enumerate((s0_ref, s1_ref, s2_ref))]
    accs = [_load_token_major(a_ref, ascr_ref.at[i]) for i, a_ref in enumerate((a0_ref, a1_ref, a2_ref))]
    lane = lax.broadcasted_iota(jnp.int32, (1, LANES), 1)
    is_max_lane = functools.reduce(jnp.logical_or, [lane == _stat_lane(h) for h in range(HEADS_PER_GROUP)])

    def mix_rows(rows):
        st = [x[rows] for x in stats]
        m = jnp.maximum(jnp.maximum(st[0], st[1]), st[2])
        es = [jnp.exp2(x - m) for x in st]
        dens = [pltpu.roll(x, LANES - _DEN_SHIFT, 1) for x in st]
        den = es[0] * dens[0] + es[1] * dens[1] + es[2] * dens[2]
        inv_den = 1.0 / jnp.where(is_max_lane, den, 1.0)
        y = None
        for e, acc in zip(es, accs):
            w = e * inv_den
            w_hi = w.astype(BF16)
            w_lo = (w - w_hi.astype(F32)).astype(BF16)
            w_wide = _dot(jnp.concatenate([w_hi, w_lo], axis=1), expand_ref[...])
            term = w_wide * acc[rows]
            y = term if y is None else y + term
        branch_a = _dot(y.astype(BF16), wab_ref[...])
        merged = ga_ref[rows, :].astype(F32) * branch_a + gm_ref[rows, :].astype(F32)
        mix = _dot(merged.astype(BF16), wout_ref[...])
        return _layer_norm(ALPHA * h1_ref[rows, :] + mix, g2_ref[...], b2_ref[...])

    h2 = mix_rows(slice(0, tm))
    ffn = _swiglu(h2.astype(BF16), wg_ref, wu_ref, wd_ref)
    out_ref[...] = _layer_norm(ALPHA * h2 + 0.5 * ffn, g3_ref[...], b3_ref[...])


def _mixer_out(accs, stats, ga, gm, h1, expand, wab, wout, g2, b2, wg, wu, wd, g3, b3, tm):
    t = h1.shape[0]
    tiles_per_seq = accs[0].shape[1] * accs[0].shape[2] // tm
    acc_specs = [_residue_major_spec(a.shape[1], tm, GROUP_WIDTH, tiles_per_seq) for a in accs]
    stat_specs = [_residue_major_spec(s.shape[1], tm, LANES, tiles_per_seq) for s in stats]
    return pl.pallas_call(
        _mixer_out_kernel,
        grid=(t // tm,),
        scratch_shapes=[pltpu.VMEM((3, N_SLAB, tm, LANES), F32), pltpu.VMEM((3, 1, tm, LANES), F32)]
                       + _weight_scratch([(GROUP_WIDTH, D_MODEL), (D_MODEL, D_MODEL), (D_MODEL, D_FF),
                                          (D_MODEL, D_FF), (D_FF, D_MODEL)], (FILL_SLOTS - 1, FFN_FILL_ROWS, D_FF)),
        in_specs=acc_specs + stat_specs
                 + [_rows(tm, D_MODEL)] * 3
                 + [_resident((2 * LANES, GROUP_WIDTH)), _HBM, _HBM, _resident((1, D_MODEL)),
                    _resident((1, D_MODEL)), _HBM, _HBM, _HBM, _resident((1, D_MODEL)), _resident((1, D_MODEL))],
        out_specs=_rows(tm, D_MODEL),
        out_shape=jax.ShapeDtypeStruct((t, D_MODEL), F32),
        compiler_params=_params(1),
        name="mixer_out_ffn",
    )(*accs, *stats, ga, gm, h1, expand, wab, wout, g2, b2, wg, wu, wd, g3, b3)


def kernel(x, positions, ffn1_w_gate, ffn1_w_up, ffn1_w_down, ln1_g, ln1_b, w_in, b_gates, gmlp_ln_g, gmlp_ln_b, gmlp_w_s, gmlp_b_s, w_attn_branch, w_gmlp_branch, w_out, ln2_g, ln2_b, ffn2_w_gate, ffn2_w_up, ffn2_w_down, ln3_g, ln3_b):
    b, s, d_model = x.shape
    assert d_model == D_MODEL and ln1_g.shape[0] == DEPTH == 1
    t = b * s
    tm = 512
    assert t % tm == 0 and tm % GMLP_CHUNK == 0 and s % tm == 0

    lane = jnp.arange(LANES) % (HEAD_DIM // 2)
    invf = (ROPE_THETA ** (-(2 * lane).astype(F32) / HEAD_DIM)).reshape(1, LANES)
    stat_lane_of_col = jnp.array([_stat_lane(c // HEAD_DIM) for c in range(GROUP_WIDTH)])
    expand = (jnp.arange(LANES)[:, None] == stat_lane_of_col[None, :]).astype(BF16)
    expand = jnp.concatenate([expand, expand], axis=0)
    b_s = jnp.repeat(gmlp_b_s[0].T, GMLP_WIDTH // GMLP_GROUPS, axis=1)

    h = x.reshape(t, D_MODEL)
    h1, h1b = _ffn_ln(h, ffn1_w_gate[0], ffn1_w_up[0], ffn1_w_down[0], ln1_g, ln1_b, tm)
    pos_packed = jnp.repeat(positions.reshape(t * (HEAD_DIM // 2) // LANES, -1), HEAD_DIM // 2, axis=1)
    outs = _mixer_in(h1b, pos_packed, invf, w_in[0], b_gates, gmlp_ln_g, gmlp_ln_b,
                     gmlp_w_s[0], b_s, w_gmlp_branch[0], b, tm)
    ga, gm = outs[9], outs[10]
    accs, stats = [], []
    for gi in range(N_ATTN_GROUPS):
        acc, stat = _attention(*outs[3 * gi:3 * gi + 3])
        accs.append(acc)
        stats.append(stat)
    out = _mixer_out(accs, stats, ga, gm, h1, expand, w_attn_branch[0], w_out[0],
                     ln2_g, ln2_b, ffn2_w_gate[0], ffn2_w_up[0], ffn2_w_down[0], ln3_g, ln3_b, tm)
    return out.reshape(b, s, D_MODEL)
```

```python
import functools
import math

import jax
import jax.numpy as jnp
from jax import lax
from jax.experimental import pallas as pl
from jax.experimental.pallas import tpu as pltpu

D_MODEL = 1024
HEAD_DIM = 64
HEADS_PER_GROUP = 8
ATTN_PATTERNS = ((128, 1), (512, 4), (2048, 16))
N_ATTN_GROUPS = len(ATTN_PATTERNS)
GROUP_WIDTH = HEADS_PER_GROUP * HEAD_DIM
ATTN_WIDTH = N_ATTN_GROUPS * GROUP_WIDTH
ATTN_BLOCK = 128
ROPE_THETA = 10000.0
GMLP_CHUNK = 128
GMLP_GROUPS = 8
GMLP_WIDTH = D_MODEL
D_FF = 2816
DEPTH = 1
ALPHA = (2 * DEPTH) ** 0.25
LN_EPS = 1e-5

LANES = 128
VMEM_LIMIT_BYTES = 60 * 1024 * 1024
N_SLAB = GROUP_WIDTH // LANES

_Q0, _K0, _V0 = 0, ATTN_WIDTH, 2 * ATTN_WIDTH
_U0 = 3 * ATTN_WIDTH
_VG0 = _U0 + GMLP_WIDTH
_GA0 = _VG0 + GMLP_WIDTH
_GM0 = _GA0 + D_MODEL

_Q_SCALE = HEAD_DIM ** -0.5 * math.log2(math.e)

_DEN_SHIFT = 8


def _stat_lane(h):
    return HEAD_DIM + h if h % 2 == 0 else h


BF16 = jnp.bfloat16
F32 = jnp.float32


def _layer_norm(x, g, b):
    mu = jnp.mean(x, axis=-1, keepdims=True)
    xc = x - mu
    var = jnp.mean(xc * xc, axis=-1, keepdims=True)
    return xc * lax.rsqrt(var + LN_EPS) * g + b


def _dot(a, b):
    return jnp.dot(a, b, preferred_element_type=F32)


def _swiglu(xb, wg_ref, wu_ref, wd_ref):
    g = _dot(xb, wg_ref[...])
    u = _dot(xb, wu_ref[...])
    a = g * jax.nn.sigmoid(g) * u
    return _dot(a.astype(BF16), wd_ref[...])


def _gelu(x):
    return 0.5 * x * (1.0 + lax.erf(x * math.sqrt(0.5)))


def _resident(shape):
    return pl.BlockSpec(shape, lambda *_: (0,) * len(shape), pipeline_mode=pl.Buffered(1))


def _rows(tm, width):
    return pl.BlockSpec((tm, width), lambda i: (i, 0))


def _residue_major_spec(d, tm, width, tiles_per_seq):
    return pl.BlockSpec((None, d, tm // d, width), lambda i: (i // tiles_per_seq, 0, i % tiles_per_seq, 0))


def _params(n_axes):
    return pltpu.CompilerParams(dimension_semantics=("arbitrary",) * n_axes,
                                vmem_limit_bytes=VMEM_LIMIT_BYTES)


_HBM = pl.BlockSpec(memory_space=pl.ANY)


FILL_SLOTS = 4


def _fill_bf16(w_hbm, w_vmem, stage, sem, rows, cols=None):
    k, n = w_hbm.shape
    cols = n if cols is None else cols
    slots = min(stage.shape[0], sem.shape[0])
    assert k % rows == 0 and rows <= stage.shape[1] and cols <= stage.shape[2]
    blocks = [(r, c, min(cols, n - c)) for r in range(0, k, rows) for c in range(0, n, cols)]

    def copy(i):
        r, c, width = blocks[i]
        dst = stage.at[i % slots, pl.ds(0, rows), pl.ds(0, width)]
        return pltpu.make_async_copy(w_hbm.at[pl.ds(r, rows), pl.ds(c, width)], dst, sem.at[i % slots])

    for i in range(min(slots - 1, len(blocks))):
        copy(i).start(priority=i % 2)
    for i, (r, c, width) in enumerate(blocks):
        nxt = i + slots - 1
        if nxt < len(blocks):
            copy(nxt).start(priority=nxt % 2)
        copy(i).wait()
        w_vmem[r:r + rows, c:c + width] = stage[i % slots, 0:rows, 0:width].astype(BF16)


def _weight_scratch(shapes, stage_shape=None):
    stage = [] if stage_shape is None else [pltpu.VMEM(stage_shape, F32)]
    return [pltpu.VMEM(s, BF16) for s in shapes] + stage + [pltpu.SemaphoreType.DMA((FILL_SLOTS,))]


FFN_FILL_ROWS = 256


def _ffn_ln_kernel(x_ref, wg_hbm, wu_hbm, wd_hbm, g_ref, b_ref, h_ref, hb_ref,
                   wg_ref, wu_ref, wd_ref, stage_ref, sem):
    @pl.when(pl.program_id(0) == 0)
    def _():
        for w_hbm, w_ref in ((wg_hbm, wg_ref), (wu_hbm, wu_ref), (wd_hbm, wd_ref)):
            _fill_bf16(w_hbm, w_ref, stage_ref, sem, FFN_FILL_ROWS)

    x = x_ref[...]
    y = _swiglu(x.astype(BF16), wg_ref, wu_ref, wd_ref)
    h = _layer_norm(ALPHA * x + 0.5 * y, g_ref[...], b_ref[...])
    h_ref[...] = h
    hb_ref[...] = h.astype(BF16)


def _ffn_ln(x, wg, wu, wd, g, b, tm):
    t = x.shape[0]
    return pl.pallas_call(
        _ffn_ln_kernel,
        grid=(t // tm,),
        in_specs=[_rows(tm, D_MODEL), _HBM, _HBM, _HBM, _resident((1, D_MODEL)), _resident((1, D_MODEL))],
        out_specs=[_rows(tm, D_MODEL), _rows(tm, D_MODEL)],
        out_shape=[jax.ShapeDtypeStruct((t, D_MODEL), F32), jax.ShapeDtypeStruct((t, D_MODEL), BF16)],
        scratch_shapes=_weight_scratch([(D_MODEL, D_FF), (D_MODEL, D_FF), (D_FF, D_MODEL)],
                                       (FILL_SLOTS, FFN_FILL_ROWS, D_FF)),
        compiler_params=_params(1),
        name="ffn_ln",
    )(x, wg, wu, wd, g, b)


def _project_slabs(hb_ref, w_ref, col0, dst_ref):
    res = _dot(hb_ref[...], w_ref[:, col0:col0 + GROUP_WIDTH])
    for j in range(N_SLAB):
        dst_ref[j] = res[:, j * LANES:(j + 1) * LANES]


def _emit_residue_major(p_ref, out_ref, cos_ref=None, sin_ref=None):
    d, rows, _ = out_ref.shape
    lane = lax.broadcasted_iota(jnp.int32, (1, LANES), 1)
    first_half = (lane % HEAD_DIM) < (HEAD_DIM // 2)
    for r in range(d):
        rs = pl.ds(r, rows, stride=d) if d > 1 else slice(None)
        if cos_ref is not None:
            c, s = cos_ref[rs, :], sin_ref[rs, :]
        blocks = []
        for j in range(N_SLAB):
            x = p_ref[j, rs, :]
            if cos_ref is not None:
                partner = jnp.where(first_half, pltpu.roll(x, LANES - HEAD_DIM // 2, 1),
                                    pltpu.roll(x, HEAD_DIM // 2, 1))
                x = x * c + partner * s
            blocks.append(x)
        out_ref[r] = jnp.concatenate(blocks, axis=1).astype(out_ref.dtype)


def _mixer_in_kernel(hb_ref, pos_ref, invf_ref, w_hbm, bg_ref, lng_ref, lnb_ref, ws_ref, bs_ref, wgb_hbm,
                     q0_ref, k0_ref, v0_ref, q1_ref, k1_ref, v1_ref, q2_ref, k2_ref, v2_ref,
                     ga_ref, gm_ref, ag_ref, p_ref, cs_ref, vgn_ref, w_ref, wgb_ref, sem):
    tm = hb_ref.shape[0]
    a_ref, g_ref = ag_ref.at[pl.ds(0, 2)], ag_ref.at[pl.ds(2, 2)]
    qkv_refs = ((q0_ref, k0_ref, v0_ref), (q1_ref, k1_ref, v1_ref), (q2_ref, k2_ref, v2_ref))

    @pl.when(pl.program_id(0) == 0)
    def _():
        _fill_bf16(w_hbm, w_ref, ag_ref, sem, tm, ag_ref.shape[2])
        _fill_bf16(wgb_hbm, wgb_ref, ag_ref, sem, tm)

    def project_group(gi, slot):
        c0 = gi * GROUP_WIDTH
        for i, base in enumerate((_Q0, _K0, _V0)):
            _project_slabs(hb_ref, w_ref, base + c0, p_ref.at[slot + i])

    def emit_group(gi, slot):
        q_ref, k_ref, v_ref = qkv_refs[gi]
        _emit_residue_major(p_ref.at[slot], q_ref, cs_ref.at[2], cs_ref.at[3])
        _emit_residue_major(p_ref.at[slot + 1], k_ref, cs_ref.at[0], cs_ref.at[1])
        _emit_residue_major(p_ref.at[slot + 2], v_ref)

    hb = hb_ref[...]
    a_ref[0] = _dot(hb, w_ref[:, _U0:_U0 + GMLP_WIDTH])
    a_ref[1] = _dot(hb, w_ref[:, _VG0:_VG0 + GMLP_WIDTH])
    n_freq = HEAD_DIM // 2
    per_row = LANES // n_freq
    ang = pos_ref[...].astype(F32) * invf_ref[...]
    lane = lax.broadcasted_iota(jnp.int32, (1, LANES), 1)
    first_half = (lane % HEAD_DIM) < n_freq
    lane_group = lane // n_freq
    for k, packed in enumerate((jnp.cos(ang), jnp.sin(ang))):
        rolled = [packed] + [pltpu.roll(packed, n_freq * s, 1) for s in range(1, per_row)]
        for a in range(per_row):
            spread = rolled[(0 - a) % per_row]
            for b in range(1, per_row):
                spread = jnp.where(lane_group == b, rolled[(b - a) % per_row], spread)
            if k == 1:
                spread = jnp.where(first_half, -spread, spread)
            rows = pl.ds(a, tm // per_row, stride=per_row)
            cs_ref[k, rows, :] = spread
            cs_ref[k + 2, rows, :] = spread * _Q_SCALE

    hb = hb_ref[...]
    g_ref[0] = _dot(hb, w_ref[:, _GM0:_GM0 + D_MODEL]) + bg_ref[:, D_MODEL:]
    g_ref[1] = _dot(hb, w_ref[:, _GA0:_GA0 + D_MODEL]) + bg_ref[:, :D_MODEL]
    vgn_ref[...] = _layer_norm(_gelu(a_ref[1]), lng_ref[...], lnb_ref[...]).astype(BF16)

    project_group(0, 0)
    n_chunk = tm // GMLP_CHUNK
    gdim = GMLP_WIDTH // GMLP_GROUPS
    row = lax.broadcasted_iota(jnp.int32, (GMLP_CHUNK, GMLP_CHUNK), 0)
    col = lax.broadcasted_iota(jnp.int32, (GMLP_CHUNK, GMLP_CHUNK), 1)
    causal = col <= row
    for g in range(GMLP_GROUPS):
        ws = jnp.where(causal, ws_ref[g], 0.0).astype(BF16)
        rhs = jnp.concatenate(
            [vgn_ref[c * GMLP_CHUNK:(c + 1) * GMLP_CHUNK, g * gdim:(g + 1) * gdim] for c in range(n_chunk)],
            axis=1)
        mixed = _dot(ws, rhs)
        for c in range(n_chunk):
            a_ref[1, c * GMLP_CHUNK:(c + 1) * GMLP_CHUNK, g * gdim:(g + 1) * gdim] = (
                mixed[:, c * gdim:(c + 1) * gdim] + bs_ref[:, g * gdim:(g + 1) * gdim])
    a_ref[0] = _gelu(a_ref[0])

    project_group(1, 3)
    vgn_ref[...] = (a_ref[0] * a_ref[1]).astype(BF16)
    emit_group(0, 0)
    ga_ref[...] = jax.nn.sigmoid(g_ref[1]).astype(BF16)

    project_group(2, 0)
    emit_group(1, 3)
    g_ref[0] = jax.nn.sigmoid(g_ref[0])

    gm_ref[...] = (g_ref[0] * _dot(vgn_ref[...], wgb_ref[...])).astype(BF16)
    emit_group(2, 0)


def _mixer_in(hb, pos, invf, w_in, b_gates, ln_g, ln_b, w_s, b_s, w_gb, batch, tm):
    t = hb.shape[0]
    seq = t // batch
    in_width = w_in.shape[1]
    qkv_specs, qkv_shapes = [], []
    for _, d in ATTN_PATTERNS:
        qkv_specs += [_residue_major_spec(d, tm, GROUP_WIDTH, seq // tm)] * 3
        qkv_shapes += [jax.ShapeDtypeStruct((batch, d, seq // d, GROUP_WIDTH), BF16)] * 3
    return pl.pallas_call(
        _mixer_in_kernel,
        grid=(t // tm,),
        in_specs=[_rows(tm, D_MODEL), _rows(tm * (HEAD_DIM // 2) // LANES, LANES), _resident((1, LANES)), _HBM,
                  _resident((1, 2 * D_MODEL)), _resident((1, GMLP_WIDTH)), _resident((1, GMLP_WIDTH)),
                  _resident((GMLP_GROUPS, GMLP_CHUNK, GMLP_CHUNK)), _resident((GMLP_CHUNK, GMLP_WIDTH)),
                  _HBM],
        out_specs=qkv_specs + [_rows(tm, D_MODEL), _rows(tm, D_MODEL)],
        out_shape=qkv_shapes + [jax.ShapeDtypeStruct((t, D_MODEL), BF16)] * 2,
        scratch_shapes=[pltpu.VMEM((FILL_SLOTS, tm, D_MODEL), F32),
                        pltpu.VMEM((6, N_SLAB, tm, LANES), F32), pltpu.VMEM((4, tm, LANES), F32),
                        pltpu.VMEM((tm, GMLP_WIDTH), BF16)]
                       + _weight_scratch([(D_MODEL, in_width), (GMLP_WIDTH, D_MODEL)]),
        compiler_params=_params(1),
        name="mixer_in",
    )(hb, pos, invf, w_in, b_gates, ln_g, ln_b, w_s, b_s, w_gb)


def _attn_kernel(q_ref, k_ref, v_ref, acc_ref, stat_ref, *, n_blk):
    n_items = q_ref.shape[0] // ATTN_BLOCK
    has_prev = n_blk > 1
    row = lax.broadcasted_iota(jnp.int32, (2 * ATTN_BLOCK, ATTN_BLOCK), 0) % ATTN_BLOCK
    col = lax.broadcasted_iota(jnp.int32, (2 * ATTN_BLOCK, ATTN_BLOCK), 1)
    neg_inf = jnp.float32(-jnp.inf)
    bias_cur = jnp.where(col <= row, 0.0, neg_inf)
    bias_prev = jnp.where(col >= row, 0.0, neg_inf)
    lane = lax.broadcasted_iota(jnp.int32, (ATTN_BLOCK, LANES), 1)
    low_head = lane < HEAD_DIM
    n_keys = 2 * ATTN_BLOCK if has_prev else ATTN_BLOCK
    low_head_keys = lax.broadcasted_iota(jnp.int32, (n_keys, LANES), 1) < HEAD_DIM
    contract_last = (((1,), (1,)), ((), ()))

    def item(j, carry):
        r0 = pl.multiple_of(j * ATTN_BLOCK, ATTN_BLOCK)
        if has_prev:
            p0 = pl.multiple_of(jnp.maximum(j - 1, 0) * ATTN_BLOCK, ATTN_BLOCK)
            prev_bias = bias_prev + jnp.where(j % n_blk > 0, 0.0, neg_inf)
        scores = []
        for pair in range(N_SLAB):
            sl = slice(pair * LANES, (pair + 1) * LANES)
            q = q_ref[pl.ds(r0, ATTN_BLOCK), sl]
            zero = jnp.zeros_like(q)
            q2 = jnp.concatenate([jnp.where(low_head, q, zero), jnp.where(low_head, zero, q)], axis=0)
            keys = k_ref[pl.ds(r0, ATTN_BLOCK), sl]
            if has_prev:
                keys = jnp.concatenate([k_ref[pl.ds(p0, ATTN_BLOCK), sl], keys], axis=0)
            s = lax.dot_general(q2, keys, contract_last, preferred_element_type=F32)
            if has_prev:
                s = s + jnp.concatenate([prev_bias, bias_cur], axis=1)
            else:
                s = s + bias_cur
            scores.append(s)
        maxes = [jnp.max(s, axis=1, keepdims=True) for s in scores]
        probs = [jnp.exp2(s - m).astype(BF16) for s, m in zip(scores, maxes)]
        stat = jnp.zeros((ATTN_BLOCK, LANES), F32)
        for pair in range(N_SLAB):
            sl = slice(pair * LANES, (pair + 1) * LANES)
            vals = v_ref[pl.ds(r0, ATTN_BLOCK), sl]
            if has_prev:
                vals = jnp.concatenate([v_ref[pl.ds(p0, ATTN_BLOCK), sl], vals], axis=0)
            one = jnp.ones_like(vals)
            r_even = _dot(probs[pair][:ATTN_BLOCK], jnp.where(low_head_keys, vals, one))
            r_odd = _dot(probs[pair][ATTN_BLOCK:], jnp.where(low_head_keys, one, vals))
            acc_ref[pl.ds(r0, ATTN_BLOCK), sl] = jnp.where(low_head, r_even, r_odd).astype(BF16)
            m_even, m_odd = maxes[pair][:ATTN_BLOCK], maxes[pair][ATTN_BLOCK:]
            for h, m_h, r_h in ((2 * pair, m_even, r_even), (2 * pair + 1, m_odd, r_odd)):
                stat = jnp.where(lane == _stat_lane(h), m_h, stat)
                stat = jnp.where(lane == _stat_lane(h) + _DEN_SHIFT, r_h, stat)
        stat_ref[pl.ds(r0, ATTN_BLOCK), :] = stat
        return carry

    lax.fori_loop(0, n_items, item, 0, unroll=16)


def _attention(q, k, v):
    b, d, l, _ = q.shape
    n_blk = l // ATTN_BLOCK
    seq = pl.BlockSpec((None, d * l, GROUP_WIDTH), lambda bi: (bi, 0, 0))
    stat_spec = pl.BlockSpec((None, d * l, LANES), lambda bi: (bi, 0, 0))
    q, k, v = (a.reshape(b, d * l, GROUP_WIDTH) for a in (q, k, v))
    acc, stat = pl.pallas_call(
        functools.partial(_attn_kernel, n_blk=n_blk),
        grid=(b,),
        in_specs=[seq, seq, seq],
        out_specs=[seq, stat_spec],
        out_shape=[jax.ShapeDtypeStruct(q.shape, BF16), jax.ShapeDtypeStruct((b, d * l, LANES), F32)],
        compiler_params=_params(1),
        name=f"band_attention_d{d}",
    )(q, k, v)
    return acc.reshape(b, d, l, GROUP_WIDTH), stat.reshape(b, d, l, LANES)


def _load_token_major(in_ref, scr_ref):
    d, rows, width = in_ref.shape
    if d == 1:
        return in_ref[0].astype(F32)
    n_slab = width // LANES
    for r in range(d):
        x = in_ref[r].astype(F32)
        for j in range(n_slab):
            scr_ref[j, pl.ds(r, rows, stride=d), :] = x[:, j * LANES:(j + 1) * LANES]
    return jnp.concatenate([scr_ref[j] for j in range(n_slab)], axis=1)


def _mixer_out_kernel(a0_ref, a1_ref, a2_ref, s0_ref, s1_ref, s2_ref, ga_ref, gm_ref, h1_ref, expand_ref,
                      wab_hbm, wout_hbm, g2_ref, b2_ref, wg_hbm, wu_hbm, wd_hbm, g3_ref, b3_ref, out_ref,
                      ascr_ref, sscr_ref, wab_ref, wout_ref, wg_ref, wu_ref, wd_ref, stage_ref, sem):
    @pl.when(pl.program_id(0) == 0)
    def _():
        for w_hbm, w_ref in ((wab_hbm, wab_ref), (wout_hbm, wout_ref), (wg_hbm, wg_ref), (wu_hbm, wu_ref),
                             (wd_hbm, wd_ref)):
            _fill_bf16(w_hbm, w_ref, stage_ref, sem, FFN_FILL_ROWS)

    tm = h1_ref.shape[0]
    stats = [_load_token_major(s_ref, sscr_ref.at[i]) for i, s_ref in enumerate((s0_ref, s1_ref, s2_ref))]
    accs = [_load_token_major(a_ref, ascr_ref.at[i]) for i, a_ref in enumerate((a0_ref, a1_ref, a2_ref))]
    lane = lax.broadcasted_iota(jnp.int32, (1, LANES), 1)
    is_max_lane = functools.reduce(jnp.logical_or, [lane == _stat_lane(h) for h in range(HEADS_PER_GROUP)])

    def mix_rows(rows):
        st = [x[rows] for x in stats]
        m = jnp.maximum(jnp.maximum(st[0], st[1]), st[2])
        es = [jnp.exp2(x - m) for x in st]
        dens = [pltpu.roll(x, LANES - _DEN_SHIFT, 1) for x in st]
        den = es[0] * dens[0] + es[1] * dens[1] + es[2] * dens[2]
        inv_den = 1.0 / jnp.where(is_max_lane, den, 1.0)
        y = None
        for e, acc in zip(es, accs):
            w = e * inv_den
            w_hi = w.astype(BF16)
            w_lo = (w - w_hi.astype(F32)).astype(BF16)
            w_wide = _dot(jnp.concatenate([w_hi, w_lo], axis=1), expand_ref[...])
            term = w_wide * acc[rows]
            y = term if y is None else y + term
        branch_a = _dot(y.astype(BF16), wab_ref[...])
        merged = ga_ref[rows, :].astype(F32) * branch_a + gm_ref[rows, :].astype(F32)
        mix = _dot(merged.astype(BF16), wout_ref[...])
        return _layer_norm(ALPHA * h1_ref[rows, :] + mix, g2_ref[...], b2_ref[...])

    h2 = mix_rows(slice(0, tm))
    ffn = _swiglu(h2.astype(BF16), wg_ref, wu_ref, wd_ref)
    out_ref[...] = _layer_norm(ALPHA * h2 + 0.5 * ffn, g3_ref[...], b3_ref[...])


def _mixer_out(accs, stats, ga, gm, h1, expand, wab, wout, g2, b2, wg, wu, wd, g3, b3, tm):
    t = h1.shape[0]
    tiles_per_seq = accs[0].shape[1] * accs[0].shape[2] // tm
    acc_specs = [_residue_major_spec(a.shape[1], tm, GROUP_WIDTH, tiles_per_seq) for a in accs]
    stat_specs = [_residue_major_spec(s.shape[1], tm, LANES, tiles_per_seq) for s in stats]
    return pl.pallas_call(
        _mixer_out_kernel,
        grid=(t // tm,),
        scratch_shapes=[pltpu.VMEM((3, N_SLAB, tm, LANES), F32), pltpu.VMEM((3, 1, tm, LANES), F32)]
                       + _weight_scratch([(GROUP_WIDTH, D_MODEL), (D_MODEL, D_MODEL), (D_MODEL, D_FF),
                                          (D_MODEL, D_FF), (D_FF, D_MODEL)], (FILL_SLOTS - 1, FFN_FILL_ROWS, D_FF)),
        in_specs=acc_specs + stat_specs
                 + [_rows(tm, D_MODEL)] * 3
                 + [_resident((2 * LANES, GROUP_WIDTH)), _HBM, _HBM, _resident((1, D_MODEL)),
                    _resident((1, D_MODEL)), _HBM, _HBM, _HBM, _resident((1, D_MODEL)), _resident((1, D_MODEL))],
        out_specs=_rows(tm, D_MODEL),
        out_shape=jax.ShapeDtypeStruct((t, D_MODEL), F32),
        compiler_params=_params(1),
        name="mixer_out_ffn",
    )(*accs, *stats, ga, gm, h1, expand, wab, wout, g2, b2, wg, wu, wd, g3, b3)


def kernel(x, positions, ffn1_w_gate, ffn1_w_up, ffn1_w_down, ln1_g, ln1_b, w_in, b_gates, gmlp_ln_g, gmlp_ln_b, gmlp_w_s, gmlp_b_s, w_attn_branch, w_gmlp_branch, w_out, ln2_g, ln2_b, ffn2_w_gate, ffn2_w_up, ffn2_w_down, ln3_g, ln3_b):
    b, s, d_model = x.shape
    assert d_model == D_MODEL and ln1_g.shape[0] == DEPTH == 1
    t = b * s
    tm = 512
    assert t % tm == 0 and tm % GMLP_CHUNK == 0 and s % tm == 0

    lane = jnp.arange(LANES) % (HEAD_DIM // 2)
    invf = (ROPE_THETA ** (-(2 * lane).astype(F32) / HEAD_DIM)).reshape(1, LANES)
    stat_lane_of_col = jnp.array([_stat_lane(c // HEAD_DIM) for c in range(GROUP_WIDTH)])
    expand = (jnp.arange(LANES)[:, None] == stat_lane_of_col[None, :]).astype(BF16)
    expand = jnp.concatenate([expand, expand], axis=0)
    b_s = jnp.repeat(gmlp_b_s[0].T, GMLP_WIDTH // GMLP_GROUPS, axis=1)

    h = x.reshape(t, D_MODEL)
    h1, h1b = _ffn_ln(h, ffn1_w_gate[0], ffn1_w_up[0], ffn1_w_down[0], ln1_g, ln1_b, tm)
    pos_packed = jnp.repeat(positions.reshape(t * (HEAD_DIM // 2) // LANES, -1), HEAD_DIM // 2, axis=1)
    outs = _mixer_in(h1b, pos_packed, invf, w_in[0], b_gates, gmlp_ln_g, gmlp_ln_b,
                     gmlp_w_s[0], b_s, w_gmlp_branch[0], b, tm)
    ga, gm = outs[9], outs[10]
    accs, stats = [], []
    for gi in range(N_ATTN_GROUPS):
        acc, stat = _attention(*outs[3 * gi:3 * gi + 3])
        accs.append(acc)
        stats.append(stat)
    out = _mixer_out(accs, stats, ga, gm, h1, expand, w_attn_branch[0], w_out[0],
                     ln2_g, ln2_b, ffn2_w_gate[0], ffn2_w_up[0], ffn2_w_down[0], ln3_g, ln3_b, tm)
    return out.reshape(b, s, D_MODEL)
```

```python
import functools
import math

import jax
import jax.numpy as jnp
from jax import lax
from jax.experimental import pallas as pl
from jax.experimental.pallas import tpu as pltpu

D_MODEL = 1024
HEAD_DIM = 64
HEADS_PER_GROUP = 8
ATTN_PATTERNS = ((128, 1), (512, 4), (2048, 16))
N_ATTN_GROUPS = len(ATTN_PATTERNS)
GROUP_WIDTH = HEADS_PER_GROUP * HEAD_DIM
ATTN_WIDTH = N_ATTN_GROUPS * GROUP_WIDTH
ATTN_BLOCK = 128
ROPE_THETA = 10000.0
GMLP_CHUNK = 128
GMLP_GROUPS = 8
GMLP_WIDTH = D_MODEL
D_FF = 2816
DEPTH = 1
ALPHA = (2 * DEPTH) ** 0.25
LN_EPS = 1e-5

LANES = 128
VMEM_LIMIT_BYTES = 60 * 1024 * 1024
N_SLAB = GROUP_WIDTH // LANES

_Q0, _K0, _V0 = 0, ATTN_WIDTH, 2 * ATTN_WIDTH
_U0 = 3 * ATTN_WIDTH
_VG0 = _U0 + GMLP_WIDTH
_GA0 = _VG0 + GMLP_WIDTH
_GM0 = _GA0 + D_MODEL

_Q_SCALE = HEAD_DIM ** -0.5 * math.log2(math.e)

_DEN_SHIFT = 8


def _stat_lane(h):
    return HEAD_DIM + h if h % 2 == 0 else h


BF16 = jnp.bfloat16
F32 = jnp.float32


def _layer_norm(x, g, b):
    mu = jnp.mean(x, axis=-1, keepdims=True)
    xc = x - mu
    var = jnp.mean(xc * xc, axis=-1, keepdims=True)
    return xc * lax.rsqrt(var + LN_EPS) * g + b


def _dot(a, b):
    return jnp.dot(a, b, preferred_element_type=F32)


def _swiglu(xb, wg_ref, wu_ref, wd_ref):
    g = _dot(xb, wg_ref[...])
    u = _dot(xb, wu_ref[...])
    a = g * jax.nn.sigmoid(g) * u
    return _dot(a.astype(BF16), wd_ref[...])


def _gelu(x):
    return 0.5 * x * (1.0 + lax.erf(x * math.sqrt(0.5)))


def _resident(shape):
    return pl.BlockSpec(shape, lambda *_: (0,) * len(shape), pipeline_mode=pl.Buffered(1))


def _rows(tm, width):
    return pl.BlockSpec((tm, width), lambda i: (i, 0))


def _residue_major_spec(d, tm, width, tiles_per_seq):
    return pl.BlockSpec((None, d, tm // d, width), lambda i: (i // tiles_per_seq, 0, i % tiles_per_seq, 0))


def _params(n_axes):
    return pltpu.CompilerParams(dimension_semantics=("arbitrary",) * n_axes,
                                vmem_limit_bytes=VMEM_LIMIT_BYTES)


_HBM = pl.BlockSpec(memory_space=pl.ANY)


FILL_SLOTS = 4


def _fill_bf16(w_hbm, w_vmem, stage, sem, rows, cols=None):
    k, n = w_hbm.shape
    cols = n if cols is None else cols
    slots = min(stage.shape[0], sem.shape[0])
    assert k % rows == 0 and rows <= stage.shape[1] and cols <= stage.shape[2]
    blocks = [(r, c, min(cols, n - c)) for r in range(0, k, rows) for c in range(0, n, cols)]

    def copy(i):
        r, c, width = blocks[i]
        dst = stage.at[i % slots, pl.ds(0, rows), pl.ds(0, width)]
        return pltpu.make_async_copy(w_hbm.at[pl.ds(r, rows), pl.ds(c, width)], dst, sem.at[i % slots])

    for i in range(min(slots - 1, len(blocks))):
        copy(i).start()
    for i, (r, c, width) in enumerate(blocks):
        if i + slots - 1 < len(blocks):
            copy(i + slots - 1).start()
        copy(i).wait()
        w_vmem[r:r + rows, c:c + width] = stage[i % slots, 0:rows, 0:width].astype(BF16)


def _weight_scratch(shapes, stage_shape=None):
    stage = [] if stage_shape is None else [pltpu.VMEM(stage_shape, F32)]
    return [pltpu.VMEM(s, BF16) for s in shapes] + stage + [pltpu.SemaphoreType.DMA((FILL_SLOTS,))]


FFN_FILL_ROWS = 256


def _ffn_ln_kernel(x_ref, wg_hbm, wu_hbm, wd_hbm, g_ref, b_ref, h_ref, hb_ref,
                   wg_ref, wu_ref, wd_ref, stage_ref, sem):
    @pl.when(pl.program_id(0) == 0)
    def _():
        for w_hbm, w_ref in ((wg_hbm, wg_ref), (wu_hbm, wu_ref), (wd_hbm, wd_ref)):
            _fill_bf16(w_hbm, w_ref, stage_ref, sem, FFN_FILL_ROWS)

    x = x_ref[...]
    y = _swiglu(x.astype(BF16), wg_ref, wu_ref, wd_ref)
    h = _layer_norm(ALPHA * x + 0.5 * y, g_ref[...], b_ref[...])
    h_ref[...] = h
    hb_ref[...] = h.astype(BF16)


def _ffn_ln(x, wg, wu, wd, g, b, tm):
    t = x.shape[0]
    return pl.pallas_call(
        _ffn_ln_kernel,
        grid=(t // tm,),
        in_specs=[_rows(tm, D_MODEL), _HBM, _HBM, _HBM, _resident((1, D_MODEL)), _resident((1, D_MODEL))],
        out_specs=[_rows(tm, D_MODEL), _rows(tm, D_MODEL)],
        out_shape=[jax.ShapeDtypeStruct((t, D_MODEL), F32), jax.ShapeDtypeStruct((t, D_MODEL), BF16)],
        scratch_shapes=_weight_scratch([(D_MODEL, D_FF), (D_MODEL, D_FF), (D_FF, D_MODEL)],
                                       (FILL_SLOTS, FFN_FILL_ROWS, D_FF)),
        compiler_params=_params(1),
        name="ffn_ln",
    )(x, wg, wu, wd, g, b)


def _project_slabs(hb_ref, w_ref, col0, dst_ref):
    res = _dot(hb_ref[...], w_ref[:, col0:col0 + GROUP_WIDTH])
    for j in range(N_SLAB):
        dst_ref[j] = res[:, j * LANES:(j + 1) * LANES]


def _emit_residue_major(p_ref, out_ref, col0, cos_ref=None, sin_ref=None):
    d, rows, _ = out_ref.shape
    lane = lax.broadcasted_iota(jnp.int32, (1, LANES), 1)
    first_half = (lane % HEAD_DIM) < (HEAD_DIM // 2)
    for r in range(d):
        rs = pl.ds(r, rows, stride=d) if d > 1 else slice(None)
        if cos_ref is not None:
            c, s = cos_ref[rs, :], sin_ref[rs, :]
        blocks = []
        for j in range(N_SLAB):
            x = p_ref[j, rs, :]
            if cos_ref is not None:
                partner = jnp.where(first_half, pltpu.roll(x, LANES - HEAD_DIM // 2, 1),
                                    pltpu.roll(x, HEAD_DIM // 2, 1))
                x = x * c + partner * s
            blocks.append(x)
        out_ref[r, :, col0:col0 + GROUP_WIDTH] = jnp.concatenate(blocks, axis=1).astype(out_ref.dtype)


def _mixer_in_kernel(hb_ref, pos_ref, invf_ref, w_hbm, bg_ref, lng_ref, lnb_ref, ws_ref, bs_ref, wgb_hbm,
                     qkv0_ref, qkv1_ref, qkv2_ref,
                     ga_ref, gm_ref, ag_ref, p_ref, cs_ref, vgn_ref, w_ref, wgb_ref, sem):
    tm = hb_ref.shape[0]
    a_ref, g_ref = ag_ref.at[pl.ds(0, 2)], ag_ref.at[pl.ds(2, 2)]
    qkv_refs = (qkv0_ref, qkv1_ref, qkv2_ref)

    @pl.when(pl.program_id(0) == 0)
    def _():
        _fill_bf16(w_hbm, w_ref, ag_ref, sem, tm, ag_ref.shape[2])
        _fill_bf16(wgb_hbm, wgb_ref, ag_ref, sem, tm)

    def project_group(gi, slot):
        c0 = gi * GROUP_WIDTH
        for i, base in enumerate((_Q0, _K0, _V0)):
            _project_slabs(hb_ref, w_ref, base + c0, p_ref.at[slot + i])

    def emit_group(gi, slot):
        out_ref = qkv_refs[gi]
        _emit_residue_major(p_ref.at[slot], out_ref, 0, cs_ref.at[2], cs_ref.at[3])
        _emit_residue_major(p_ref.at[slot + 1], out_ref, GROUP_WIDTH, cs_ref.at[0], cs_ref.at[1])
        _emit_residue_major(p_ref.at[slot + 2], out_ref, 2 * GROUP_WIDTH)

    hb = hb_ref[...]
    a_ref[0] = _dot(hb, w_ref[:, _U0:_U0 + GMLP_WIDTH])
    a_ref[1] = _dot(hb, w_ref[:, _VG0:_VG0 + GMLP_WIDTH])
    n_freq = HEAD_DIM // 2
    per_row = LANES // n_freq
    ang = pos_ref[...].astype(F32) * invf_ref[...]
    lane = lax.broadcasted_iota(jnp.int32, (1, LANES), 1)
    first_half = (lane % HEAD_DIM) < n_freq
    lane_group = lane // n_freq
    for k, packed in enumerate((jnp.cos(ang), jnp.sin(ang))):
        rolled = [packed] + [pltpu.roll(packed, n_freq * s, 1) for s in range(1, per_row)]
        for a in range(per_row):
            spread = rolled[(0 - a) % per_row]
            for b in range(1, per_row):
                spread = jnp.where(lane_group == b, rolled[(b - a) % per_row], spread)
            if k == 1:
                spread = jnp.where(first_half, -spread, spread)
            rows = pl.ds(a, tm // per_row, stride=per_row)
            cs_ref[k, rows, :] = spread
            cs_ref[k + 2, rows, :] = spread * _Q_SCALE

    hb = hb_ref[...]
    g_ref[0] = _dot(hb, w_ref[:, _GM0:_GM0 + D_MODEL]) + bg_ref[:, D_MODEL:]
    g_ref[1] = _dot(hb, w_ref[:, _GA0:_GA0 + D_MODEL]) + bg_ref[:, :D_MODEL]
    vgn_ref[...] = _layer_norm(_gelu(a_ref[1]), lng_ref[...], lnb_ref[...]).astype(BF16)

    project_group(0, 0)
    n_chunk = tm // GMLP_CHUNK
    gdim = GMLP_WIDTH // GMLP_GROUPS
    row = lax.broadcasted_iota(jnp.int32, (GMLP_CHUNK, GMLP_CHUNK), 0)
    col = lax.broadcasted_iota(jnp.int32, (GMLP_CHUNK, GMLP_CHUNK), 1)
    causal = col <= row
    for g in range(GMLP_GROUPS):
        ws = jnp.where(causal, ws_ref[g], 0.0).astype(BF16)
        rhs = jnp.concatenate(
            [vgn_ref[c * GMLP_CHUNK:(c + 1) * GMLP_CHUNK, g * gdim:(g + 1) * gdim] for c in range(n_chunk)],
            axis=1)
        mixed = _dot(ws, rhs)
        for c in range(n_chunk):
            a_ref[1, c * GMLP_CHUNK:(c + 1) * GMLP_CHUNK, g * gdim:(g + 1) * gdim] = (
                mixed[:, c * gdim:(c + 1) * gdim] + bs_ref[:, g * gdim:(g + 1) * gdim])
    a_ref[0] = _gelu(a_ref[0])

    project_group(1, 3)
    vgn_ref[...] = (a_ref[0] * a_ref[1]).astype(BF16)
    emit_group(0, 0)
    ga_ref[...] = jax.nn.sigmoid(g_ref[1]).astype(BF16)

    project_group(2, 0)
    emit_group(1, 3)
    g_ref[0] = jax.nn.sigmoid(g_ref[0])

    gm_ref[...] = (g_ref[0] * _dot(vgn_ref[...], wgb_ref[...])).astype(BF16)
    emit_group(2, 0)


def _mixer_in(hb, pos, invf, w_in, b_gates, ln_g, ln_b, w_s, b_s, w_gb, batch, tm):
    t = hb.shape[0]
    seq = t // batch
    in_width = w_in.shape[1]
    qkv_specs, qkv_shapes = [], []
    for _, d in ATTN_PATTERNS:
        qkv_specs += [_residue_major_spec(d, tm, 3 * GROUP_WIDTH, seq // tm)]
        qkv_shapes += [jax.ShapeDtypeStruct((batch, d, seq // d, 3 * GROUP_WIDTH), BF16)]
    return pl.pallas_call(
        _mixer_in_kernel,
        grid=(t // tm,),
        in_specs=[_rows(tm, D_MODEL), _rows(tm * (HEAD_DIM // 2) // LANES, LANES), _resident((1, LANES)), _HBM,
                  _resident((1, 2 * D_MODEL)), _resident((1, GMLP_WIDTH)), _resident((1, GMLP_WIDTH)),
                  _resident((GMLP_GROUPS, GMLP_CHUNK, GMLP_CHUNK)), _resident((GMLP_CHUNK, GMLP_WIDTH)),
                  _HBM],
        out_specs=qkv_specs + [_rows(tm, D_MODEL), _rows(tm, D_MODEL)],
        out_shape=qkv_shapes + [jax.ShapeDtypeStruct((t, D_MODEL), BF16)] * 2,
        scratch_shapes=[pltpu.VMEM((FILL_SLOTS, tm, D_MODEL), F32),
                        pltpu.VMEM((6, N_SLAB, tm, LANES), F32), pltpu.VMEM((4, tm, LANES), F32),
                        pltpu.VMEM((tm, GMLP_WIDTH), BF16)]
                       + _weight_scratch([(D_MODEL, in_width), (GMLP_WIDTH, D_MODEL)]),
        compiler_params=_params(1),
        name="mixer_in",
    )(hb, pos, invf, w_in, b_gates, ln_g, ln_b, w_s, b_s, w_gb)


def _attn_kernel(qkv_ref, acc_ref, stat_ref, *, n_blk):
    n_items = qkv_ref.shape[0] // ATTN_BLOCK
    has_prev = n_blk > 1
    row = lax.broadcasted_iota(jnp.int32, (2 * ATTN_BLOCK, ATTN_BLOCK), 0) % ATTN_BLOCK
    col = lax.broadcasted_iota(jnp.int32, (2 * ATTN_BLOCK, ATTN_BLOCK), 1)
    neg_inf = jnp.float32(-jnp.inf)
    bias_cur = jnp.where(col <= row, 0.0, neg_inf)
    bias_prev = jnp.where(col >= row, 0.0, neg_inf)
    lane = lax.broadcasted_iota(jnp.int32, (ATTN_BLOCK, LANES), 1)
    low_head = lane < HEAD_DIM
    n_keys = 2 * ATTN_BLOCK if has_prev else ATTN_BLOCK
    low_head_keys = lax.broadcasted_iota(jnp.int32, (n_keys, LANES), 1) < HEAD_DIM
    contract_last = (((1,), (1,)), ((), ()))

    def item(j, carry):
        r0 = pl.multiple_of(j * ATTN_BLOCK, ATTN_BLOCK)
        if has_prev:
            p0 = pl.multiple_of(jnp.maximum(j - 1, 0) * ATTN_BLOCK, ATTN_BLOCK)
            prev_bias = bias_prev + jnp.where(j % n_blk > 0, 0.0, neg_inf)
        scores = []
        for pair in range(N_SLAB):
            q_sl = slice(pair * LANES, (pair + 1) * LANES)
            k_sl = slice(GROUP_WIDTH + pair * LANES, GROUP_WIDTH + (pair + 1) * LANES)
            q = qkv_ref[pl.ds(r0, ATTN_BLOCK), q_sl]
            zero = jnp.zeros_like(q)
            q2 = jnp.concatenate([jnp.where(low_head, q, zero), jnp.where(low_head, zero, q)], axis=0)
            keys = qkv_ref[pl.ds(r0, ATTN_BLOCK), k_sl]
            if has_prev:
                keys = jnp.concatenate([qkv_ref[pl.ds(p0, ATTN_BLOCK), k_sl], keys], axis=0)
            s = lax.dot_general(q2, keys, contract_last, preferred_element_type=F32)
            if has_prev:
                s = s + jnp.concatenate([prev_bias, bias_cur], axis=1)
            else:
                s = s + bias_cur
            scores.append(s)
        maxes = [jnp.max(s, axis=1, keepdims=True) for s in scores]
        probs = [jnp.exp2(s - m).astype(BF16) for s, m in zip(scores, maxes)]
        stat = jnp.zeros((ATTN_BLOCK, LANES), F32)
        for pair in range(N_SLAB):
            sl = slice(pair * LANES, (pair + 1) * LANES)
            v_sl = slice(2 * GROUP_WIDTH + pair * LANES, 2 * GROUP_WIDTH + (pair + 1) * LANES)
            vals = qkv_ref[pl.ds(r0, ATTN_BLOCK), v_sl]
            if has_prev:
                vals = jnp.concatenate([qkv_ref[pl.ds(p0, ATTN_BLOCK), v_sl], vals], axis=0)
            one = jnp.ones_like(vals)
            r_even = _dot(probs[pair][:ATTN_BLOCK], jnp.where(low_head_keys, vals, one))
            r_odd = _dot(probs[pair][ATTN_BLOCK:], jnp.where(low_head_keys, one, vals))
            acc_ref[pl.ds(r0, ATTN_BLOCK), sl] = jnp.where(low_head, r_even, r_odd).astype(BF16)
            m_even, m_odd = maxes[pair][:ATTN_BLOCK], maxes[pair][ATTN_BLOCK:]
            for h, m_h, r_h in ((2 * pair, m_even, r_even), (2 * pair + 1, m_odd, r_odd)):
                stat = jnp.where(lane == _stat_lane(h), m_h, stat)
                stat = jnp.where(lane == _stat_lane(h) + _DEN_SHIFT, r_h, stat)
        stat_ref[pl.ds(r0, ATTN_BLOCK), :] = stat
        return carry

    lax.fori_loop(0, n_items, item, 0, unroll=16)


def _attention(qkv):
    b, d, l, width = qkv.shape

    def seq(w):
        return pl.BlockSpec((None, d * l, w), lambda bi: (bi, 0, 0))

    acc, stat = pl.pallas_call(
        functools.partial(_attn_kernel, n_blk=l // ATTN_BLOCK),
        grid=(b,),
        in_specs=[seq(width)],
        out_specs=[seq(GROUP_WIDTH), seq(LANES)],
        out_shape=[jax.ShapeDtypeStruct((b, d * l, GROUP_WIDTH), BF16),
                   jax.ShapeDtypeStruct((b, d * l, LANES), F32)],
        compiler_params=_params(1),
        name=f"band_attention_d{d}",
    )(qkv.reshape(b, d * l, width))
    return acc.reshape(b, d, l, GROUP_WIDTH), stat.reshape(b, d, l, LANES)


def _load_token_major(in_ref, scr_ref):
    d, rows, width = in_ref.shape
    if d == 1:
        return in_ref[0].astype(F32)
    n_slab = width // LANES
    for r in range(d):
        x = in_ref[r].astype(F32)
        for j in range(n_slab):
            scr_ref[j, pl.ds(r, rows, stride=d), :] = x[:, j * LANES:(j + 1) * LANES]
    return jnp.concatenate([scr_ref[j] for j in range(n_slab)], axis=1)


def _mixer_out_kernel(a0_ref, a1_ref, a2_ref, s0_ref, s1_ref, s2_ref, ga_ref, gm_ref, h1_ref, expand_ref,
                      wab_hbm, wout_hbm, g2_ref, b2_ref, wg_hbm, wu_hbm, wd_hbm, g3_ref, b3_ref, out_ref,
                      ascr_ref, sscr_ref, wab_ref, wout_ref, wg_ref, wu_ref, wd_ref, stage_ref, sem):
    @pl.when(pl.program_id(0) == 0)
    def _():
        for w_hbm, w_ref in ((wab_hbm, wab_ref), (wout_hbm, wout_ref), (wg_hbm, wg_ref), (wu_hbm, wu_ref),
                             (wd_hbm, wd_ref)):
            _fill_bf16(w_hbm, w_ref, stage_ref, sem, FFN_FILL_ROWS)

    tm = h1_ref.shape[0]
    stats = [_load_token_major(s_ref, sscr_ref.at[i]) for i, s_ref in enumerate((s0_ref, s1_ref, s2_ref))]
    accs = [_load_token_major(a_ref, ascr_ref.at[i]) for i, a_ref in enumerate((a0_ref, a1_ref, a2_ref))]
    lane = lax.broadcasted_iota(jnp.int32, (1, LANES), 1)
    is_max_lane = functools.reduce(jnp.logical_or, [lane == _stat_lane(h) for h in range(HEADS_PER_GROUP)])

    def mix_rows(rows):
        st = [x[rows] for x in stats]
        m = jnp.maximum(jnp.maximum(st[0], st[1]), st[2])
        es = [jnp.exp2(x - m) for x in st]
        dens = [pltpu.roll(x, LANES - _DEN_SHIFT, 1) for x in st]
        den = es[0] * dens[0] + es[1] * dens[1] + es[2] * dens[2]
        inv_den = 1.0 / jnp.where(is_max_lane, den, 1.0)
        y = None
        for e, acc in zip(es, accs):
            w = e * inv_den
            w_hi = w.astype(BF16)
            w_lo = (w - w_hi.astype(F32)).astype(BF16)
            w_wide = _dot(jnp.concatenate([w_hi, w_lo], axis=1), expand_ref[...])
            term = w_wide * acc[rows]
            y = term if y is None else y + term
        branch_a = _dot(y.astype(BF16), wab_ref[...])
        merged = ga_ref[rows, :].astype(F32) * branch_a + gm_ref[rows, :].astype(F32)
        mix = _dot(merged.astype(BF16), wout_ref[...])
        return _layer_norm(ALPHA * h1_ref[rows, :] + mix, g2_ref[...], b2_ref[...])

    h2 = mix_rows(slice(0, tm))
    ffn = _swiglu(h2.astype(BF16), wg_ref, wu_ref, wd_ref)
    out_ref[...] = _layer_norm(ALPHA * h2 + 0.5 * ffn, g3_ref[...], b3_ref[...])


def _mixer_out(accs, stats, ga, gm, h1, expand, wab, wout, g2, b2, wg, wu, wd, g3, b3, tm):
    t = h1.shape[0]
    tiles_per_seq = accs[0].shape[1] * accs[0].shape[2] // tm
    acc_specs = [_residue_major_spec(a.shape[1], tm, GROUP_WIDTH, tiles_per_seq) for a in accs]
    stat_specs = [_residue_major_spec(s.shape[1], tm, LANES, tiles_per_seq) for s in stats]
    return pl.pallas_call(
        _mixer_out_kernel,
        grid=(t // tm,),
        scratch_shapes=[pltpu.VMEM((3, N_SLAB, tm, LANES), F32), pltpu.VMEM((3, 1, tm, LANES), F32)]
                       + _weight_scratch([(GROUP_WIDTH, D_MODEL), (D_MODEL, D_MODEL), (D_MODEL, D_FF),
                                          (D_MODEL, D_FF), (D_FF, D_MODEL)], (FILL_SLOTS - 1, FFN_FILL_ROWS, D_FF)),
        in_specs=acc_specs + stat_specs
                 + [_rows(tm, D_MODEL)] * 3
                 + [_resident((2 * LANES, GROUP_WIDTH)), _HBM, _HBM, _resident((1, D_MODEL)),
                    _resident((1, D_MODEL)), _HBM, _HBM, _HBM, _resident((1, D_MODEL)), _resident((1, D_MODEL))],
        out_specs=_rows(tm, D_MODEL),
        out_shape=jax.ShapeDtypeStruct((t, D_MODEL), F32),
        compiler_params=_params(1),
        name="mixer_out_ffn",
    )(*accs, *stats, ga, gm, h1, expand, wab, wout, g2, b2, wg, wu, wd, g3, b3)


def kernel(x, positions, ffn1_w_gate, ffn1_w_up, ffn1_w_down, ln1_g, ln1_b, w_in, b_gates, gmlp_ln_g, gmlp_ln_b, gmlp_w_s, gmlp_b_s, w_attn_branch, w_gmlp_branch, w_out, ln2_g, ln2_b, ffn2_w_gate, ffn2_w_up, ffn2_w_down, ln3_g, ln3_b):
    b, s, d_model = x.shape
    assert d_model == D_MODEL and ln1_g.shape[0] == DEPTH == 1
    t = b * s
    tm = 512
    assert t % tm == 0 and tm % GMLP_CHUNK == 0 and s % tm == 0

    lane = jnp.arange(LANES) % (HEAD_DIM // 2)
    invf = (ROPE_THETA ** (-(2 * lane).astype(F32) / HEAD_DIM)).reshape(1, LANES)
    stat_lane_of_col = jnp.array([_stat_lane(c // HEAD_DIM) for c in range(GROUP_WIDTH)])
    expand = (jnp.arange(LANES)[:, None] == stat_lane_of_col[None, :]).astype(BF16)
    expand = jnp.concatenate([expand, expand], axis=0)
    b_s = jnp.repeat(gmlp_b_s[0].T, GMLP_WIDTH // GMLP_GROUPS, axis=1)

    h = x.reshape(t, D_MODEL)
    h1, h1b = _ffn_ln(h, ffn1_w_gate[0], ffn1_w_up[0], ffn1_w_down[0], ln1_g, ln1_b, tm)
    pos_packed = jnp.repeat(positions.reshape(t * (HEAD_DIM // 2) // LANES, -1), HEAD_DIM // 2, axis=1)
    outs = _mixer_in(h1b, pos_packed, invf, w_in[0], b_gates, gmlp_ln_g, gmlp_ln_b,
                     gmlp_w_s[0], b_s, w_gmlp_branch[0], b, tm)
    ga, gm = outs[N_ATTN_GROUPS], outs[N_ATTN_GROUPS + 1]
    accs, stats = [], []
    for gi in range(N_ATTN_GROUPS):
        acc, stat = _attention(outs[gi])
        accs.append(acc)
        stats.append(stat)
    out = _mixer_out(accs, stats, ga, gm, h1, expand, w_attn_branch[0], w_out[0],
                     ln2_g, ln2_b, ffn2_w_gate[0], ffn2_w_up[0], ffn2_w_down[0], ln3_g, ln3_b, tm)
    return out.reshape(b, s, D_MODEL)
```

```python
import functools
import math

import jax
import jax.numpy as jnp
from jax import lax
from jax.experimental import pallas as pl
from jax.experimental.pallas import tpu as pltpu

D_MODEL = 1024
HEAD_DIM = 64
HEADS_PER_GROUP = 8
ATTN_PATTERNS = ((128, 1), (512, 4), (2048, 16))
N_ATTN_GROUPS = len(ATTN_PATTERNS)
GROUP_WIDTH = HEADS_PER_GROUP * HEAD_DIM
ATTN_WIDTH = N_ATTN_GROUPS * GROUP_WIDTH
ATTN_BLOCK = 128
ROPE_THETA = 10000.0
GMLP_CHUNK = 128
GMLP_GROUPS = 8
GMLP_WIDTH = D_MODEL
D_FF = 2816
DEPTH = 1
ALPHA = (2 * DEPTH) ** 0.25
LN_EPS = 1e-5

LANES = 128
VMEM_LIMIT_BYTES = 60 * 1024 * 1024
N_SLAB = GROUP_WIDTH // LANES

_Q0, _K0, _V0 = 0, ATTN_WIDTH, 2 * ATTN_WIDTH
_U0 = 3 * ATTN_WIDTH
_VG0 = _U0 + GMLP_WIDTH
_GA0 = _VG0 + GMLP_WIDTH
_GM0 = _GA0 + D_MODEL

_Q_SCALE = HEAD_DIM ** -0.5 * math.log2(math.e)

_DEN_SHIFT = 8


def _stat_lane(h):
    return HEAD_DIM + h if h % 2 == 0 else h


BF16 = jnp.bfloat16
F32 = jnp.float32


def _layer_norm(x, g, b):
    mu = jnp.mean(x, axis=-1, keepdims=True)
    xc = x - mu
    var = jnp.mean(xc * xc, axis=-1, keepdims=True)
    return xc * lax.rsqrt(var + LN_EPS) * g + b


def _dot(a, b):
    return jnp.dot(a, b, preferred_element_type=F32)


def _swiglu(xb, wg_ref, wu_ref, wd_ref):
    g = _dot(xb, wg_ref[...])
    u = _dot(xb, wu_ref[...])
    a = g * jax.nn.sigmoid(g) * u
    return _dot(a.astype(BF16), wd_ref[...])


def _gelu(x):
    return 0.5 * x * (1.0 + lax.erf(x * math.sqrt(0.5)))


def _resident(shape):
    return pl.BlockSpec(shape, lambda *_: (0,) * len(shape), pipeline_mode=pl.Buffered(1))


def _rows(tm, width):
    return pl.BlockSpec((tm, width), lambda i: (i, 0))


def _residue_major_spec(d, tm, width, tiles_per_seq):
    return pl.BlockSpec((None, d, tm // d, width), lambda i: (i // tiles_per_seq, 0, i % tiles_per_seq, 0))


def _params(n_axes):
    return pltpu.CompilerParams(dimension_semantics=("arbitrary",) * n_axes,
                                vmem_limit_bytes=VMEM_LIMIT_BYTES)


_HBM = pl.BlockSpec(memory_space=pl.ANY)


FILL_SLOTS = 4


def _fill_bf16(w_hbm, w_vmem, stage, sem, rows, cols=None):
    k, n = w_hbm.shape
    cols = n if cols is None else cols
    slots = min(stage.shape[0], sem.shape[0])
    assert k % rows == 0 and rows <= stage.shape[1] and cols <= stage.shape[2]
    blocks = [(r, c, min(cols, n - c)) for r in range(0, k, rows) for c in range(0, n, cols)]

    def copy(i):
        r, c, width = blocks[i]
        dst = stage.at[i % slots, pl.ds(0, rows), pl.ds(0, width)]
        return pltpu.make_async_copy(w_hbm.at[pl.ds(r, rows), pl.ds(c, width)], dst, sem.at[i % slots])

    for i in range(min(slots - 1, len(blocks))):
        copy(i).start()
    for i, (r, c, width) in enumerate(blocks):
        if i + slots - 1 < len(blocks):
            copy(i + slots - 1).start()
        copy(i).wait()
        w_vmem[r:r + rows, c:c + width] = stage[i % slots, 0:rows, 0:width].astype(BF16)


def _weight_scratch(shapes, stage_shape=None):
    stage = [] if stage_shape is None else [pltpu.VMEM(stage_shape, F32)]
    return [pltpu.VMEM(s, BF16) for s in shapes] + stage + [pltpu.SemaphoreType.DMA((FILL_SLOTS,))]


BF16_SUBLANES = 16


def _side_cast_spec(shape, n_steps):
    k, n = shape
    n_blocks = n_steps
    while k % n_blocks or (k // n_blocks) % BF16_SUBLANES:
        n_blocks //= 2
    assert n_blocks >= 1 and n_steps % n_blocks == 0
    steps_per_block = n_steps // n_blocks
    return pl.BlockSpec((k // n_blocks, n), lambda i: (i // steps_per_block, 0))


def _side_cast(pairs):
    for src_ref, dst_ref in pairs:
        dst_ref[...] = src_ref[...].astype(BF16)


FFN_FILL_ROWS = 256


def _ffn_ln_kernel(x_ref, wg_hbm, wu_hbm, wd_hbm, g_ref, b_ref, *rest, n_side):
    side_in, rest = rest[:n_side], rest[n_side:]
    h_ref, hb_ref = rest[:2]
    side_out, (wg_ref, wu_ref, wd_ref, stage_ref, sem) = rest[2:2 + n_side], rest[2 + n_side:]

    @pl.when(pl.program_id(0) == 0)
    def _():
        for w_hbm, w_ref in ((wg_hbm, wg_ref), (wu_hbm, wu_ref), (wd_hbm, wd_ref)):
            _fill_bf16(w_hbm, w_ref, stage_ref, sem, FFN_FILL_ROWS)

    x = x_ref[...]
    y = _swiglu(x.astype(BF16), wg_ref, wu_ref, wd_ref)
    h = _layer_norm(ALPHA * x + 0.5 * y, g_ref[...], b_ref[...])
    h_ref[...] = h
    hb_ref[...] = h.astype(BF16)
    _side_cast(zip(side_in, side_out))


def _ffn_ln(x, wg, wu, wd, g, b, next_weights, tm):
    t = x.shape[0]
    n_steps = t // tm
    side_specs = [_side_cast_spec(w.shape, n_steps) for w in next_weights]
    return pl.pallas_call(
        functools.partial(_ffn_ln_kernel, n_side=len(next_weights)),
        grid=(n_steps,),
        in_specs=[_rows(tm, D_MODEL), _HBM, _HBM, _HBM, _resident((1, D_MODEL)), _resident((1, D_MODEL))]
                 + side_specs,
        out_specs=[_rows(tm, D_MODEL), _rows(tm, D_MODEL)] + side_specs,
        out_shape=[jax.ShapeDtypeStruct((t, D_MODEL), F32), jax.ShapeDtypeStruct((t, D_MODEL), BF16)]
                  + [jax.ShapeDtypeStruct(w.shape, BF16) for w in next_weights],
        scratch_shapes=_weight_scratch([(D_MODEL, D_FF), (D_MODEL, D_FF), (D_FF, D_MODEL)],
                                       (FILL_SLOTS, FFN_FILL_ROWS, D_FF)),
        compiler_params=_params(1),
        name="ffn_ln",
    )(x, wg, wu, wd, g, b, *next_weights)


def _project_slabs(hb_ref, w_ref, col0, dst_ref):
    res = _dot(hb_ref[...], w_ref[:, col0:col0 + GROUP_WIDTH])
    for j in range(N_SLAB):
        dst_ref[j] = res[:, j * LANES:(j + 1) * LANES]


def _emit_residue_major(p_ref, out_ref, col0, cos_ref=None, sin_ref=None):
    d, rows, _ = out_ref.shape
    lane = lax.broadcasted_iota(jnp.int32, (1, LANES), 1)
    first_half = (lane % HEAD_DIM) < (HEAD_DIM // 2)
    for r in range(d):
        rs = pl.ds(r, rows, stride=d) if d > 1 else slice(None)
        if cos_ref is not None:
            c, s = cos_ref[rs, :], sin_ref[rs, :]
        blocks = []
        for j in range(N_SLAB):
            x = p_ref[j, rs, :]
            if cos_ref is not None:
                partner = jnp.where(first_half, pltpu.roll(x, LANES - HEAD_DIM // 2, 1),
                                    pltpu.roll(x, HEAD_DIM // 2, 1))
                x = x * c + partner * s
            blocks.append(x)
        out_ref[r, :, col0:col0 + GROUP_WIDTH] = jnp.concatenate(blocks, axis=1).astype(out_ref.dtype)


def _mixer_in_kernel(hb_ref, pos_ref, invf_ref, w_ref, bg_ref, lng_ref, lnb_ref, ws_ref, bs_ref, wgb_ref,
                     *rest, n_side):
    side_in, rest = rest[:n_side], rest[n_side:]
    qkv_refs, (ga_ref, gm_ref) = rest[:N_ATTN_GROUPS], rest[N_ATTN_GROUPS:N_ATTN_GROUPS + 2]
    side_out = rest[N_ATTN_GROUPS + 2:N_ATTN_GROUPS + 2 + n_side]
    a_ref, g_ref, p_ref, cs_ref, vgn_ref = rest[N_ATTN_GROUPS + 2 + n_side:]
    tm = hb_ref.shape[0]
    _side_cast(zip(side_in, side_out))

    def project_group(gi, slot):
        c0 = gi * GROUP_WIDTH
        for i, base in enumerate((_Q0, _K0, _V0)):
            _project_slabs(hb_ref, w_ref, base + c0, p_ref.at[slot + i])

    def emit_group(gi, slot):
        out_ref = qkv_refs[gi]
        _emit_residue_major(p_ref.at[slot], out_ref, 0, cs_ref.at[2], cs_ref.at[3])
        _emit_residue_major(p_ref.at[slot + 1], out_ref, GROUP_WIDTH, cs_ref.at[0], cs_ref.at[1])
        _emit_residue_major(p_ref.at[slot + 2], out_ref, 2 * GROUP_WIDTH)

    hb = hb_ref[...]
    a_ref[0] = _dot(hb, w_ref[:, _U0:_U0 + GMLP_WIDTH])
    a_ref[1] = _dot(hb, w_ref[:, _VG0:_VG0 + GMLP_WIDTH])
    n_freq = HEAD_DIM // 2
    per_row = LANES // n_freq
    ang = pos_ref[...].astype(F32) * invf_ref[...]
    lane = lax.broadcasted_iota(jnp.int32, (1, LANES), 1)
    first_half = (lane % HEAD_DIM) < n_freq
    lane_group = lane // n_freq
    for k, packed in enumerate((jnp.cos(ang), jnp.sin(ang))):
        rolled = [packed] + [pltpu.roll(packed, n_freq * s, 1) for s in range(1, per_row)]
        for a in range(per_row):
            spread = rolled[(0 - a) % per_row]
            for b in range(1, per_row):
                spread = jnp.where(lane_group == b, rolled[(b - a) % per_row], spread)
            if k == 1:
                spread = jnp.where(first_half, -spread, spread)
            rows = pl.ds(a, tm // per_row, stride=per_row)
            cs_ref[k, rows, :] = spread
            cs_ref[k + 2, rows, :] = spread * _Q_SCALE

    hb = hb_ref[...]
    g_ref[0] = _dot(hb, w_ref[:, _GM0:_GM0 + D_MODEL]) + bg_ref[:, D_MODEL:]
    g_ref[1] = _dot(hb, w_ref[:, _GA0:_GA0 + D_MODEL]) + bg_ref[:, :D_MODEL]
    vgn_ref[...] = _layer_norm(_gelu(a_ref[1]), lng_ref[...], lnb_ref[...]).astype(BF16)

    project_group(0, 0)
    n_chunk = tm // GMLP_CHUNK
    gdim = GMLP_WIDTH // GMLP_GROUPS
    row = lax.broadcasted_iota(jnp.int32, (GMLP_CHUNK, GMLP_CHUNK), 0)
    col = lax.broadcasted_iota(jnp.int32, (GMLP_CHUNK, GMLP_CHUNK), 1)
    causal = col <= row
    for g in range(GMLP_GROUPS):
        ws = jnp.where(causal, ws_ref[g], 0.0).astype(BF16)
        rhs = jnp.concatenate(
            [vgn_ref[c * GMLP_CHUNK:(c + 1) * GMLP_CHUNK, g * gdim:(g + 1) * gdim] for c in range(n_chunk)],
            axis=1)
        mixed = _dot(ws, rhs)
        for c in range(n_chunk):
            a_ref[1, c * GMLP_CHUNK:(c + 1) * GMLP_CHUNK, g * gdim:(g + 1) * gdim] = (
                mixed[:, c * gdim:(c + 1) * gdim] + bs_ref[:, g * gdim:(g + 1) * gdim])
    a_ref[0] = _gelu(a_ref[0])

    project_group(1, 3)
    vgn_ref[...] = (a_ref[0] * a_ref[1]).astype(BF16)
    emit_group(0, 0)
    ga_ref[...] = jax.nn.sigmoid(g_ref[1]).astype(BF16)

    project_group(2, 0)
    emit_group(1, 3)
    g_ref[0] = jax.nn.sigmoid(g_ref[0])

    gm_ref[...] = (g_ref[0] * _dot(vgn_ref[...], wgb_ref[...])).astype(BF16)
    emit_group(2, 0)


def _mixer_in(hb, pos, invf, w_in, b_gates, ln_g, ln_b, w_s, b_s, w_gb, next_weights, batch, tm):
    t = hb.shape[0]
    seq = t // batch
    n_steps = t // tm
    in_width = w_in.shape[1]
    qkv_specs, qkv_shapes = [], []
    for _, d in ATTN_PATTERNS:
        qkv_specs += [_residue_major_spec(d, tm, 3 * GROUP_WIDTH, seq // tm)]
        qkv_shapes += [jax.ShapeDtypeStruct((batch, d, seq // d, 3 * GROUP_WIDTH), BF16)]
    side_specs = [_side_cast_spec(w.shape, n_steps) for w in next_weights]
    return pl.pallas_call(
        functools.partial(_mixer_in_kernel, n_side=len(next_weights)),
        grid=(n_steps,),
        in_specs=[_rows(tm, D_MODEL), _rows(tm * (HEAD_DIM // 2) // LANES, LANES), _resident((1, LANES)),
                  _resident((D_MODEL, in_width)),
                  _resident((1, 2 * D_MODEL)), _resident((1, GMLP_WIDTH)), _resident((1, GMLP_WIDTH)),
                  _resident((GMLP_GROUPS, GMLP_CHUNK, GMLP_CHUNK)), _resident((GMLP_CHUNK, GMLP_WIDTH)),
                  _resident((GMLP_WIDTH, D_MODEL))] + side_specs,
        out_specs=qkv_specs + [_rows(tm, D_MODEL), _rows(tm, D_MODEL)] + side_specs,
        out_shape=qkv_shapes + [jax.ShapeDtypeStruct((t, D_MODEL), BF16)] * 2
                  + [jax.ShapeDtypeStruct(w.shape, BF16) for w in next_weights],
        scratch_shapes=[pltpu.VMEM((2, tm, GMLP_WIDTH), F32), pltpu.VMEM((2, tm, D_MODEL), F32),
                        pltpu.VMEM((6, N_SLAB, tm, LANES), F32), pltpu.VMEM((4, tm, LANES), F32),
                        pltpu.VMEM((tm, GMLP_WIDTH), BF16)],
        compiler_params=_params(1),
        name="mixer_in",
    )(hb, pos, invf, w_in, b_gates, ln_g, ln_b, w_s, b_s, w_gb, *next_weights)


def _attn_kernel(qkv_ref, acc_ref, stat_ref, *, n_blk):
    n_items = qkv_ref.shape[0] // ATTN_BLOCK
    has_prev = n_blk > 1
    row = lax.broadcasted_iota(jnp.int32, (2 * ATTN_BLOCK, ATTN_BLOCK), 0) % ATTN_BLOCK
    col = lax.broadcasted_iota(jnp.int32, (2 * ATTN_BLOCK, ATTN_BLOCK), 1)
    neg_inf = jnp.float32(-jnp.inf)
    bias_cur = jnp.where(col <= row, 0.0, neg_inf)
    bias_prev = jnp.where(col >= row, 0.0, neg_inf)
    lane = lax.broadcasted_iota(jnp.int32, (ATTN_BLOCK, LANES), 1)
    low_head = lane < HEAD_DIM
    n_keys = 2 * ATTN_BLOCK if has_prev else ATTN_BLOCK
    low_head_keys = lax.broadcasted_iota(jnp.int32, (n_keys, LANES), 1) < HEAD_DIM
    contract_last = (((1,), (1,)), ((), ()))

    def item(j, carry):
        r0 = pl.multiple_of(j * ATTN_BLOCK, ATTN_BLOCK)
        if has_prev:
            p0 = pl.multiple_of(jnp.maximum(j - 1, 0) * ATTN_BLOCK, ATTN_BLOCK)
            prev_bias = bias_prev + jnp.where(j % n_blk > 0, 0.0, neg_inf)
        scores = []
        for pair in range(N_SLAB):
            q_sl = slice(pair * LANES, (pair + 1) * LANES)
            k_sl = slice(GROUP_WIDTH + pair * LANES, GROUP_WIDTH + (pair + 1) * LANES)
            q = qkv_ref[pl.ds(r0, ATTN_BLOCK), q_sl]
            zero = jnp.zeros_like(q)
            q2 = jnp.concatenate([jnp.where(low_head, q, zero), jnp.where(low_head, zero, q)], axis=0)
            keys = qkv_ref[pl.ds(r0, ATTN_BLOCK), k_sl]
            if has_prev:
                keys = jnp.concatenate([qkv_ref[pl.ds(p0, ATTN_BLOCK), k_sl], keys], axis=0)
            s = lax.dot_general(q2, keys, contract_last, preferred_element_type=F32)
            if has_prev:
                s = s + jnp.concatenate([prev_bias, bias_cur], axis=1)
            else:
                s = s + bias_cur
            scores.append(s)
        maxes = [jnp.max(s, axis=1, keepdims=True) for s in scores]
        probs = [jnp.exp2(s - m).astype(BF16) for s, m in zip(scores, maxes)]
        stat = jnp.zeros((ATTN_BLOCK, LANES), F32)
        for pair in range(N_SLAB):
            sl = slice(pair * LANES, (pair + 1) * LANES)
            v_sl = slice(2 * GROUP_WIDTH + pair * LANES, 2 * GROUP_WIDTH + (pair + 1) * LANES)
            vals = qkv_ref[pl.ds(r0, ATTN_BLOCK), v_sl]
            if has_prev:
                vals = jnp.concatenate([qkv_ref[pl.ds(p0, ATTN_BLOCK), v_sl], vals], axis=0)
            one = jnp.ones_like(vals)
            r_even = _dot(probs[pair][:ATTN_BLOCK], jnp.where(low_head_keys, vals, one))
            r_odd = _dot(probs[pair][ATTN_BLOCK:], jnp.where(low_head_keys, one, vals))
            acc_ref[pl.ds(r0, ATTN_BLOCK), sl] = jnp.where(low_head, r_even, r_odd).astype(BF16)
            m_even, m_odd = maxes[pair][:ATTN_BLOCK], maxes[pair][ATTN_BLOCK:]
            for h, m_h, r_h in ((2 * pair, m_even, r_even), (2 * pair + 1, m_odd, r_odd)):
                stat = jnp.where(lane == _stat_lane(h), m_h, stat)
                stat = jnp.where(lane == _stat_lane(h) + _DEN_SHIFT, r_h, stat)
        stat_ref[pl.ds(r0, ATTN_BLOCK), :] = stat
        return carry

    lax.fori_loop(0, n_items, item, 0, unroll=16)


def _attention(qkv):
    b, d, l, width = qkv.shape

    def seq(w):
        return pl.BlockSpec((None, d * l, w), lambda bi: (bi, 0, 0))

    acc, stat = pl.pallas_call(
        functools.partial(_attn_kernel, n_blk=l // ATTN_BLOCK),
        grid=(b,),
        in_specs=[seq(width)],
        out_specs=[seq(GROUP_WIDTH), seq(LANES)],
        out_shape=[jax.ShapeDtypeStruct((b, d * l, GROUP_WIDTH), BF16),
                   jax.ShapeDtypeStruct((b, d * l, LANES), F32)],
        compiler_params=_params(1),
        name=f"band_attention_d{d}",
    )(qkv.reshape(b, d * l, width))
    return acc.reshape(b, d, l, GROUP_WIDTH), stat.reshape(b, d, l, LANES)


def _load_token_major(in_ref, scr_ref):
    d, rows, width = in_ref.shape
    if d == 1:
        return in_ref[0].astype(F32)
    n_slab = width // LANES
    for r in range(d):
        x = in_ref[r].astype(F32)
        for j in range(n_slab):
            scr_ref[j, pl.ds(r, rows, stride=d), :] = x[:, j * LANES:(j + 1) * LANES]
    return jnp.concatenate([scr_ref[j] for j in range(n_slab)], axis=1)


def _mixer_out_kernel(a0_ref, a1_ref, a2_ref, s0_ref, s1_ref, s2_ref, ga_ref, gm_ref, h1_ref, expand_ref,
                      wab_ref, wout_ref, g2_ref, b2_ref, wg_ref, wu_ref, wd_ref, g3_ref, b3_ref, out_ref,
                      ascr_ref, sscr_ref):
    tm = h1_ref.shape[0]
    stats = [_load_token_major(s_ref, sscr_ref.at[i]) for i, s_ref in enumerate((s0_ref, s1_ref, s2_ref))]
    accs = [_load_token_major(a_ref, ascr_ref.at[i]) for i, a_ref in enumerate((a0_ref, a1_ref, a2_ref))]
    lane = lax.broadcasted_iota(jnp.int32, (1, LANES), 1)
    is_max_lane = functools.reduce(jnp.logical_or, [lane == _stat_lane(h) for h in range(HEADS_PER_GROUP)])

    def mix_rows(rows):
        st = [x[rows] for x in stats]
        m = jnp.maximum(jnp.maximum(st[0], st[1]), st[2])
        es = [jnp.exp2(x - m) for x in st]
        dens = [pltpu.roll(x, LANES - _DEN_SHIFT, 1) for x in st]
        den = es[0] * dens[0] + es[1] * dens[1] + es[2] * dens[2]
        inv_den = 1.0 / jnp.where(is_max_lane, den, 1.0)
        y = None
        for e, acc in zip(es, accs):
            w = e * inv_den
            w_hi = w.astype(BF16)
            w_lo = (w - w_hi.astype(F32)).astype(BF16)
            w_wide = _dot(jnp.concatenate([w_hi, w_lo], axis=1), expand_ref[...])
            term = w_wide * acc[rows]
            y = term if y is None else y + term
        branch_a = _dot(y.astype(BF16), wab_ref[...])
        merged = ga_ref[rows, :].astype(F32) * branch_a + gm_ref[rows, :].astype(F32)
        mix = _dot(merged.astype(BF16), wout_ref[...])
        return _layer_norm(ALPHA * h1_ref[rows, :] + mix, g2_ref[...], b2_ref[...])

    h2 = mix_rows(slice(0, tm))
    ffn = _swiglu(h2.astype(BF16), wg_ref, wu_ref, wd_ref)
    out_ref[...] = _layer_norm(ALPHA * h2 + 0.5 * ffn, g3_ref[...], b3_ref[...])


def _mixer_out(accs, stats, ga, gm, h1, expand, wab, wout, g2, b2, wg, wu, wd, g3, b3, tm):
    t = h1.shape[0]
    tiles_per_seq = accs[0].shape[1] * accs[0].shape[2] // tm
    acc_specs = [_residue_major_spec(a.shape[1], tm, GROUP_WIDTH, tiles_per_seq) for a in accs]
    stat_specs = [_residue_major_spec(s.shape[1], tm, LANES, tiles_per_seq) for s in stats]
    return pl.pallas_call(
        _mixer_out_kernel,
        grid=(t // tm,),
        scratch_shapes=[pltpu.VMEM((3, N_SLAB, tm, LANES), F32), pltpu.VMEM((3, 1, tm, LANES), F32)],
        in_specs=acc_specs + stat_specs
                 + [_rows(tm, D_MODEL)] * 3
                 + [_resident((2 * LANES, GROUP_WIDTH)), _resident((GROUP_WIDTH, D_MODEL)),
                    _resident((D_MODEL, D_MODEL)), _resident((1, D_MODEL)), _resident((1, D_MODEL)),
                    _resident((D_MODEL, D_FF)), _resident((D_MODEL, D_FF)), _resident((D_FF, D_MODEL)),
                    _resident((1, D_MODEL)), _resident((1, D_MODEL))],
        out_specs=_rows(tm, D_MODEL),
        out_shape=jax.ShapeDtypeStruct((t, D_MODEL), F32),
        compiler_params=_params(1),
        name="mixer_out_ffn",
    )(*accs, *stats, ga, gm, h1, expand, wab, wout, g2, b2, wg, wu, wd, g3, b3)


def kernel(x, positions, ffn1_w_gate, ffn1_w_up, ffn1_w_down, ln1_g, ln1_b, w_in, b_gates, gmlp_ln_g, gmlp_ln_b, gmlp_w_s, gmlp_b_s, w_attn_branch, w_gmlp_branch, w_out, ln2_g, ln2_b, ffn2_w_gate, ffn2_w_up, ffn2_w_down, ln3_g, ln3_b):
    b, s, d_model = x.shape
    assert d_model == D_MODEL and ln1_g.shape[0] == DEPTH == 1
    t = b * s
    tm = 512
    assert t % tm == 0 and tm % GMLP_CHUNK == 0 and s % tm == 0

    lane = jnp.arange(LANES) % (HEAD_DIM // 2)
    invf = (ROPE_THETA ** (-(2 * lane).astype(F32) / HEAD_DIM)).reshape(1, LANES)
    stat_lane_of_col = jnp.array([_stat_lane(c // HEAD_DIM) for c in range(GROUP_WIDTH)])
    expand = (jnp.arange(LANES)[:, None] == stat_lane_of_col[None, :]).astype(BF16)
    expand = jnp.concatenate([expand, expand], axis=0)
    b_s = jnp.repeat(gmlp_b_s[0].T, GMLP_WIDTH // GMLP_GROUPS, axis=1)

    h = x.reshape(t, D_MODEL)
    h1, h1b, w_in_b, w_gb_b = _ffn_ln(h, ffn1_w_gate[0], ffn1_w_up[0], ffn1_w_down[0], ln1_g, ln1_b,
                                      (w_in[0], w_gmlp_branch[0]), tm)
    pos_packed = jnp.repeat(positions.reshape(t * (HEAD_DIM // 2) // LANES, -1), HEAD_DIM // 2, axis=1)
    outs = _mixer_in(h1b, pos_packed, invf, w_in_b, b_gates, gmlp_ln_g, gmlp_ln_b, gmlp_w_s[0], b_s, w_gb_b,
                     (w_attn_branch[0], w_out[0], ffn2_w_gate[0], ffn2_w_up[0], ffn2_w_down[0]), b, tm)
    ga, gm = outs[N_ATTN_GROUPS], outs[N_ATTN_GROUPS + 1]
    out_weights = outs[N_ATTN_GROUPS + 2:]
    accs, stats = [], []
    for gi in range(N_ATTN_GROUPS):
        acc, stat = _attention(outs[gi])
        accs.append(acc)
        stats.append(stat)
    wab, wout, wg2, wu2, wd2 = out_weights
    out = _mixer_out(accs, stats, ga, gm, h1, expand, wab, wout, ln2_g, ln2_b, wg2, wu2, wd2, ln3_g, ln3_b, tm)
    return out.reshape(b, s, D_MODEL)
```

```python
import functools
import math

import jax
import jax.numpy as jnp
from jax import lax
from jax.experimental import pallas as pl
from jax.experimental.pallas import tpu as pltpu

D_MODEL = 1024
HEAD_DIM = 64
HEADS_PER_GROUP = 8
ATTN_PATTERNS = ((128, 1), (512, 4), (2048, 16))
N_ATTN_GROUPS = len(ATTN_PATTERNS)
GROUP_WIDTH = HEADS_PER_GROUP * HEAD_DIM
ATTN_WIDTH = N_ATTN_GROUPS * GROUP_WIDTH
ATTN_BLOCK = 128
ROPE_THETA = 10000.0
GMLP_CHUNK = 128
GMLP_GROUPS = 8
GMLP_WIDTH = D_MODEL
D_FF = 2816
DEPTH = 1
ALPHA = (2 * DEPTH) ** 0.25
LN_EPS = 1e-5

LANES = 128
VMEM_LIMIT_BYTES = 60 * 1024 * 1024
N_SLAB = GROUP_WIDTH // LANES

_Q0, _K0, _V0 = 0, ATTN_WIDTH, 2 * ATTN_WIDTH
_U0 = 3 * ATTN_WIDTH
_VG0 = _U0 + GMLP_WIDTH
_GA0 = _VG0 + GMLP_WIDTH
_GM0 = _GA0 + D_MODEL

_Q_SCALE = HEAD_DIM ** -0.5 * math.log2(math.e)

_DEN_SHIFT = 8


def _stat_lane(h):
    return HEAD_DIM + h if h % 2 == 0 else h


BF16 = jnp.bfloat16
F32 = jnp.float32


def _layer_norm(x, g, b):
    mu = jnp.mean(x, axis=-1, keepdims=True)
    xc = x - mu
    var = jnp.mean(xc * xc, axis=-1, keepdims=True)
    return xc * lax.rsqrt(var + LN_EPS) * g + b


def _dot(a, b):
    return jnp.dot(a, b, preferred_element_type=F32)


def _swiglu(xb, wg_ref, wu_ref, wd_ref):
    g = _dot(xb, wg_ref[...])
    u = _dot(xb, wu_ref[...])
    a = g * jax.nn.sigmoid(g) * u
    return _dot(a.astype(BF16), wd_ref[...])


def _gelu(x):
    return 0.5 * x * (1.0 + lax.erf(x * math.sqrt(0.5)))


def _resident(shape):
    return pl.BlockSpec(shape, lambda *_: (0,) * len(shape), pipeline_mode=pl.Buffered(1))


def _rows(tm, width):
    return pl.BlockSpec((tm, width), lambda i: (i, 0))


def _residue_major_spec(d, tm, width, tiles_per_seq):
    return pl.BlockSpec((None, d, tm // d, width), lambda i: (i // tiles_per_seq, 0, i % tiles_per_seq, 0))


def _params(n_axes):
    return pltpu.CompilerParams(dimension_semantics=("arbitrary",) * n_axes,
                                vmem_limit_bytes=VMEM_LIMIT_BYTES)


_HBM = pl.BlockSpec(memory_space=pl.ANY)


FILL_SLOTS = 4


def _fill_bf16(w_hbm, w_vmem, stage, sem, rows, cols=None):
    k, n = w_hbm.shape
    cols = n if cols is None else cols
    slots = min(stage.shape[0], sem.shape[0])
    assert k % rows == 0 and rows <= stage.shape[1] and cols <= stage.shape[2]
    blocks = [(r, c, min(cols, n - c)) for r in range(0, k, rows) for c in range(0, n, cols)]

    def copy(i):
        r, c, width = blocks[i]
        dst = stage.at[i % slots, pl.ds(0, rows), pl.ds(0, width)]
        return pltpu.make_async_copy(w_hbm.at[pl.ds(r, rows), pl.ds(c, width)], dst, sem.at[i % slots])

    for i in range(min(slots - 1, len(blocks))):
        copy(i).start()
    for i, (r, c, width) in enumerate(blocks):
        if i + slots - 1 < len(blocks):
            copy(i + slots - 1).start()
        copy(i).wait()
        w_vmem[r:r + rows, c:c + width] = stage[i % slots, 0:rows, 0:width].astype(BF16)


def _weight_scratch(shapes, stage_shape=None):
    stage = [] if stage_shape is None else [pltpu.VMEM(stage_shape, F32)]
    return [pltpu.VMEM(s, BF16) for s in shapes] + stage + [pltpu.SemaphoreType.DMA((FILL_SLOTS,))]


BF16_SUBLANES = 16


def _side_cast_spec(shape, n_steps):
    k, n = shape
    n_blocks = n_steps
    while k % n_blocks or (k // n_blocks) % BF16_SUBLANES:
        n_blocks //= 2
    assert n_blocks >= 1 and n_steps % n_blocks == 0
    steps_per_block = n_steps // n_blocks
    return pl.BlockSpec((k // n_blocks, n), lambda i: (i // steps_per_block, 0))


def _side_cast(pairs, col_chunks=()):
    for idx, (src_ref, dst_ref) in enumerate(pairs):
        chunks = col_chunks[idx] if idx < len(col_chunks) and col_chunks[idx] else [(0, src_ref.shape[1])]
        dst0 = 0
        for src0, width in chunks:
            dst_ref[:, dst0:dst0 + width] = src_ref[:, src0:src0 + width].astype(BF16)
            dst0 += width


FFN_FILL_ROWS = 256


def _ffn_ln_kernel(x_ref, wg_hbm, wu_hbm, wd_hbm, g_ref, b_ref, *rest, n_side, side_cols):
    side_in, rest = rest[:n_side], rest[n_side:]
    h_ref, hb_ref = rest[:2]
    side_out, (wg_ref, wu_ref, wd_ref, stage_ref, sem) = rest[2:2 + n_side], rest[2 + n_side:]

    @pl.when(pl.program_id(0) == 0)
    def _():
        for w_hbm, w_ref in ((wg_hbm, wg_ref), (wu_hbm, wu_ref), (wd_hbm, wd_ref)):
            _fill_bf16(w_hbm, w_ref, stage_ref, sem, FFN_FILL_ROWS)

    x = x_ref[...]
    y = _swiglu(x.astype(BF16), wg_ref, wu_ref, wd_ref)
    h = _layer_norm(ALPHA * x + 0.5 * y, g_ref[...], b_ref[...])
    h_ref[...] = h
    hb_ref[...] = pltpu.bitcast(h.astype(BF16), jnp.uint32)
    _side_cast(zip(side_in, side_out), side_cols)


def _ffn_ln(x, wg, wu, wd, g, b, next_weights, side_cols, tm):
    t = x.shape[0]
    n_steps = t // tm
    side_specs = [_side_cast_spec(w.shape, n_steps) for w in next_weights]
    return pl.pallas_call(
        functools.partial(_ffn_ln_kernel, n_side=len(next_weights), side_cols=side_cols),
        grid=(n_steps,),
        in_specs=[_rows(tm, D_MODEL), _HBM, _HBM, _HBM, _resident((1, D_MODEL)), _resident((1, D_MODEL))]
                 + side_specs,
        out_specs=[_rows(tm, D_MODEL), _rows(tm // 2, D_MODEL)] + side_specs,
        out_shape=[jax.ShapeDtypeStruct((t, D_MODEL), F32), jax.ShapeDtypeStruct((t // 2, D_MODEL), jnp.uint32)]
                  + [jax.ShapeDtypeStruct(w.shape, BF16) for w in next_weights],
        scratch_shapes=_weight_scratch([(D_MODEL, D_FF), (D_MODEL, D_FF), (D_FF, D_MODEL)],
                                       (FILL_SLOTS, FFN_FILL_ROWS, D_FF)),
        compiler_params=_params(1),
        name="ffn_ln",
    )(x, wg, wu, wd, g, b, *next_weights)


def _project_group(hb_ref, w_ref, gi, dst_ref, slot):
    width = 3 * GROUP_WIDTH
    res = _dot(pltpu.bitcast(hb_ref[...], BF16), w_ref[:, gi * width:(gi + 1) * width])
    for i in range(3):
        for j in range(N_SLAB):
            c0 = i * GROUP_WIDTH + j * LANES
            dst_ref[slot + i, j] = res[:, c0:c0 + LANES]


def _emit_residue_major(p_ref, out_ref, col0, cos_ref=None, sin_ref=None):
    d, rows, _ = out_ref.shape
    lane = lax.broadcasted_iota(jnp.int32, (1, LANES), 1)
    first_half = (lane % HEAD_DIM) < (HEAD_DIM // 2)
    for r in range(d):
        rs = pl.ds(r, rows, stride=d) if d > 1 else slice(None)
        if cos_ref is not None:
            c, s = cos_ref[rs, :], sin_ref[rs, :]
        blocks = []
        for j in range(N_SLAB):
            x = p_ref[j, rs, :]
            if cos_ref is not None:
                partner = jnp.where(first_half, pltpu.roll(x, LANES - HEAD_DIM // 2, 1),
                                    pltpu.roll(x, HEAD_DIM // 2, 1))
                x = x * c + partner * s
            blocks.append(x)
        out_ref[r, :, col0:col0 + GROUP_WIDTH] = jnp.concatenate(blocks, axis=1).astype(out_ref.dtype)


def _mixer_in_kernel(hb_ref, pos_ref, invf_ref, w_ref, bg_ref, lng_ref, lnb_ref, ws_ref, bs_ref, wgb_ref,
                     *rest, n_side):
    side_in, rest = rest[:n_side], rest[n_side:]
    qkv_refs, (ga_ref, gm_ref) = rest[:N_ATTN_GROUPS], rest[N_ATTN_GROUPS:N_ATTN_GROUPS + 2]
    side_out = rest[N_ATTN_GROUPS + 2:N_ATTN_GROUPS + 2 + n_side]
    a_ref, g_ref, p_ref, cs_ref, vgn_ref = rest[N_ATTN_GROUPS + 2 + n_side:]
    tm = 2 * hb_ref.shape[0]
    _side_cast(zip(side_in, side_out))

    def project_group(gi, slot):
        _project_group(hb_ref, w_ref, gi, p_ref, slot)

    def emit_group(gi, slot):
        out_ref = qkv_refs[gi]
        _emit_residue_major(p_ref.at[slot], out_ref, 0, cs_ref.at[2], cs_ref.at[3])
        _emit_residue_major(p_ref.at[slot + 1], out_ref, GROUP_WIDTH, cs_ref.at[0], cs_ref.at[1])
        _emit_residue_major(p_ref.at[slot + 2], out_ref, 2 * GROUP_WIDTH)

    uv = _dot(pltpu.bitcast(hb_ref[...], BF16), w_ref[:, _U0:_U0 + 2 * GMLP_WIDTH])
    a_ref[0] = uv[:, :GMLP_WIDTH]
    a_ref[1] = uv[:, GMLP_WIDTH:]
    n_freq = HEAD_DIM // 2
    per_row = LANES // n_freq
    ang = pos_ref[...].astype(F32) * invf_ref[...]
    lane = lax.broadcasted_iota(jnp.int32, (1, LANES), 1)
    first_half = (lane % HEAD_DIM) < n_freq
    lane_group = lane // n_freq
    for k, packed in enumerate((jnp.cos(ang), jnp.sin(ang))):
        rolled = [packed] + [pltpu.roll(packed, n_freq * s, 1) for s in range(1, per_row)]
        for a in range(per_row):
            spread = rolled[(0 - a) % per_row]
            for b in range(1, per_row):
                spread = jnp.where(lane_group == b, rolled[(b - a) % per_row], spread)
            if k == 1:
                spread = jnp.where(first_half, -spread, spread)
            rows = pl.ds(a, tm // per_row, stride=per_row)
            cs_ref[k, rows, :] = spread
            cs_ref[k + 2, rows, :] = spread * _Q_SCALE

    gates = _dot(pltpu.bitcast(hb_ref[...], BF16), w_ref[:, _GA0:_GA0 + 2 * D_MODEL]) + bg_ref[...]
    g_ref[1] = gates[:, :D_MODEL]
    g_ref[0] = gates[:, D_MODEL:]
    vgn_ref[...] = _layer_norm(_gelu(a_ref[1]), lng_ref[...], lnb_ref[...]).astype(BF16)

    project_group(0, 0)
    n_chunk = tm // GMLP_CHUNK
    gdim = GMLP_WIDTH // GMLP_GROUPS
    row = lax.broadcasted_iota(jnp.int32, (GMLP_CHUNK, GMLP_CHUNK), 0)
    col = lax.broadcasted_iota(jnp.int32, (GMLP_CHUNK, GMLP_CHUNK), 1)
    causal = col <= row
    for g in range(GMLP_GROUPS):
        ws = jnp.where(causal, ws_ref[g], 0.0).astype(BF16)
        rhs = jnp.concatenate(
            [vgn_ref[c * GMLP_CHUNK:(c + 1) * GMLP_CHUNK, g * gdim:(g + 1) * gdim] for c in range(n_chunk)],
            axis=1)
        mixed = _dot(ws, rhs)
        for c in range(n_chunk):
            a_ref[1, c * GMLP_CHUNK:(c + 1) * GMLP_CHUNK, g * gdim:(g + 1) * gdim] = (
                mixed[:, c * gdim:(c + 1) * gdim] + bs_ref[:, g * gdim:(g + 1) * gdim])
    a_ref[0] = _gelu(a_ref[0])

    project_group(1, 3)
    vgn_ref[...] = (a_ref[0] * a_ref[1]).astype(BF16)
    emit_group(0, 0)
    ga_ref[...] = jax.nn.sigmoid(g_ref[1]).astype(BF16)

    project_group(2, 0)
    emit_group(1, 3)
    g_ref[0] = jax.nn.sigmoid(g_ref[0])

    gm_ref[...] = (g_ref[0] * _dot(vgn_ref[...], wgb_ref[...])).astype(BF16)
    emit_group(2, 0)


def _mixer_in(hb, pos, invf, w_in, b_gates, ln_g, ln_b, w_s, b_s, w_gb, next_weights, batch, tm):
    t = 2 * hb.shape[0]
    seq = t // batch
    n_steps = t // tm
    in_width = w_in.shape[1]
    qkv_specs, qkv_shapes = [], []
    for _, d in ATTN_PATTERNS:
        qkv_specs += [_residue_major_spec(d, tm, 3 * GROUP_WIDTH, seq // tm)]
        qkv_shapes += [jax.ShapeDtypeStruct((batch, d, seq // d, 3 * GROUP_WIDTH), BF16)]
    side_specs = [_side_cast_spec(w.shape, n_steps) for w in next_weights]
    return pl.pallas_call(
        functools.partial(_mixer_in_kernel, n_side=len(next_weights)),
        grid=(n_steps,),
        in_specs=[_rows(tm // 2, D_MODEL), _rows(tm * (HEAD_DIM // 2) // LANES, LANES), _resident((1, LANES)),
                  _resident((D_MODEL, in_width)),
                  _resident((1, 2 * D_MODEL)), _resident((1, GMLP_WIDTH)), _resident((1, GMLP_WIDTH)),
                  _resident((GMLP_GROUPS, GMLP_CHUNK, GMLP_CHUNK)), _resident((GMLP_CHUNK, GMLP_WIDTH)),
                  _resident((GMLP_WIDTH, D_MODEL))] + side_specs,
        out_specs=qkv_specs + [_rows(tm, D_MODEL), _rows(tm, D_MODEL)] + side_specs,
        out_shape=qkv_shapes + [jax.ShapeDtypeStruct((t, D_MODEL), BF16)] * 2
                  + [jax.ShapeDtypeStruct(w.shape, BF16) for w in next_weights],
        scratch_shapes=[pltpu.VMEM((2, tm, GMLP_WIDTH), F32), pltpu.VMEM((2, tm, D_MODEL), F32),
                        pltpu.VMEM((6, N_SLAB, tm, LANES), F32), pltpu.VMEM((4, tm, LANES), F32),
                        pltpu.VMEM((tm, GMLP_WIDTH), BF16)],
        compiler_params=_params(1),
        name="mixer_in",
    )(hb, pos, invf, w_in, b_gates, ln_g, ln_b, w_s, b_s, w_gb, *next_weights)


def _attn_kernel(qkv_ref, acc_ref, stat_ref, *, n_blk):
    n_items = qkv_ref.shape[0] // ATTN_BLOCK
    has_prev = n_blk > 1
    row = lax.broadcasted_iota(jnp.int32, (2 * ATTN_BLOCK, ATTN_BLOCK), 0) % ATTN_BLOCK
    col = lax.broadcasted_iota(jnp.int32, (2 * ATTN_BLOCK, ATTN_BLOCK), 1)
    neg_inf = jnp.float32(-jnp.inf)
    bias_cur = jnp.where(col <= row, 0.0, neg_inf)
    bias_prev = jnp.where(col >= row, 0.0, neg_inf)
    lane = lax.broadcasted_iota(jnp.int32, (ATTN_BLOCK, LANES), 1)
    low_head = lane < HEAD_DIM
    n_keys = 2 * ATTN_BLOCK if has_prev else ATTN_BLOCK
    low_head_keys = lax.broadcasted_iota(jnp.int32, (n_keys, LANES), 1) < HEAD_DIM
    contract_last = (((1,), (1,)), ((), ()))

    def item(j, carry):
        r0 = pl.multiple_of(j * ATTN_BLOCK, ATTN_BLOCK)
        if has_prev:
            p0 = pl.multiple_of(jnp.maximum(j - 1, 0) * ATTN_BLOCK, ATTN_BLOCK)
            prev_bias = bias_prev + jnp.where(j % n_blk > 0, 0.0, neg_inf)
        scores = []
        for pair in range(N_SLAB):
            q_sl = slice(pair * LANES, (pair + 1) * LANES)
            k_sl = slice(GROUP_WIDTH + pair * LANES, GROUP_WIDTH + (pair + 1) * LANES)
            q = qkv_ref[pl.ds(r0, ATTN_BLOCK), q_sl]
            zero = jnp.zeros_like(q)
            q2 = jnp.concatenate([jnp.where(low_head, q, zero), jnp.where(low_head, zero, q)], axis=0)
            keys = qkv_ref[pl.ds(r0, ATTN_BLOCK), k_sl]
            if has_prev:
                keys = jnp.concatenate([qkv_ref[pl.ds(p0, ATTN_BLOCK), k_sl], keys], axis=0)
            s = lax.dot_general(q2, keys, contract_last, preferred_element_type=F32)
            if has_prev:
                s = s + jnp.concatenate([prev_bias, bias_cur], axis=1)
            else:
                s = s + bias_cur
            scores.append(s)
        maxes = [jnp.max(s, axis=1, keepdims=True) for s in scores]
        probs = [jnp.exp2(s - m).astype(BF16) for s, m in zip(scores, maxes)]
        stat = jnp.zeros((ATTN_BLOCK, LANES), F32)
        for pair in range(N_SLAB):
            sl = slice(pair * LANES, (pair + 1) * LANES)
            v_sl = slice(2 * GROUP_WIDTH + pair * LANES, 2 * GROUP_WIDTH + (pair + 1) * LANES)
            vals = qkv_ref[pl.ds(r0, ATTN_BLOCK), v_sl]
            if has_prev:
                vals = jnp.concatenate([qkv_ref[pl.ds(p0, ATTN_BLOCK), v_sl], vals], axis=0)
            one = jnp.ones_like(vals)
            r_even = _dot(probs[pair][:ATTN_BLOCK], jnp.where(low_head_keys, vals, one))
            r_odd = _dot(probs[pair][ATTN_BLOCK:], jnp.where(low_head_keys, one, vals))
            acc_ref[pl.ds(r0, ATTN_BLOCK), sl] = jnp.where(low_head, r_even, r_odd).astype(BF16)
            m_even, m_odd = maxes[pair][:ATTN_BLOCK], maxes[pair][ATTN_BLOCK:]
            for h, m_h, r_h in ((2 * pair, m_even, r_even), (2 * pair + 1, m_odd, r_odd)):
                stat = jnp.where(lane == _stat_lane(h), m_h, stat)
                stat = jnp.where(lane == _stat_lane(h) + _DEN_SHIFT, r_h, stat)
        stat_ref[pl.ds(r0, ATTN_BLOCK), :] = stat
        return carry

    lax.fori_loop(0, n_items, item, 0, unroll=n_items)


def _attention(qkv):
    b, d, l, width = qkv.shape

    def seq(w):
        return pl.BlockSpec((None, d * l, w), lambda bi: (bi, 0, 0))

    acc, stat = pl.pallas_call(
        functools.partial(_attn_kernel, n_blk=l // ATTN_BLOCK),
        grid=(b,),
        in_specs=[seq(width)],
        out_specs=[seq(GROUP_WIDTH), seq(LANES)],
        out_shape=[jax.ShapeDtypeStruct((b, d * l, GROUP_WIDTH), BF16),
                   jax.ShapeDtypeStruct((b, d * l, LANES), F32)],
        compiler_params=_params(1),
        name=f"band_attention_d{d}",
    )(qkv.reshape(b, d * l, width))
    return acc.reshape(b, d, l, GROUP_WIDTH), stat.reshape(b, d, l, LANES)


def _load_token_major(in_ref, scr_ref):
    d, rows, width = in_ref.shape
    if d == 1:
        return in_ref[0].astype(F32)
    n_slab = width // LANES
    for r in range(d):
        x = in_ref[r].astype(F32)
        for j in range(n_slab):
            scr_ref[j, pl.ds(r, rows, stride=d), :] = x[:, j * LANES:(j + 1) * LANES]
    return jnp.concatenate([scr_ref[j] for j in range(n_slab)], axis=1)


def _mixer_out_kernel(a0_ref, a1_ref, a2_ref, s0_ref, s1_ref, s2_ref, ga_ref, gm_ref, h1_ref, expand_ref,
                      wab_ref, wout_ref, g2_ref, b2_ref, wg_ref, wu_ref, wd_ref, g3_ref, b3_ref, out_ref,
                      ascr_ref, sscr_ref):
    tm = h1_ref.shape[0]
    stats = [_load_token_major(s_ref, sscr_ref.at[i]) for i, s_ref in enumerate((s0_ref, s1_ref, s2_ref))]
    accs = [_load_token_major(a_ref, ascr_ref.at[i]) for i, a_ref in enumerate((a0_ref, a1_ref, a2_ref))]
    lane = lax.broadcasted_iota(jnp.int32, (1, LANES), 1)
    is_max_lane = functools.reduce(jnp.logical_or, [lane == _stat_lane(h) for h in range(HEADS_PER_GROUP)])

    def mix_rows(rows):
        st = [x[rows] for x in stats]
        m = jnp.maximum(jnp.maximum(st[0], st[1]), st[2])
        es = [jnp.exp2(x - m) for x in st]
        dens = [pltpu.roll(x, LANES - _DEN_SHIFT, 1) for x in st]
        den = es[0] * dens[0] + es[1] * dens[1] + es[2] * dens[2]
        inv_den = 1.0 / jnp.where(is_max_lane, den, 1.0)
        y = None
        for e, acc in zip(es, accs):
            w = e * inv_den
            w_hi = w.astype(BF16)
            w_lo = (w - w_hi.astype(F32)).astype(BF16)
            w_wide = _dot(jnp.concatenate([w_hi, w_lo], axis=1), expand_ref[...])
            term = w_wide * acc[rows]
            y = term if y is None else y + term
        branch_a = _dot(y.astype(BF16), wab_ref[...])
        merged = ga_ref[rows, :].astype(F32) * branch_a + gm_ref[rows, :].astype(F32)
        mix = _dot(merged.astype(BF16), wout_ref[...])
        return _layer_norm(ALPHA * h1_ref[rows, :] + mix, g2_ref[...], b2_ref[...])

    h2 = mix_rows(slice(0, tm))
    ffn = _swiglu(h2.astype(BF16), wg_ref, wu_ref, wd_ref)
    out_ref[...] = _layer_norm(ALPHA * h2 + 0.5 * ffn, g3_ref[...], b3_ref[...])


def _mixer_out(accs, stats, ga, gm, h1, expand, wab, wout, g2, b2, wg, wu, wd, g3, b3, tm):
    t = h1.shape[0]
    tiles_per_seq = accs[0].shape[1] * accs[0].shape[2] // tm
    acc_specs = [_residue_major_spec(a.shape[1], tm, GROUP_WIDTH, tiles_per_seq) for a in accs]
    stat_specs = [_residue_major_spec(s.shape[1], tm, LANES, tiles_per_seq) for s in stats]
    return pl.pallas_call(
        _mixer_out_kernel,
        grid=(t // tm,),
        scratch_shapes=[pltpu.VMEM((3, N_SLAB, tm, LANES), F32), pltpu.VMEM((3, 1, tm, LANES), F32)],
        in_specs=acc_specs + stat_specs
                 + [_rows(tm, D_MODEL)] * 3
                 + [_resident((2 * LANES, GROUP_WIDTH)), _resident((GROUP_WIDTH, D_MODEL)),
                    _resident((D_MODEL, D_MODEL)), _resident((1, D_MODEL)), _resident((1, D_MODEL)),
                    _resident((D_MODEL, D_FF)), _resident((D_MODEL, D_FF)), _resident((D_FF, D_MODEL)),
                    _resident((1, D_MODEL)), _resident((1, D_MODEL))],
        out_specs=_rows(tm, D_MODEL),
        out_shape=jax.ShapeDtypeStruct((t, D_MODEL), F32),
        compiler_params=_params(1),
        name="mixer_out_ffn",
    )(*accs, *stats, ga, gm, h1, expand, wab, wout, g2, b2, wg, wu, wd, g3, b3)


def kernel(x, positions, ffn1_w_gate, ffn1_w_up, ffn1_w_down, ln1_g, ln1_b, w_in, b_gates, gmlp_ln_g, gmlp_ln_b, gmlp_w_s, gmlp_b_s, w_attn_branch, w_gmlp_branch, w_out, ln2_g, ln2_b, ffn2_w_gate, ffn2_w_up, ffn2_w_down, ln3_g, ln3_b):
    b, s, d_model = x.shape
    assert d_model == D_MODEL and ln1_g.shape[0] == DEPTH == 1
    t = b * s
    tm = 512
    assert t % tm == 0 and tm % GMLP_CHUNK == 0 and s % tm == 0

    lane = jnp.arange(LANES) % (HEAD_DIM // 2)
    invf = (ROPE_THETA ** (-(2 * lane).astype(F32) / HEAD_DIM)).reshape(1, LANES)
    stat_lane_of_col = jnp.array([_stat_lane(c // HEAD_DIM) for c in range(GROUP_WIDTH)])
    expand = (jnp.arange(LANES)[:, None] == stat_lane_of_col[None, :]).astype(BF16)
    expand = jnp.concatenate([expand, expand], axis=0)
    b_s = jnp.repeat(gmlp_b_s[0].T, GMLP_WIDTH // GMLP_GROUPS, axis=1)

    h = x.reshape(t, D_MODEL)
    w_in_cols = [(base + gi * GROUP_WIDTH, GROUP_WIDTH) for gi in range(N_ATTN_GROUPS) for base in (_Q0, _K0, _V0)]
    w_in_cols.append((_U0, w_in.shape[-1] - _U0))
    h1, h1b, w_in_b, w_gb_b = _ffn_ln(h, ffn1_w_gate[0], ffn1_w_up[0], ffn1_w_down[0], ln1_g, ln1_b,
                                      (w_in[0], w_gmlp_branch[0]), (tuple(w_in_cols), None), tm)
    pos_packed = jnp.repeat(positions.reshape(t * (HEAD_DIM // 2) // LANES, -1), HEAD_DIM // 2, axis=1)
    outs = _mixer_in(h1b, pos_packed, invf, w_in_b, b_gates, gmlp_ln_g, gmlp_ln_b, gmlp_w_s[0], b_s, w_gb_b,
                     (w_attn_branch[0], w_out[0], ffn2_w_gate[0], ffn2_w_up[0], ffn2_w_down[0]), b, tm)
    ga, gm = outs[N_ATTN_GROUPS], outs[N_ATTN_GROUPS + 1]
    out_weights = outs[N_ATTN_GROUPS + 2:]
    accs, stats = [], []
    for gi in range(N_ATTN_GROUPS):
        acc, stat = _attention(outs[gi])
        accs.append(acc)
        stats.append(stat)
    wab, wout, wg2, wu2, wd2 = out_weights
    out = _mixer_out(accs, stats, ga, gm, h1, expand, wab, wout, ln2_g, ln2_b, wg2, wu2, wd2, ln3_g, ln3_b, tm)
    return out.reshape(b, s, D_MODEL)
```

```python
import functools
import math

import jax
import jax.numpy as jnp
from jax import lax
from jax.experimental import pallas as pl
from jax.experimental.pallas import tpu as pltpu

D_MODEL = 1024
HEAD_DIM = 64
HEADS_PER_GROUP = 8
ATTN_PATTERNS = ((128, 1), (512, 4), (2048, 16))
N_ATTN_GROUPS = len(ATTN_PATTERNS)
GROUP_WIDTH = HEADS_PER_GROUP * HEAD_DIM
ATTN_WIDTH = N_ATTN_GROUPS * GROUP_WIDTH
ATTN_BLOCK = 128
ROPE_THETA = 10000.0
GMLP_CHUNK = 128
GMLP_GROUPS = 8
GMLP_WIDTH = D_MODEL
D_FF = 2816
DEPTH = 1
ALPHA = (2 * DEPTH) ** 0.25
LN_EPS = 1e-5

LANES = 128
VMEM_LIMIT_BYTES = 60 * 1024 * 1024
N_SLAB = GROUP_WIDTH // LANES

_Q0, _K0, _V0 = 0, ATTN_WIDTH, 2 * ATTN_WIDTH
_U0 = 3 * ATTN_WIDTH
_VG0 = _U0 + GMLP_WIDTH
_GA0 = _VG0 + GMLP_WIDTH
_GM0 = _GA0 + D_MODEL

_Q_SCALE = HEAD_DIM ** -0.5 * math.log2(math.e)

_DEN_SHIFT = 8


def _stat_lane(h):
    return HEAD_DIM + h if h % 2 == 0 else h


BF16 = jnp.bfloat16
F32 = jnp.float32


def _layer_norm(x, g, b):
    mu = jnp.mean(x, axis=-1, keepdims=True)
    xc = x - mu
    var = jnp.mean(xc * xc, axis=-1, keepdims=True)
    return xc * lax.rsqrt(var + LN_EPS) * g + b


def _dot(a, b):
    return jnp.dot(a, b, preferred_element_type=F32)


def _swiglu(xb, wg_ref, wu_ref, wd_ref):
    g = _dot(xb, wg_ref[...])
    u = _dot(xb, wu_ref[...])
    a = g * jax.nn.sigmoid(g) * u
    return _dot(a.astype(BF16), wd_ref[...])


def _gelu(x):
    return 0.5 * x * (1.0 + lax.erf(x * math.sqrt(0.5)))


def _resident(shape):
    return pl.BlockSpec(shape, lambda *_: (0,) * len(shape), pipeline_mode=pl.Buffered(1))


def _rows(tm, width):
    return pl.BlockSpec((tm, width), lambda i: (i, 0))


def _residue_major_spec(d, tm, width, tiles_per_seq):
    return pl.BlockSpec((None, d, tm // d, width), lambda i: (i // tiles_per_seq, 0, i % tiles_per_seq, 0))


def _params(n_axes):
    return pltpu.CompilerParams(dimension_semantics=("arbitrary",) * n_axes,
                                vmem_limit_bytes=VMEM_LIMIT_BYTES)


_HBM = pl.BlockSpec(memory_space=pl.ANY)


FILL_SLOTS = 4


def _fill_bf16(w_hbm, w_vmem, stage, sem, rows, cols=None):
    k, n = w_hbm.shape
    cols = n if cols is None else cols
    slots = min(stage.shape[0], sem.shape[0])
    assert k % rows == 0 and rows <= stage.shape[1] and cols <= stage.shape[2]
    blocks = [(r, c, min(cols, n - c)) for r in range(0, k, rows) for c in range(0, n, cols)]

    def copy(i):
        r, c, width = blocks[i]
        dst = stage.at[i % slots, pl.ds(0, rows), pl.ds(0, width)]
        return pltpu.make_async_copy(w_hbm.at[pl.ds(r, rows), pl.ds(c, width)], dst, sem.at[i % slots])

    for i in range(min(slots - 1, len(blocks))):
        copy(i).start()
    for i, (r, c, width) in enumerate(blocks):
        if i + slots - 1 < len(blocks):
            copy(i + slots - 1).start()
        copy(i).wait()
        w_vmem[r:r + rows, c:c + width] = stage[i % slots, 0:rows, 0:width].astype(BF16)


def _weight_scratch(shapes, stage_shape=None):
    stage = [] if stage_shape is None else [pltpu.VMEM(stage_shape, F32)]
    return [pltpu.VMEM(s, BF16) for s in shapes] + stage + [pltpu.SemaphoreType.DMA((FILL_SLOTS,))]


BF16_SUBLANES = 16


def _side_cast_spec(shape, n_steps):
    k, n = shape
    n_blocks = n_steps
    while k % n_blocks or (k // n_blocks) % BF16_SUBLANES:
        n_blocks //= 2
    assert n_blocks >= 1 and n_steps % n_blocks == 0
    steps_per_block = n_steps // n_blocks
    return pl.BlockSpec((k // n_blocks, n), lambda i: (i // steps_per_block, 0))


def _side_cast(pairs, col_chunks=()):
    for idx, (src_ref, dst_ref) in enumerate(pairs):
        chunks = col_chunks[idx] if idx < len(col_chunks) and col_chunks[idx] else [(0, src_ref.shape[1])]
        dst0 = 0
        for src0, width in chunks:
            dst_ref[:, dst0:dst0 + width] = src_ref[:, src0:src0 + width].astype(BF16)
            dst0 += width


FFN_FILL_ROWS = 256


def _ffn_ln_kernel(x_ref, wg_hbm, wu_hbm, wd_hbm, g_ref, b_ref, *rest, n_side, side_cols):
    side_in, rest = rest[:n_side], rest[n_side:]
    h_ref, hb_ref = rest[:2]
    side_out, (wg_ref, wu_ref, wd_ref, stage_ref, sem) = rest[2:2 + n_side], rest[2 + n_side:]

    @pl.when(pl.program_id(0) == 0)
    def _():
        for w_hbm, w_ref in ((wg_hbm, wg_ref), (wu_hbm, wu_ref), (wd_hbm, wd_ref)):
            _fill_bf16(w_hbm, w_ref, stage_ref, sem, FFN_FILL_ROWS)

    x = x_ref[...]
    y = _swiglu(x.astype(BF16), wg_ref, wu_ref, wd_ref)
    h = _layer_norm(ALPHA * x + 0.5 * y, g_ref[...], b_ref[...])
    h_ref[...] = h
    hb_ref[...] = pltpu.bitcast(h.astype(BF16), jnp.uint32)
    _side_cast(zip(side_in, side_out), side_cols)


def _ffn_ln(x, wg, wu, wd, g, b, next_weights, side_cols, tm):
    t = x.shape[0]
    n_steps = t // tm
    side_specs = [_side_cast_spec(w.shape, n_steps) for w in next_weights]
    return pl.pallas_call(
        functools.partial(_ffn_ln_kernel, n_side=len(next_weights), side_cols=side_cols),
        grid=(n_steps,),
        in_specs=[_rows(tm, D_MODEL), _HBM, _HBM, _HBM, _resident((1, D_MODEL)), _resident((1, D_MODEL))]
                 + side_specs,
        out_specs=[_rows(tm, D_MODEL), _rows(tm // 2, D_MODEL)] + side_specs,
        out_shape=[jax.ShapeDtypeStruct((t, D_MODEL), F32), jax.ShapeDtypeStruct((t // 2, D_MODEL), jnp.uint32)]
                  + [jax.ShapeDtypeStruct(w.shape, BF16) for w in next_weights],
        scratch_shapes=_weight_scratch([(D_MODEL, D_FF), (D_MODEL, D_FF), (D_FF, D_MODEL)],
                                       (FILL_SLOTS, FFN_FILL_ROWS, D_FF)),
        compiler_params=_params(1),
        name="ffn_ln",
    )(x, wg, wu, wd, g, b, *next_weights)


def _project_group(hb_ref, w_ref, gi, dst_ref, slot):
    width = 3 * GROUP_WIDTH
    res = _dot(pltpu.bitcast(hb_ref[...], BF16), w_ref[:, gi * width:(gi + 1) * width])
    for i in range(3):
        for j in range(N_SLAB):
            c0 = i * GROUP_WIDTH + j * LANES
            dst_ref[slot + i, j] = res[:, c0:c0 + LANES]


def _emit_residue_major(p_ref, out_ref, col0, cos_ref=None, sin_ref=None):
    d, rows, _ = out_ref.shape
    lane = lax.broadcasted_iota(jnp.int32, (1, LANES), 1)
    first_half = (lane % HEAD_DIM) < (HEAD_DIM // 2)
    for r in range(d):
        rs = pl.ds(r, rows, stride=d) if d > 1 else slice(None)
        if cos_ref is not None:
            c, s = cos_ref[rs, :], sin_ref[rs, :]
        blocks = []
        for j in range(N_SLAB):
            x = p_ref[j, rs, :]
            if cos_ref is not None:
                partner = jnp.where(first_half, pltpu.roll(x, LANES - HEAD_DIM // 2, 1),
                                    pltpu.roll(x, HEAD_DIM // 2, 1))
                x = x * c + partner * s
            blocks.append(x)
        out_ref[r, :, col0:col0 + GROUP_WIDTH] = jnp.concatenate(blocks, axis=1).astype(out_ref.dtype)


def _mixer_in_kernel(hb_ref, pos_ref, invf_ref, w_ref, bg_ref, lng_ref, lnb_ref, ws_ref, bs_ref, wgb_ref,
                     *rest, n_side):
    side_in, rest = rest[:n_side], rest[n_side:]
    qkv_refs, (ga_ref, gm_ref) = rest[:N_ATTN_GROUPS], rest[N_ATTN_GROUPS:N_ATTN_GROUPS + 2]
    side_out = rest[N_ATTN_GROUPS + 2:N_ATTN_GROUPS + 2 + n_side]
    a_ref, g_ref, p_ref, cs_ref, vgn_ref = rest[N_ATTN_GROUPS + 2 + n_side:]
    tm = 2 * hb_ref.shape[0]
    _side_cast(zip(side_in, side_out))

    def project_group(gi, slot):
        _project_group(hb_ref, w_ref, gi, p_ref, slot)

    def emit_group(gi, slot):
        out_ref = qkv_refs[gi]
        _emit_residue_major(p_ref.at[slot], out_ref, 0, cs_ref.at[2], cs_ref.at[3])
        _emit_residue_major(p_ref.at[slot + 1], out_ref, GROUP_WIDTH, cs_ref.at[0], cs_ref.at[1])
        _emit_residue_major(p_ref.at[slot + 2], out_ref, 2 * GROUP_WIDTH)

    uv = _dot(pltpu.bitcast(hb_ref[...], BF16), w_ref[:, _U0:_U0 + 2 * GMLP_WIDTH])
    a_ref[0] = uv[:, :GMLP_WIDTH]
    a_ref[1] = uv[:, GMLP_WIDTH:]
    n_freq = HEAD_DIM // 2
    per_row = LANES // n_freq
    ang = pos_ref[...].astype(F32) * invf_ref[...]
    lane = lax.broadcasted_iota(jnp.int32, (1, LANES), 1)
    first_half = (lane % HEAD_DIM) < n_freq
    lane_group = lane // n_freq
    for k, packed in enumerate((jnp.cos(ang), jnp.sin(ang))):
        rolled = [packed] + [pltpu.roll(packed, n_freq * s, 1) for s in range(1, per_row)]
        for a in range(per_row):
            spread = rolled[(0 - a) % per_row]
            for b in range(1, per_row):
                spread = jnp.where(lane_group == b, rolled[(b - a) % per_row], spread)
            if k == 1:
                spread = jnp.where(first_half, -spread, spread)
            rows = pl.ds(a, tm // per_row, stride=per_row)
            cs_ref[k, rows, :] = spread
            cs_ref[k + 2, rows, :] = spread * _Q_SCALE

    gates = _dot(pltpu.bitcast(hb_ref[...], BF16), w_ref[:, _GA0:_GA0 + 2 * D_MODEL]) + bg_ref[...]
    g_ref[1] = gates[:, :D_MODEL]
    g_ref[0] = gates[:, D_MODEL:]
    vgn_ref[...] = _layer_norm(_gelu(a_ref[1]), lng_ref[...], lnb_ref[...]).astype(BF16)

    project_group(0, 0)
    n_chunk = tm // GMLP_CHUNK
    gdim = GMLP_WIDTH // GMLP_GROUPS
    row = lax.broadcasted_iota(jnp.int32, (GMLP_CHUNK, GMLP_CHUNK), 0)
    col = lax.broadcasted_iota(jnp.int32, (GMLP_CHUNK, GMLP_CHUNK), 1)
    causal = col <= row
    for g in range(GMLP_GROUPS):
        ws = jnp.where(causal, ws_ref[g], 0.0).astype(BF16)
        rhs = jnp.concatenate(
            [vgn_ref[c * GMLP_CHUNK:(c + 1) * GMLP_CHUNK, g * gdim:(g + 1) * gdim] for c in range(n_chunk)],
            axis=1)
        mixed = _dot(ws, rhs)
        for c in range(n_chunk):
            a_ref[1, c * GMLP_CHUNK:(c + 1) * GMLP_CHUNK, g * gdim:(g + 1) * gdim] = (
                mixed[:, c * gdim:(c + 1) * gdim] + bs_ref[:, g * gdim:(g + 1) * gdim])
    a_ref[0] = _gelu(a_ref[0])

    project_group(1, 3)
    vgn_ref[...] = (a_ref[0] * a_ref[1]).astype(BF16)
    emit_group(0, 0)
    ga_ref[...] = jax.nn.sigmoid(g_ref[1]).astype(BF16)

    project_group(2, 0)
    emit_group(1, 3)
    g_ref[0] = jax.nn.sigmoid(g_ref[0])

    gm_ref[...] = (g_ref[0] * _dot(vgn_ref[...], wgb_ref[...])).astype(BF16)
    emit_group(2, 0)


def _mixer_in(hb, pos, invf, w_in, b_gates, ln_g, ln_b, w_s, b_s, w_gb, next_weights, batch, tm):
    t = 2 * hb.shape[0]
    seq = t // batch
    n_steps = t // tm
    in_width = w_in.shape[1]
    qkv_specs, qkv_shapes = [], []
    for _, d in ATTN_PATTERNS:
        qkv_specs += [_residue_major_spec(d, tm, 3 * GROUP_WIDTH, seq // tm)]
        qkv_shapes += [jax.ShapeDtypeStruct((batch, d, seq // d, 3 * GROUP_WIDTH), BF16)]
    side_specs = [_side_cast_spec(w.shape, n_steps) for w in next_weights]
    return pl.pallas_call(
        functools.partial(_mixer_in_kernel, n_side=len(next_weights)),
        grid=(n_steps,),
        in_specs=[_rows(tm // 2, D_MODEL), _rows(tm * (HEAD_DIM // 2) // LANES, LANES), _resident((1, LANES)),
                  _resident((D_MODEL, in_width)),
                  _resident((1, 2 * D_MODEL)), _resident((1, GMLP_WIDTH)), _resident((1, GMLP_WIDTH)),
                  _resident((GMLP_GROUPS, GMLP_CHUNK, GMLP_CHUNK)), _resident((GMLP_CHUNK, GMLP_WIDTH)),
                  _resident((GMLP_WIDTH, D_MODEL))] + side_specs,
        out_specs=qkv_specs + [_rows(tm, D_MODEL), _rows(tm, D_MODEL)] + side_specs,
        out_shape=qkv_shapes + [jax.ShapeDtypeStruct((t, D_MODEL), BF16)] * 2
                  + [jax.ShapeDtypeStruct(w.shape, BF16) for w in next_weights],
        scratch_shapes=[pltpu.VMEM((2, tm, GMLP_WIDTH), F32), pltpu.VMEM((2, tm, D_MODEL), F32),
                        pltpu.VMEM((6, N_SLAB, tm, LANES), F32), pltpu.VMEM((4, tm, LANES), F32),
                        pltpu.VMEM((tm, GMLP_WIDTH), BF16)],
        compiler_params=_params(1),
        name="mixer_in",
    )(hb, pos, invf, w_in, b_gates, ln_g, ln_b, w_s, b_s, w_gb, *next_weights)


def _attn_kernel(qkv_ref, side_in_ref, acc_ref, stat_ref, side_out_ref, *, n_blk):
    _side_cast([(side_in_ref, side_out_ref)])
    n_items = qkv_ref.shape[0] // ATTN_BLOCK
    has_prev = n_blk > 1
    row = lax.broadcasted_iota(jnp.int32, (2 * ATTN_BLOCK, ATTN_BLOCK), 0) % ATTN_BLOCK
    col = lax.broadcasted_iota(jnp.int32, (2 * ATTN_BLOCK, ATTN_BLOCK), 1)
    neg_inf = jnp.float32(-jnp.inf)
    bias_cur = jnp.where(col <= row, 0.0, neg_inf)
    bias_prev = jnp.where(col >= row, 0.0, neg_inf)
    lane = lax.broadcasted_iota(jnp.int32, (ATTN_BLOCK, LANES), 1)
    low_head = lane < HEAD_DIM
    n_keys = 2 * ATTN_BLOCK if has_prev else ATTN_BLOCK
    low_head_keys = lax.broadcasted_iota(jnp.int32, (n_keys, LANES), 1) < HEAD_DIM
    contract_last = (((1,), (1,)), ((), ()))

    def item(j, carry):
        r0 = pl.multiple_of(j * ATTN_BLOCK, ATTN_BLOCK)
        if has_prev:
            p0 = pl.multiple_of(jnp.maximum(j - 1, 0) * ATTN_BLOCK, ATTN_BLOCK)
            prev_bias = bias_prev + jnp.where(j % n_blk > 0, 0.0, neg_inf)
        scores = []
        for pair in range(N_SLAB):
            q_sl = slice(pair * LANES, (pair + 1) * LANES)
            k_sl = slice(GROUP_WIDTH + pair * LANES, GROUP_WIDTH + (pair + 1) * LANES)
            q = qkv_ref[pl.ds(r0, ATTN_BLOCK), q_sl]
            zero = jnp.zeros_like(q)
            q2 = jnp.concatenate([jnp.where(low_head, q, zero), jnp.where(low_head, zero, q)], axis=0)
            keys = qkv_ref[pl.ds(r0, ATTN_BLOCK), k_sl]
            if has_prev:
                keys = jnp.concatenate([qkv_ref[pl.ds(p0, ATTN_BLOCK), k_sl], keys], axis=0)
            s = lax.dot_general(q2, keys, contract_last, preferred_element_type=F32)
            if has_prev:
                s = s + jnp.concatenate([prev_bias, bias_cur], axis=1)
            else:
                s = s + bias_cur
            scores.append(s)
        maxes = [jnp.max(s, axis=1, keepdims=True) for s in scores]
        probs = [jnp.exp2(s - m).astype(BF16) for s, m in zip(scores, maxes)]
        stat = jnp.zeros((ATTN_BLOCK, LANES), F32)
        for pair in range(N_SLAB):
            sl = slice(pair * LANES, (pair + 1) * LANES)
            v_sl = slice(2 * GROUP_WIDTH + pair * LANES, 2 * GROUP_WIDTH + (pair + 1) * LANES)
            vals = qkv_ref[pl.ds(r0, ATTN_BLOCK), v_sl]
            if has_prev:
                vals = jnp.concatenate([qkv_ref[pl.ds(p0, ATTN_BLOCK), v_sl], vals], axis=0)
            one = jnp.ones_like(vals)
            r_even = _dot(probs[pair][:ATTN_BLOCK], jnp.where(low_head_keys, vals, one))
            r_odd = _dot(probs[pair][ATTN_BLOCK:], jnp.where(low_head_keys, one, vals))
            acc_ref[pl.ds(r0, ATTN_BLOCK), sl] = jnp.where(low_head, r_even, r_odd).astype(BF16)
            m_even, m_odd = maxes[pair][:ATTN_BLOCK], maxes[pair][ATTN_BLOCK:]
            for h, m_h, r_h in ((2 * pair, m_even, r_even), (2 * pair + 1, m_odd, r_odd)):
                stat = jnp.where(lane == _stat_lane(h), m_h, stat)
                stat = jnp.where(lane == _stat_lane(h) + _DEN_SHIFT, r_h, stat)
        stat_ref[pl.ds(r0, ATTN_BLOCK), :] = stat
        return carry

    lax.fori_loop(0, n_items, item, 0, unroll=n_items)


def _attention(qkv, next_weight):
    b, d, l, width = qkv.shape

    def seq(w):
        return pl.BlockSpec((None, d * l, w), lambda bi: (bi, 0, 0))

    side_spec = _side_cast_spec(next_weight.shape, b)
    acc, stat, weight_b = pl.pallas_call(
        functools.partial(_attn_kernel, n_blk=l // ATTN_BLOCK),
        grid=(b,),
        in_specs=[seq(width), side_spec],
        out_specs=[seq(GROUP_WIDTH), seq(LANES), side_spec],
        out_shape=[jax.ShapeDtypeStruct((b, d * l, GROUP_WIDTH), BF16),
                   jax.ShapeDtypeStruct((b, d * l, LANES), F32),
                   jax.ShapeDtypeStruct(next_weight.shape, BF16)],
        compiler_params=_params(1),
        name=f"band_attention_d{d}",
    )(qkv.reshape(b, d * l, width), next_weight)
    return acc.reshape(b, d, l, GROUP_WIDTH), stat.reshape(b, d, l, LANES), weight_b


def _load_token_major(in_ref, scr_ref):
    d, rows, width = in_ref.shape
    if d == 1:
        return in_ref[0].astype(F32)
    n_slab = width // LANES
    for r in range(d):
        x = in_ref[r].astype(F32)
        for j in range(n_slab):
            scr_ref[j, pl.ds(r, rows, stride=d), :] = x[:, j * LANES:(j + 1) * LANES]
    return jnp.concatenate([scr_ref[j] for j in range(n_slab)], axis=1)


def _mixer_out_kernel(a0_ref, a1_ref, a2_ref, s0_ref, s1_ref, s2_ref, ga_ref, gm_ref, h1_ref, expand_ref,
                      wab_ref, wout_ref, g2_ref, b2_ref, wg_ref, wu_ref, wd_ref, g3_ref, b3_ref, out_ref,
                      ascr_ref, sscr_ref):
    tm = h1_ref.shape[0]
    stats = [_load_token_major(s_ref, sscr_ref.at[i]) for i, s_ref in enumerate((s0_ref, s1_ref, s2_ref))]
    accs = [_load_token_major(a_ref, ascr_ref.at[i]) for i, a_ref in enumerate((a0_ref, a1_ref, a2_ref))]
    lane = lax.broadcasted_iota(jnp.int32, (1, LANES), 1)
    is_max_lane = functools.reduce(jnp.logical_or, [lane == _stat_lane(h) for h in range(HEADS_PER_GROUP)])

    def mix_rows(rows):
        st = [x[rows] for x in stats]
        m = jnp.maximum(jnp.maximum(st[0], st[1]), st[2])
        es = [jnp.exp2(x - m) for x in st]
        dens = [pltpu.roll(x, LANES - _DEN_SHIFT, 1) for x in st]
        den = es[0] * dens[0] + es[1] * dens[1] + es[2] * dens[2]
        inv_den = 1.0 / jnp.where(is_max_lane, den, 1.0)
        y = None
        for e, acc in zip(es, accs):
            w = e * inv_den
            w_hi = w.astype(BF16)
            w_lo = (w - w_hi.astype(F32)).astype(BF16)
            w_wide = _dot(jnp.concatenate([w_hi, w_lo], axis=1), expand_ref[...])
            term = w_wide * acc[rows]
            y = term if y is None else y + term
        branch_a = _dot(y.astype(BF16), wab_ref[...])
        merged = ga_ref[rows, :].astype(F32) * branch_a + gm_ref[rows, :].astype(F32)
        mix = _dot(merged.astype(BF16), wout_ref[...])
        return _layer_norm(ALPHA * h1_ref[rows, :] + mix, g2_ref[...], b2_ref[...])

    h2 = mix_rows(slice(0, tm))
    ffn = _swiglu(h2.astype(BF16), wg_ref, wu_ref, wd_ref)
    out_ref[...] = _layer_norm(ALPHA * h2 + 0.5 * ffn, g3_ref[...], b3_ref[...])


def _mixer_out(accs, stats, ga, gm, h1, expand, wab, wout, g2, b2, wg, wu, wd, g3, b3, tm):
    t = h1.shape[0]
    tiles_per_seq = accs[0].shape[1] * accs[0].shape[2] // tm
    acc_specs = [_residue_major_spec(a.shape[1], tm, GROUP_WIDTH, tiles_per_seq) for a in accs]
    stat_specs = [_residue_major_spec(s.shape[1], tm, LANES, tiles_per_seq) for s in stats]
    return pl.pallas_call(
        _mixer_out_kernel,
        grid=(t // tm,),
        scratch_shapes=[pltpu.VMEM((3, N_SLAB, tm, LANES), F32), pltpu.VMEM((3, 1, tm, LANES), F32)],
        in_specs=acc_specs + stat_specs
                 + [_rows(tm, D_MODEL)] * 3
                 + [_resident((2 * LANES, GROUP_WIDTH)), _resident((GROUP_WIDTH, D_MODEL)),
                    _resident((D_MODEL, D_MODEL)), _resident((1, D_MODEL)), _resident((1, D_MODEL)),
                    _resident((D_MODEL, D_FF)), _resident((D_MODEL, D_FF)), _resident((D_FF, D_MODEL)),
                    _resident((1, D_MODEL)), _resident((1, D_MODEL))],
        out_specs=_rows(tm, D_MODEL),
        out_shape=jax.ShapeDtypeStruct((t, D_MODEL), F32),
        compiler_params=_params(1),
        name="mixer_out_ffn",
    )(*accs, *stats, ga, gm, h1, expand, wab, wout, g2, b2, wg, wu, wd, g3, b3)


def kernel(x, positions, ffn1_w_gate, ffn1_w_up, ffn1_w_down, ln1_g, ln1_b, w_in, b_gates, gmlp_ln_g, gmlp_ln_b, gmlp_w_s, gmlp_b_s, w_attn_branch, w_gmlp_branch, w_out, ln2_g, ln2_b, ffn2_w_gate, ffn2_w_up, ffn2_w_down, ln3_g, ln3_b):
    b, s, d_model = x.shape
    assert d_model == D_MODEL and ln1_g.shape[0] == DEPTH == 1
    t = b * s
    tm = 512
    assert t % tm == 0 and tm % GMLP_CHUNK == 0 and s % tm == 0

    lane = jnp.arange(LANES) % (HEAD_DIM // 2)
    invf = (ROPE_THETA ** (-(2 * lane).astype(F32) / HEAD_DIM)).reshape(1, LANES)
    stat_lane_of_col = jnp.array([_stat_lane(c // HEAD_DIM) for c in range(GROUP_WIDTH)])
    expand = (jnp.arange(LANES)[:, None] == stat_lane_of_col[None, :]).astype(BF16)
    expand = jnp.concatenate([expand, expand], axis=0)
    b_s = jnp.repeat(gmlp_b_s[0].T, GMLP_WIDTH // GMLP_GROUPS, axis=1)

    h = x.reshape(t, D_MODEL)
    w_in_cols = [(base + gi * GROUP_WIDTH, GROUP_WIDTH) for gi in range(N_ATTN_GROUPS) for base in (_Q0, _K0, _V0)]
    w_in_cols.append((_U0, w_in.shape[-1] - _U0))
    h1, h1b, w_in_b, w_gb_b = _ffn_ln(h, ffn1_w_gate[0], ffn1_w_up[0], ffn1_w_down[0], ln1_g, ln1_b,
                                      (w_in[0], w_gmlp_branch[0]), (tuple(w_in_cols), None), tm)
    pos_packed = jnp.repeat(positions.reshape(t * (HEAD_DIM // 2) // LANES, -1), HEAD_DIM // 2, axis=1)
    outs = _mixer_in(h1b, pos_packed, invf, w_in_b, b_gates, gmlp_ln_g, gmlp_ln_b, gmlp_w_s[0], b_s, w_gb_b,
                     (w_attn_branch[0], w_out[0]), b, tm)
    ga, gm = outs[N_ATTN_GROUPS], outs[N_ATTN_GROUPS + 1]
    wab, wout = outs[N_ATTN_GROUPS + 2:]
    accs, stats, ffn2_b = [], [], []
    for gi, w_ffn2 in enumerate((ffn2_w_gate[0], ffn2_w_up[0], ffn2_w_down[0])):
        acc, stat, w_b = _attention(outs[gi], w_ffn2)
        accs.append(acc)
        stats.append(stat)
        ffn2_b.append(w_b)
    out = _mixer_out(accs, stats, ga, gm, h1, expand, wab, wout, ln2_g, ln2_b, *ffn2_b, ln3_g, ln3_b, tm)
    return out.reshape(b, s, D_MODEL)
```

```python
import functools
import math

import jax
import jax.numpy as jnp
from jax import lax
from jax.experimental import pallas as pl
from jax.experimental.pallas import tpu as pltpu

D_MODEL = 1024
HEAD_DIM = 64
HEADS_PER_GROUP = 8
ATTN_PATTERNS = ((128, 1), (512, 4), (2048, 16))
N_ATTN_GROUPS = len(ATTN_PATTERNS)
GROUP_WIDTH = HEADS_PER_GROUP * HEAD_DIM
ATTN_WIDTH = N_ATTN_GROUPS * GROUP_WIDTH
ATTN_BLOCK = 128
ROPE_THETA = 10000.0
GMLP_CHUNK = 128
GMLP_GROUPS = 8
GMLP_WIDTH = D_MODEL
D_FF = 2816
DEPTH = 1
ALPHA = (2 * DEPTH) ** 0.25
LN_EPS = 1e-5

LANES = 128
VMEM_LIMIT_BYTES = 60 * 1024 * 1024
N_SLAB = GROUP_WIDTH // LANES

_Q0, _K0, _V0 = 0, ATTN_WIDTH, 2 * ATTN_WIDTH
_U0 = 3 * ATTN_WIDTH
_VG0 = _U0 + GMLP_WIDTH
_GA0 = _VG0 + GMLP_WIDTH

_Q_SCALE = HEAD_DIM ** -0.5 * math.log2(math.e)

_DEN_SHIFT = 8


def _stat_lane(h):
    return HEAD_DIM + h if h % 2 == 0 else h


BF16 = jnp.bfloat16
F32 = jnp.float32


def _layer_norm(x, g, b, eps=LN_EPS):
    mu = jnp.mean(x, axis=-1, keepdims=True)
    xc = x - mu
    var = jnp.mean(xc * xc, axis=-1, keepdims=True)
    return xc * lax.rsqrt(var + eps) * g + b


def _deepnorm(x, update, scale, g, b):
    return _layer_norm(x + (scale / ALPHA) * update, g, b, LN_EPS / ALPHA ** 2)


def _dot(a, b):
    return jnp.dot(a, b, preferred_element_type=F32)


def _swiglu(xb, wg_ref, wu_ref, wd_ref):
    g = _dot(xb, wg_ref[...])
    u = _dot(xb, wu_ref[...])
    a = g * jax.nn.sigmoid(g) * u
    return _dot(a.astype(BF16), wd_ref[...])


def _gelu(x):
    return 0.5 * x * (1.0 + lax.erf(x * math.sqrt(0.5)))


def _resident(shape):
    return pl.BlockSpec(shape, lambda *_: (0,) * len(shape), pipeline_mode=pl.Buffered(1))


def _rows(tm, width):
    return pl.BlockSpec((tm, width), lambda i: (i, 0))


def _residue_major_spec(d, tm, width, tiles_per_seq):
    return pl.BlockSpec((None, d, tm // d, width), lambda i: (i // tiles_per_seq, 0, i % tiles_per_seq, 0))


def _params(n_axes):
    return pltpu.CompilerParams(dimension_semantics=("arbitrary",) * n_axes,
                                vmem_limit_bytes=VMEM_LIMIT_BYTES)


_HBM = pl.BlockSpec(memory_space=pl.ANY)


FILL_SLOTS = 4


def _fill_bf16(w_hbm, w_vmem, stage, sem, rows):
    k, n = w_hbm.shape
    slots = stage.shape[0]
    assert k % rows == 0 and rows == stage.shape[1] and n <= stage.shape[2] and sem.shape[0] == slots
    n_blk = k // rows

    def copy(i):
        dst = stage.at[i % slots, :, pl.ds(0, n)]
        return pltpu.make_async_copy(w_hbm.at[pl.ds(i * rows, rows), :], dst, sem.at[i % slots])

    for i in range(min(slots - 1, n_blk)):
        copy(i).start()
    for i in range(n_blk):
        if i + slots - 1 < n_blk:
            copy(i + slots - 1).start()
        copy(i).wait()
        w_vmem[i * rows:(i + 1) * rows, :] = stage[i % slots, :, 0:n].astype(BF16)


def _weight_scratch(shapes, stage_rows, stage_cols):
    return ([pltpu.VMEM(s, BF16) for s in shapes]
            + [pltpu.VMEM((FILL_SLOTS, stage_rows, stage_cols), F32), pltpu.SemaphoreType.DMA((FILL_SLOTS,))])


BF16_SUBLANES = 16


def _side_cast_spec(shape, n_steps):
    k, n = shape
    n_blocks = n_steps
    while k % n_blocks or (k // n_blocks) % BF16_SUBLANES:
        n_blocks //= 2
    assert n_blocks >= 1 and n_steps % n_blocks == 0
    steps_per_block = n_steps // n_blocks
    return pl.BlockSpec((k // n_blocks, n), lambda i: (i // steps_per_block, 0))


def _side_cast(pairs, col_chunks=()):
    for idx, (src_ref, dst_ref) in enumerate(pairs):
        chunks = col_chunks[idx] if idx < len(col_chunks) and col_chunks[idx] else [(0, src_ref.shape[1])]
        dst0 = 0
        for src0, width in chunks:
            dst_ref[:, dst0:dst0 + width] = src_ref[:, src0:src0 + width].astype(BF16)
            dst0 += width


FFN_FILL_ROWS = 256


def _ffn_ln_kernel(x_ref, wg_hbm, wu_hbm, wd_hbm, g_ref, b_ref, *rest, n_side, side_cols):
    side_in, rest = rest[:n_side], rest[n_side:]
    h_ref, hb_ref = rest[:2]
    side_out, (wg_ref, wu_ref, wd_ref, stage_ref, sem) = rest[2:2 + n_side], rest[2 + n_side:]

    @pl.when(pl.program_id(0) == 0)
    def _():
        for w_hbm, w_ref in ((wg_hbm, wg_ref), (wu_hbm, wu_ref), (wd_hbm, wd_ref)):
            _fill_bf16(w_hbm, w_ref, stage_ref, sem, FFN_FILL_ROWS)

    x = x_ref[...]
    y = _swiglu(x.astype(BF16), wg_ref, wu_ref, wd_ref)
    h = _deepnorm(x, y, 0.5, g_ref[...], b_ref[...])
    h_ref[...] = h
    hb_ref[...] = pltpu.bitcast(h.astype(BF16), jnp.uint32)
    _side_cast(zip(side_in, side_out), side_cols)


def _ffn_ln(x, wg, wu, wd, g, b, next_weights, side_cols, tm):
    t = x.shape[0]
    n_steps = t // tm
    side_specs = [_side_cast_spec(w.shape, n_steps) for w in next_weights]
    return pl.pallas_call(
        functools.partial(_ffn_ln_kernel, n_side=len(next_weights), side_cols=side_cols),
        grid=(n_steps,),
        in_specs=[_rows(tm, D_MODEL), _HBM, _HBM, _HBM, _resident((1, D_MODEL)), _resident((1, D_MODEL))]
                 + side_specs,
        out_specs=[_rows(tm, D_MODEL), _rows(tm // 2, D_MODEL)] + side_specs,
        out_shape=[jax.ShapeDtypeStruct((t, D_MODEL), F32), jax.ShapeDtypeStruct((t // 2, D_MODEL), jnp.uint32)]
                  + [jax.ShapeDtypeStruct(w.shape, BF16) for w in next_weights],
        scratch_shapes=_weight_scratch([(D_MODEL, D_FF), (D_MODEL, D_FF), (D_FF, D_MODEL)], FFN_FILL_ROWS, D_FF),
        compiler_params=_params(1),
        name="ffn_ln",
    )(x, wg, wu, wd, g, b, *next_weights)


def _project_group(hb_ref, w_ref, gi, dst_ref, slot):
    width = 3 * GROUP_WIDTH
    res = _dot(pltpu.bitcast(hb_ref[...], BF16), w_ref[:, gi * width:(gi + 1) * width])
    for i in range(3):
        for j in range(N_SLAB):
            c0 = i * GROUP_WIDTH + j * LANES
            dst_ref[slot + i, j] = res[:, c0:c0 + LANES]


GATHER_STRIDE = 4


def _emit_residue_major(p_ref, out_ref, col0, tmp_ref, cos_ref=None, sin_ref=None):
    d, rows, _ = out_ref.shape
    tm = d * rows
    sources = [p_ref.at[j] for j in range(N_SLAB)] + ([cos_ref, sin_ref] if cos_ref is not None else [])
    two_level = d > GATHER_STRIDE
    if two_level:
        part = tm // GATHER_STRIDE
        for i, src in enumerate(sources):
            for a in range(GATHER_STRIDE):
                tmp_ref[i, a * part:(a + 1) * part, :] = src[pl.ds(a, part, stride=GATHER_STRIDE), :]
        sources = [tmp_ref.at[i] for i in range(len(sources))]
    lane = lax.broadcasted_iota(jnp.int32, (1, LANES), 1)
    first_half = (lane % HEAD_DIM) < (HEAD_DIM // 2)
    for r in range(d):
        if two_level:
            a, b = r % GATHER_STRIDE, r // GATHER_STRIDE
            rs = pl.ds(a * part + b, rows, stride=d // GATHER_STRIDE)
        else:
            rs = pl.ds(r, rows, stride=d) if d > 1 else slice(None)
        if cos_ref is not None:
            c, s = sources[N_SLAB][rs, :], sources[N_SLAB + 1][rs, :]
        blocks = []
        for j in range(N_SLAB):
            x = sources[j][rs, :]
            if cos_ref is not None:
                partner = jnp.where(first_half, pltpu.roll(x, LANES - HEAD_DIM // 2, 1),
                                    pltpu.roll(x, HEAD_DIM // 2, 1))
                x = x * c + partner * s
            blocks.append(x)
        out_ref[r, :, col0:col0 + GROUP_WIDTH] = jnp.concatenate(blocks, axis=1).astype(out_ref.dtype)


def _mixer_in_kernel(hb_ref, pos_ref, invf_ref, w_ref, bg_ref, lng_ref, lnb_ref, ws_ref, bs_ref, wgb_ref,
                     *rest, n_side):
    side_in, rest = rest[:n_side], rest[n_side:]
    qkv_refs, (ga_ref, gm_ref) = rest[:N_ATTN_GROUPS], rest[N_ATTN_GROUPS:N_ATTN_GROUPS + 2]
    side_out = rest[N_ATTN_GROUPS + 2:N_ATTN_GROUPS + 2 + n_side]
    a_ref, g_ref, p_ref, cs_ref, vgn_ref, tmp_ref = rest[N_ATTN_GROUPS + 2 + n_side:]
    tm = 2 * hb_ref.shape[0]
    _side_cast(zip(side_in, side_out))

    def project_group(gi, slot):
        _project_group(hb_ref, w_ref, gi, p_ref, slot)

    def emit_group(gi, slot):
        out_ref = qkv_refs[gi]
        _emit_residue_major(p_ref.at[slot], out_ref, 0, tmp_ref, cs_ref.at[2], cs_ref.at[3])
        _emit_residue_major(p_ref.at[slot + 1], out_ref, GROUP_WIDTH, tmp_ref, cs_ref.at[0], cs_ref.at[1])
        _emit_residue_major(p_ref.at[slot + 2], out_ref, 2 * GROUP_WIDTH, tmp_ref)

    uv = _dot(pltpu.bitcast(hb_ref[...], BF16), w_ref[:, _U0:_U0 + 2 * GMLP_WIDTH])
    a_ref[0] = uv[:, :GMLP_WIDTH]
    a_ref[1] = uv[:, GMLP_WIDTH:]
    n_freq = HEAD_DIM // 2
    per_row = LANES // n_freq
    ang = pos_ref[...].astype(F32) * invf_ref[...]
    lane = lax.broadcasted_iota(jnp.int32, (1, LANES), 1)
    first_half = (lane % HEAD_DIM) < n_freq
    lane_group = lane // n_freq
    for k, packed in enumerate((jnp.cos(ang), jnp.sin(ang))):
        rolled = [packed] + [pltpu.roll(packed, n_freq * s, 1) for s in range(1, per_row)]
        for a in range(per_row):
            spread = rolled[(0 - a) % per_row]
            for b in range(1, per_row):
                spread = jnp.where(lane_group == b, rolled[(b - a) % per_row], spread)
            if k == 1:
                spread = jnp.where(first_half, -spread, spread)
            rows = pl.ds(a, tm // per_row, stride=per_row)
            cs_ref[k, rows, :] = spread
            cs_ref[k + 2, rows, :] = spread * _Q_SCALE

    gates = _dot(pltpu.bitcast(hb_ref[...], BF16), w_ref[:, _GA0:_GA0 + 2 * D_MODEL]) + bg_ref[...]
    g_ref[1] = gates[:, :D_MODEL]
    g_ref[0] = gates[:, D_MODEL:]
    vgn_ref[...] = _layer_norm(_gelu(a_ref[1]), lng_ref[...], lnb_ref[...]).astype(BF16)

    project_group(0, 0)
    n_chunk = tm // GMLP_CHUNK
    gdim = GMLP_WIDTH // GMLP_GROUPS
    row = lax.broadcasted_iota(jnp.int32, (GMLP_CHUNK, GMLP_CHUNK), 0)
    col = lax.broadcasted_iota(jnp.int32, (GMLP_CHUNK, GMLP_CHUNK), 1)
    causal = col <= row
    for g in range(GMLP_GROUPS):
        ws = jnp.where(causal, ws_ref[g], 0.0).astype(BF16)
        rhs = jnp.concatenate(
            [vgn_ref[c * GMLP_CHUNK:(c + 1) * GMLP_CHUNK, g * gdim:(g + 1) * gdim] for c in range(n_chunk)],
            axis=1)
        mixed = _dot(ws, rhs)
        for c in range(n_chunk):
            a_ref[1, c * GMLP_CHUNK:(c + 1) * GMLP_CHUNK, g * gdim:(g + 1) * gdim] = (
                mixed[:, c * gdim:(c + 1) * gdim] + bs_ref[:, g * gdim:(g + 1) * gdim])
    a_ref[0] = _gelu(a_ref[0])

    project_group(1, 3)
    vgn_ref[...] = (a_ref[0] * a_ref[1]).astype(BF16)
    emit_group(0, 0)
    ga_ref[...] = jax.nn.sigmoid(g_ref[1]).astype(BF16)

    project_group(2, 0)
    emit_group(1, 3)
    g_ref[0] = jax.nn.sigmoid(g_ref[0])

    gm_ref[...] = (g_ref[0] * _dot(vgn_ref[...], wgb_ref[...])).astype(BF16)
    emit_group(2, 0)


def _mixer_in(hb, pos, invf, w_in, b_gates, ln_g, ln_b, w_s, b_s, w_gb, next_weights, batch, tm):
    t = 2 * hb.shape[0]
    seq = t // batch
    n_steps = t // tm
    in_width = w_in.shape[1]
    qkv_specs, qkv_shapes = [], []
    for _, d in ATTN_PATTERNS:
        qkv_specs += [_residue_major_spec(d, tm, 3 * GROUP_WIDTH, seq // tm)]
        qkv_shapes += [jax.ShapeDtypeStruct((batch, d, seq // d, 3 * GROUP_WIDTH), BF16)]
    side_specs = [_side_cast_spec(w.shape, n_steps) for w in next_weights]
    return pl.pallas_call(
        functools.partial(_mixer_in_kernel, n_side=len(next_weights)),
        grid=(n_steps,),
        in_specs=[_rows(tm // 2, D_MODEL), _rows(tm * (HEAD_DIM // 2) // LANES, LANES), _resident((1, LANES)),
                  _resident((D_MODEL, in_width)),
                  _resident((1, 2 * D_MODEL)), _resident((1, GMLP_WIDTH)), _resident((1, GMLP_WIDTH)),
                  _resident((GMLP_GROUPS, GMLP_CHUNK, GMLP_CHUNK)), _resident((GMLP_CHUNK, GMLP_WIDTH)),
                  _resident((GMLP_WIDTH, D_MODEL))] + side_specs,
        out_specs=qkv_specs + [_rows(tm, D_MODEL), _rows(tm, D_MODEL)] + side_specs,
        out_shape=qkv_shapes + [jax.ShapeDtypeStruct((t, D_MODEL), BF16)] * 2
                  + [jax.ShapeDtypeStruct(w.shape, BF16) for w in next_weights],
        scratch_shapes=[pltpu.VMEM((2, tm, GMLP_WIDTH), F32), pltpu.VMEM((2, tm, D_MODEL), F32),
                        pltpu.VMEM((6, N_SLAB, tm, LANES), F32), pltpu.VMEM((4, tm, LANES), F32),
                        pltpu.VMEM((tm, GMLP_WIDTH), BF16), pltpu.VMEM((N_SLAB + 2, tm, LANES), F32)],
        compiler_params=_params(1),
        name="mixer_in",
    )(hb, pos, invf, w_in, b_gates, ln_g, ln_b, w_s, b_s, w_gb, *next_weights)


def _attn_kernel(qkv_ref, acc_ref, stat_ref, *, n_blk):
    n_items = qkv_ref.shape[0] // ATTN_BLOCK
    has_prev = n_blk > 1
    row = lax.broadcasted_iota(jnp.int32, (2 * ATTN_BLOCK, ATTN_BLOCK), 0) % ATTN_BLOCK
    col = lax.broadcasted_iota(jnp.int32, (2 * ATTN_BLOCK, ATTN_BLOCK), 1)
    neg_inf = jnp.float32(-jnp.inf)
    bias_cur = jnp.where(col <= row, 0.0, neg_inf)
    bias_prev = jnp.where(col >= row, 0.0, neg_inf)
    lane = lax.broadcasted_iota(jnp.int32, (ATTN_BLOCK, LANES), 1)
    low_head = lane < HEAD_DIM
    n_keys = 2 * ATTN_BLOCK if has_prev else ATTN_BLOCK
    low_head_keys = lax.broadcasted_iota(jnp.int32, (n_keys, LANES), 1) < HEAD_DIM
    contract_last = (((1,), (1,)), ((), ()))

    def item(j, carry):
        r0 = pl.multiple_of(j * ATTN_BLOCK, ATTN_BLOCK)
        if has_prev:
            p0 = pl.multiple_of(jnp.maximum(j - 1, 0) * ATTN_BLOCK, ATTN_BLOCK)
            prev_bias = bias_prev + jnp.where(j % n_blk > 0, 0.0, neg_inf)
        scores = []
        for pair in range(N_SLAB):
            q_sl = slice(pair * LANES, (pair + 1) * LANES)
            k_sl = slice(GROUP_WIDTH + pair * LANES, GROUP_WIDTH + (pair + 1) * LANES)
            q = qkv_ref[pl.ds(r0, ATTN_BLOCK), q_sl]
            zero = jnp.zeros_like(q)
            q2 = jnp.concatenate([jnp.where(low_head, q, zero), jnp.where(low_head, zero, q)], axis=0)
            keys = qkv_ref[pl.ds(r0, ATTN_BLOCK), k_sl]
            if has_prev:
                keys = jnp.concatenate([qkv_ref[pl.ds(p0, ATTN_BLOCK), k_sl], keys], axis=0)
            s = lax.dot_general(q2, keys, contract_last, preferred_element_type=F32)
            if has_prev:
                s = s + jnp.concatenate([prev_bias, bias_cur], axis=1)
            else:
                s = s + bias_cur
            scores.append(s)
        maxes = [jnp.max(s, axis=1, keepdims=True) for s in scores]
        probs = [jnp.exp2(s - m).astype(BF16) for s, m in zip(scores, maxes)]
        stat = jnp.zeros((ATTN_BLOCK, LANES), F32)
        for pair in range(N_SLAB):
            sl = slice(pair * LANES, (pair + 1) * LANES)
            v_sl = slice(2 * GROUP_WIDTH + pair * LANES, 2 * GROUP_WIDTH + (pair + 1) * LANES)
            vals = qkv_ref[pl.ds(r0, ATTN_BLOCK), v_sl]
            if has_prev:
                vals = jnp.concatenate([qkv_ref[pl.ds(p0, ATTN_BLOCK), v_sl], vals], axis=0)
            one = jnp.ones_like(vals)
            r_even = _dot(probs[pair][:ATTN_BLOCK], jnp.where(low_head_keys, vals, one))
            r_odd = _dot(probs[pair][ATTN_BLOCK:], jnp.where(low_head_keys, one, vals))
            acc_ref[pl.ds(r0, ATTN_BLOCK), sl] = jnp.where(low_head, r_even, r_odd).astype(BF16)
            m_even, m_odd = maxes[pair][:ATTN_BLOCK], maxes[pair][ATTN_BLOCK:]
            for h, m_h, r_h in ((2 * pair, m_even, r_even), (2 * pair + 1, m_odd, r_odd)):
                stat = jnp.where(lane == _stat_lane(h), m_h, stat)
                stat = jnp.where(lane == _stat_lane(h) + _DEN_SHIFT, r_h, stat)
        stat_ref[pl.ds(r0, ATTN_BLOCK), :] = stat
        return carry

    lax.fori_loop(0, n_items, item, 0, unroll=n_items)


def _attention(qkv):
    b, d, l, width = qkv.shape

    def seq(w):
        return pl.BlockSpec((None, d * l, w), lambda bi: (bi, 0, 0))

    acc, stat = pl.pallas_call(
        functools.partial(_attn_kernel, n_blk=l // ATTN_BLOCK),
        grid=(b,),
        in_specs=[seq(width)],
        out_specs=[seq(GROUP_WIDTH), seq(LANES)],
        out_shape=[jax.ShapeDtypeStruct((b, d * l, GROUP_WIDTH), BF16),
                   jax.ShapeDtypeStruct((b, d * l, LANES), F32)],
        compiler_params=_params(1),
        name=f"band_attention_d{d}",
    )(qkv.reshape(b, d * l, width))
    return acc.reshape(b, d, l, GROUP_WIDTH), stat.reshape(b, d, l, LANES)


def _load_token_major(in_ref, scr_ref):
    d, rows, width = in_ref.shape
    if d == 1:
        return in_ref[0].astype(F32)
    n_slab = width // LANES
    for r in range(d):
        x = in_ref[r].astype(F32)
        for j in range(n_slab):
            scr_ref[j, pl.ds(r, rows, stride=d), :] = x[:, j * LANES:(j + 1) * LANES]
    return jnp.concatenate([scr_ref[j] for j in range(n_slab)], axis=1)


def _mixer_out_kernel(a0_ref, a1_ref, a2_ref, s0_ref, s1_ref, s2_ref, ga_ref, gm_ref, h1_ref, expand_ref,
                      wab_ref, wout_ref, g2_ref, b2_ref, wg_ref, wu_ref, wd_ref, g3_ref, b3_ref, out_ref,
                      ascr_ref, sscr_ref):
    stats = [_load_token_major(s_ref, sscr_ref.at[i]) for i, s_ref in enumerate((s0_ref, s1_ref, s2_ref))]
    accs = [_load_token_major(a_ref, ascr_ref.at[i]) for i, a_ref in enumerate((a0_ref, a1_ref, a2_ref))]
    lane = lax.broadcasted_iota(jnp.int32, (1, LANES), 1)
    is_max_lane = functools.reduce(jnp.logical_or, [lane == _stat_lane(h) for h in range(HEADS_PER_GROUP)])
    m = jnp.maximum(jnp.maximum(stats[0], stats[1]), stats[2])
    es = [jnp.exp2(x - m) for x in stats]
    dens = [pltpu.roll(x, LANES - _DEN_SHIFT, 1) for x in stats]
    den = es[0] * dens[0] + es[1] * dens[1] + es[2] * dens[2]
    inv_den = 1.0 / jnp.where(is_max_lane, den, 1.0)
    y = None
    for e, acc in zip(es, accs):
        w = e * inv_den
        w_hi = w.astype(BF16)
        w_lo = (w - w_hi.astype(F32)).astype(BF16)
        w_wide = _dot(jnp.concatenate([w_hi, w_lo], axis=1), expand_ref[...])
        term = w_wide * acc
        y = term if y is None else y + term
    branch_a = _dot(y.astype(BF16), wab_ref[...])
    merged = ga_ref[...].astype(F32) * branch_a + gm_ref[...].astype(F32)
    mix = _dot(merged.astype(BF16), wout_ref[...])
    h2 = _deepnorm(h1_ref[...], mix, 1.0, g2_ref[...], b2_ref[...])
    ffn = _swiglu(h2.astype(BF16), wg_ref, wu_ref, wd_ref)
    out_ref[...] = _deepnorm(h2, ffn, 0.5, g3_ref[...], b3_ref[...])


def _mixer_out(accs, stats, ga, gm, h1, expand, wab, wout, g2, b2, wg, wu, wd, g3, b3, tm):
    t = h1.shape[0]
    tiles_per_seq = accs[0].shape[1] * accs[0].shape[2] // tm
    acc_specs = [_residue_major_spec(a.shape[1], tm, GROUP_WIDTH, tiles_per_seq) for a in accs]
    stat_specs = [_residue_major_spec(s.shape[1], tm, LANES, tiles_per_seq) for s in stats]
    return pl.pallas_call(
        _mixer_out_kernel,
        grid=(t // tm,),
        scratch_shapes=[pltpu.VMEM((3, N_SLAB, tm, LANES), F32), pltpu.VMEM((3, 1, tm, LANES), F32)],
        in_specs=acc_specs + stat_specs
                 + [_rows(tm, D_MODEL)] * 3
                 + [_resident((2 * LANES, GROUP_WIDTH)), _resident((GROUP_WIDTH, D_MODEL)),
                    _resident((D_MODEL, D_MODEL)), _resident((1, D_MODEL)), _resident((1, D_MODEL)),
                    _resident((D_MODEL, D_FF)), _resident((D_MODEL, D_FF)), _resident((D_FF, D_MODEL)),
                    _resident((1, D_MODEL)), _resident((1, D_MODEL))],
        out_specs=_rows(tm, D_MODEL),
        out_shape=jax.ShapeDtypeStruct((t, D_MODEL), F32),
        compiler_params=_params(1),
        name="mixer_out_ffn",
    )(*accs, *stats, ga, gm, h1, expand, wab, wout, g2, b2, wg, wu, wd, g3, b3)


def kernel(x, positions, ffn1_w_gate, ffn1_w_up, ffn1_w_down, ln1_g, ln1_b, w_in, b_gates, gmlp_ln_g, gmlp_ln_b, gmlp_w_s, gmlp_b_s, w_attn_branch, w_gmlp_branch, w_out, ln2_g, ln2_b, ffn2_w_gate, ffn2_w_up, ffn2_w_down, ln3_g, ln3_b):
    b, s, d_model = x.shape
    assert d_model == D_MODEL and ln1_g.shape[0] == DEPTH == 1
    t = b * s
    tm = 512
    assert t % tm == 0 and tm % GMLP_CHUNK == 0 and s % tm == 0

    lane = jnp.arange(LANES) % (HEAD_DIM // 2)
    invf = (ROPE_THETA ** (-(2 * lane).astype(F32) / HEAD_DIM)).reshape(1, LANES)
    stat_lane_of_col = jnp.array([_stat_lane(c // HEAD_DIM) for c in range(GROUP_WIDTH)])
    expand = (jnp.arange(LANES)[:, None] == stat_lane_of_col[None, :]).astype(BF16)
    expand = jnp.concatenate([expand, expand], axis=0)
    b_s = jnp.repeat(gmlp_b_s[0].T, GMLP_WIDTH // GMLP_GROUPS, axis=1)

    h = x.reshape(t, D_MODEL)
    w_in_cols = [(base + gi * GROUP_WIDTH, GROUP_WIDTH) for gi in range(N_ATTN_GROUPS) for base in (_Q0, _K0, _V0)]
    w_in_cols.append((_U0, w_in.shape[-1] - _U0))
    h1, h1b, w_in_b, w_gb_b = _ffn_ln(h, ffn1_w_gate[0], ffn1_w_up[0], ffn1_w_down[0], ln1_g, ln1_b,
                                      (w_in[0], w_gmlp_branch[0]), (tuple(w_in_cols), None), tm)
    pos_packed = jnp.repeat(positions.reshape(t * (HEAD_DIM // 2) // LANES, -1), HEAD_DIM // 2, axis=1)
    outs = _mixer_in(h1b, pos_packed, invf, w_in_b, b_gates, gmlp_ln_g, gmlp_ln_b, gmlp_w_s[0], b_s, w_gb_b,
                     (w_attn_branch[0], w_out[0], ffn2_w_gate[0], ffn2_w_up[0], ffn2_w_down[0]), b, tm)
    ga, gm = outs[N_ATTN_GROUPS], outs[N_ATTN_GROUPS + 1]
    out_weights = outs[N_ATTN_GROUPS + 2:]
    accs, stats = [], []
    for gi in range(N_ATTN_GROUPS):
        acc, stat = _attention(outs[gi])
        accs.append(acc)
        stats.append(stat)
    wab, wout, wg2, wu2, wd2 = out_weights
    out = _mixer_out(accs, stats, ga, gm, h1, expand, wab, wout, ln2_g, ln2_b, wg2, wu2, wd2, ln3_g, ln3_b, tm)
    return out.reshape(b, s, D_MODEL)
```

```python
import functools
import math

import jax
import jax.numpy as jnp
from jax import lax
from jax.experimental import pallas as pl
from jax.experimental.pallas import tpu as pltpu

D_MODEL = 1024
HEAD_DIM = 64
HEADS_PER_GROUP = 8
ATTN_PATTERNS = ((128, 1), (512, 4), (2048, 16))
N_ATTN_GROUPS = len(ATTN_PATTERNS)
GROUP_WIDTH = HEADS_PER_GROUP * HEAD_DIM
ATTN_WIDTH = N_ATTN_GROUPS * GROUP_WIDTH
ATTN_BLOCK = 128
ROPE_THETA = 10000.0
GMLP_CHUNK = 128
GMLP_GROUPS = 8
GMLP_WIDTH = D_MODEL
D_FF = 2816
DEPTH = 1
ALPHA = (2 * DEPTH) ** 0.25
LN_EPS = 1e-5

LANES = 128
VMEM_LIMIT_BYTES = 60 * 1024 * 1024
N_SLAB = GROUP_WIDTH // LANES

_Q0, _K0, _V0 = 0, ATTN_WIDTH, 2 * ATTN_WIDTH
_U0 = 3 * ATTN_WIDTH
_VG0 = _U0 + GMLP_WIDTH
_GA0 = _VG0 + GMLP_WIDTH

_Q_SCALE = HEAD_DIM ** -0.5 * math.log2(math.e)

_DEN_SHIFT = 8


def _stat_lane(h):
    return HEAD_DIM + h if h % 2 == 0 else h


BF16 = jnp.bfloat16
F32 = jnp.float32


def _layer_norm(x, g, b, eps=LN_EPS):
    mu = jnp.mean(x, axis=-1, keepdims=True)
    xc = x - mu
    var = jnp.mean(xc * xc, axis=-1, keepdims=True)
    return xc * lax.rsqrt(var + eps) * g + b


def _deepnorm(x, update, scale, g, b):
    return _layer_norm(x + (scale / ALPHA) * update, g, b, LN_EPS / ALPHA ** 2)


def _dot(a, b):
    return jnp.dot(a, b, preferred_element_type=F32)


def _swiglu(xb, wg_ref, wu_ref, wd_ref):
    g = _dot(xb, wg_ref[...])
    u = _dot(xb, wu_ref[...])
    a = g * jax.nn.sigmoid(g) * u
    return _dot(a.astype(BF16), wd_ref[...])


def _gelu(x):
    return 0.5 * x * (1.0 + lax.erf(x * math.sqrt(0.5)))


def _resident(shape):
    return pl.BlockSpec(shape, lambda *_: (0,) * len(shape), pipeline_mode=pl.Buffered(1))


def _rows(tm, width):
    return pl.BlockSpec((tm, width), lambda i: (i, 0))


def _residue_major_spec(d, tm, width, tiles_per_seq):
    return pl.BlockSpec((None, d, tm // d, width), lambda i: (i // tiles_per_seq, 0, i % tiles_per_seq, 0))


def _params(n_axes):
    return pltpu.CompilerParams(dimension_semantics=("arbitrary",) * n_axes,
                                vmem_limit_bytes=VMEM_LIMIT_BYTES)


_HBM = pl.BlockSpec(memory_space=pl.ANY)


FILL_SLOTS = 4


def _fill_bf16(w_hbm, w_vmem, stage, sem, rows):
    k, n = w_hbm.shape
    slots = stage.shape[0]
    assert k % rows == 0 and rows == stage.shape[1] and n <= stage.shape[2] and sem.shape[0] == slots
    n_blk = k // rows

    def copy(i):
        dst = stage.at[i % slots, :, pl.ds(0, n)]
        return pltpu.make_async_copy(w_hbm.at[pl.ds(i * rows, rows), :], dst, sem.at[i % slots])

    for i in range(min(slots - 1, n_blk)):
        copy(i).start()
    for i in range(n_blk):
        if i + slots - 1 < n_blk:
            copy(i + slots - 1).start()
        copy(i).wait()
        w_vmem[i * rows:(i + 1) * rows, :] = stage[i % slots, :, 0:n].astype(BF16)


def _weight_scratch(shapes, stage_rows, stage_cols):
    return ([pltpu.VMEM(s, BF16) for s in shapes]
            + [pltpu.VMEM((FILL_SLOTS, stage_rows, stage_cols), F32), pltpu.SemaphoreType.DMA((FILL_SLOTS,))])


BF16_SUBLANES = 16


def _side_cast_spec(shape, n_steps):
    k, n = shape
    n_blocks = n_steps
    while k % n_blocks or (k // n_blocks) % BF16_SUBLANES:
        n_blocks //= 2
    assert n_blocks >= 1 and n_steps % n_blocks == 0
    steps_per_block = n_steps // n_blocks
    return pl.BlockSpec((k // n_blocks, n), lambda i: (i // steps_per_block, 0))


def _side_cast(pairs, col_chunks=()):
    for idx, (src_ref, dst_ref) in enumerate(pairs):
        chunks = col_chunks[idx] if idx < len(col_chunks) and col_chunks[idx] else [(0, src_ref.shape[1])]
        dst0 = 0
        for src0, width in chunks:
            dst_ref[:, dst0:dst0 + width] = src_ref[:, src0:src0 + width].astype(BF16)
            dst0 += width


FFN_FILL_ROWS = 256


def _ffn_ln_kernel(x_ref, wg_hbm, wu_hbm, wd_hbm, g_ref, b_ref, *rest, n_side, side_cols):
    side_in, rest = rest[:n_side], rest[n_side:]
    h_ref, hb_ref = rest[:2]
    side_out, (wg_ref, wu_ref, wd_ref, stage_ref, sem) = rest[2:2 + n_side], rest[2 + n_side:]

    @pl.when(pl.program_id(0) == 0)
    def _():
        for w_hbm, w_ref in ((wg_hbm, wg_ref), (wu_hbm, wu_ref), (wd_hbm, wd_ref)):
            _fill_bf16(w_hbm, w_ref, stage_ref, sem, FFN_FILL_ROWS)

    x = x_ref[...]
    y = _swiglu(x.astype(BF16), wg_ref, wu_ref, wd_ref)
    h = _deepnorm(x, y, 0.5, g_ref[...], b_ref[...])
    h_ref[...] = h
    hb_ref[...] = pltpu.bitcast(h.astype(BF16), jnp.uint32)
    _side_cast(zip(side_in, side_out), side_cols)


def _ffn_ln(x, wg, wu, wd, g, b, next_weights, side_cols, tm):
    t = x.shape[0]
    n_steps = t // tm
    side_specs = [_side_cast_spec(w.shape, n_steps) for w in next_weights]
    return pl.pallas_call(
        functools.partial(_ffn_ln_kernel, n_side=len(next_weights), side_cols=side_cols),
        grid=(n_steps,),
        in_specs=[_rows(tm, D_MODEL), _HBM, _HBM, _HBM, _resident((1, D_MODEL)), _resident((1, D_MODEL))]
                 + side_specs,
        out_specs=[_rows(tm, D_MODEL), _rows(tm // 2, D_MODEL)] + side_specs,
        out_shape=[jax.ShapeDtypeStruct((t, D_MODEL), F32), jax.ShapeDtypeStruct((t // 2, D_MODEL), jnp.uint32)]
                  + [jax.ShapeDtypeStruct(w.shape, BF16) for w in next_weights],
        scratch_shapes=_weight_scratch([(D_MODEL, D_FF), (D_MODEL, D_FF), (D_FF, D_MODEL)], FFN_FILL_ROWS, D_FF),
        compiler_params=_params(1),
        name="ffn_ln",
    )(x, wg, wu, wd, g, b, *next_weights)


def _project_group(hb_ref, w_ref, gi, dst_ref, slot):
    width = 3 * GROUP_WIDTH
    res = _dot(pltpu.bitcast(hb_ref[...], BF16), w_ref[:, gi * width:(gi + 1) * width])
    for i in range(3):
        for j in range(N_SLAB):
            c0 = i * GROUP_WIDTH + j * LANES
            dst_ref[slot + i, j] = res[:, c0:c0 + LANES]


GATHER_STRIDE = 4


def _emit_residue_major(p_ref, out_ref, col0, tmp_ref, cos_ref=None, sin_ref=None):
    d, rows, _ = out_ref.shape
    tm = d * rows
    sources = [p_ref.at[j] for j in range(N_SLAB)] + ([cos_ref, sin_ref] if cos_ref is not None else [])
    two_level = d > GATHER_STRIDE
    if two_level:
        part = tm // GATHER_STRIDE
        for i, src in enumerate(sources):
            for a in range(GATHER_STRIDE):
                tmp_ref[i, a * part:(a + 1) * part, :] = src[pl.ds(a, part, stride=GATHER_STRIDE), :]
        sources = [tmp_ref.at[i] for i in range(len(sources))]
    lane = lax.broadcasted_iota(jnp.int32, (1, LANES), 1)
    first_half = (lane % HEAD_DIM) < (HEAD_DIM // 2)
    for r in range(d):
        if two_level:
            a, b = r % GATHER_STRIDE, r // GATHER_STRIDE
            rs = pl.ds(a * part + b, rows, stride=d // GATHER_STRIDE)
        else:
            rs = pl.ds(r, rows, stride=d) if d > 1 else slice(None)
        if cos_ref is not None:
            c, s = sources[N_SLAB][rs, :], sources[N_SLAB + 1][rs, :]
        blocks = []
        for j in range(N_SLAB):
            x = sources[j][rs, :]
            if cos_ref is not None:
                partner = jnp.where(first_half, pltpu.roll(x, LANES - HEAD_DIM // 2, 1),
                                    pltpu.roll(x, HEAD_DIM // 2, 1))
                x = x * c + partner * s
            blocks.append(x)
        out_ref[r, :, col0:col0 + GROUP_WIDTH] = jnp.concatenate(blocks, axis=1).astype(out_ref.dtype)


def _mixer_in_kernel(hb_ref, pos_ref, invf_ref, w_ref, bg_ref, lng_ref, lnb_ref, ws_ref, bs_ref, wgb_ref,
                     *rest, n_side):
    side_in, rest = rest[:n_side], rest[n_side:]
    qkv_refs, (ga_ref, gm_ref) = rest[:N_ATTN_GROUPS], rest[N_ATTN_GROUPS:N_ATTN_GROUPS + 2]
    side_out = rest[N_ATTN_GROUPS + 2:N_ATTN_GROUPS + 2 + n_side]
    a_ref, g_ref, p_ref, cs_ref, vgn_ref, tmp_ref = rest[N_ATTN_GROUPS + 2 + n_side:]
    tm = 2 * hb_ref.shape[0]
    _side_cast(zip(side_in, side_out))

    def project_group(gi, slot):
        _project_group(hb_ref, w_ref, gi, p_ref, slot)

    def emit_group(gi, slot):
        out_ref = qkv_refs[gi]
        _emit_residue_major(p_ref.at[slot], out_ref, 0, tmp_ref, cs_ref.at[2], cs_ref.at[3])
        _emit_residue_major(p_ref.at[slot + 1], out_ref, GROUP_WIDTH, tmp_ref, cs_ref.at[0], cs_ref.at[1])
        _emit_residue_major(p_ref.at[slot + 2], out_ref, 2 * GROUP_WIDTH, tmp_ref)

    uv = _dot(pltpu.bitcast(hb_ref[...], BF16), w_ref[:, _U0:_U0 + 2 * GMLP_WIDTH])
    a_ref[0] = uv[:, :GMLP_WIDTH]
    a_ref[1] = uv[:, GMLP_WIDTH:]
    n_freq = HEAD_DIM // 2
    per_row = LANES // n_freq
    ang = pos_ref[...].astype(F32) * invf_ref[...]
    lane = lax.broadcasted_iota(jnp.int32, (1, LANES), 1)
    first_half = (lane % HEAD_DIM) < n_freq
    lane_group = lane // n_freq
    for k, packed in enumerate((jnp.cos(ang), jnp.sin(ang))):
        rolled = [packed] + [pltpu.roll(packed, n_freq * s, 1) for s in range(1, per_row)]
        for a in range(per_row):
            spread = rolled[(0 - a) % per_row]
            for b in range(1, per_row):
                spread = jnp.where(lane_group == b, rolled[(b - a) % per_row], spread)
            if k == 1:
                spread = jnp.where(first_half, -spread, spread)
            rows = pl.ds(a, tm // per_row, stride=per_row)
            cs_ref[k, rows, :] = spread
            cs_ref[k + 2, rows, :] = spread * _Q_SCALE

    gates = _dot(pltpu.bitcast(hb_ref[...], BF16), w_ref[:, _GA0:_GA0 + 2 * D_MODEL]) + bg_ref[...]
    g_ref[1] = gates[:, :D_MODEL]
    g_ref[0] = gates[:, D_MODEL:]
    vgn_ref[...] = _layer_norm(_gelu(a_ref[1]), lng_ref[...], lnb_ref[...]).astype(BF16)

    project_group(0, 0)
    n_chunk = tm // GMLP_CHUNK
    gdim = GMLP_WIDTH // GMLP_GROUPS
    row = lax.broadcasted_iota(jnp.int32, (GMLP_CHUNK, GMLP_CHUNK), 0)
    col = lax.broadcasted_iota(jnp.int32, (GMLP_CHUNK, GMLP_CHUNK), 1)
    causal = col <= row
    for g in range(GMLP_GROUPS):
        ws = jnp.where(causal, ws_ref[g], 0.0).astype(BF16)
        rhs = jnp.concatenate(
            [vgn_ref[c * GMLP_CHUNK:(c + 1) * GMLP_CHUNK, g * gdim:(g + 1) * gdim] for c in range(n_chunk)],
            axis=1)
        mixed = _dot(ws, rhs)
        for c in range(n_chunk):
            a_ref[1, c * GMLP_CHUNK:(c + 1) * GMLP_CHUNK, g * gdim:(g + 1) * gdim] = (
                mixed[:, c * gdim:(c + 1) * gdim] + bs_ref[:, g * gdim:(g + 1) * gdim])
    a_ref[0] = _gelu(a_ref[0])

    project_group(1, 3)
    vgn_ref[...] = (a_ref[0] * a_ref[1]).astype(BF16)
    emit_group(0, 0)
    ga_ref[...] = jax.nn.sigmoid(g_ref[1]).astype(BF16)

    project_group(2, 0)
    emit_group(1, 3)
    g_ref[0] = jax.nn.sigmoid(g_ref[0])

    gm_ref[...] = (g_ref[0] * _dot(vgn_ref[...], wgb_ref[...])).astype(BF16)
    emit_group(2, 0)


def _mixer_in(hb, pos, invf, w_in, b_gates, ln_g, ln_b, w_s, b_s, w_gb, next_weights, batch, tm):
    t = 2 * hb.shape[0]
    seq = t // batch
    n_steps = t // tm
    in_width = w_in.shape[1]
    qkv_specs, qkv_shapes = [], []
    for _, d in ATTN_PATTERNS:
        qkv_specs += [_residue_major_spec(d, tm, 3 * GROUP_WIDTH, seq // tm)]
        qkv_shapes += [jax.ShapeDtypeStruct((batch, d, seq // d, 3 * GROUP_WIDTH), BF16)]
    side_specs = [_side_cast_spec(w.shape, n_steps) for w in next_weights]
    return pl.pallas_call(
        functools.partial(_mixer_in_kernel, n_side=len(next_weights)),
        grid=(n_steps,),
        in_specs=[_rows(tm // 2, D_MODEL), _rows(tm * (HEAD_DIM // 2) // LANES, LANES), _resident((1, LANES)),
                  _resident((D_MODEL, in_width)),
                  _resident((1, 2 * D_MODEL)), _resident((1, GMLP_WIDTH)), _resident((1, GMLP_WIDTH)),
                  _resident((GMLP_GROUPS, GMLP_CHUNK, GMLP_CHUNK)), _resident((GMLP_CHUNK, GMLP_WIDTH)),
                  _resident((GMLP_WIDTH, D_MODEL))] + side_specs,
        out_specs=qkv_specs + [_rows(tm, D_MODEL), _rows(tm, D_MODEL)] + side_specs,
        out_shape=qkv_shapes + [jax.ShapeDtypeStruct((t, D_MODEL), BF16)] * 2
                  + [jax.ShapeDtypeStruct(w.shape, BF16) for w in next_weights],
        scratch_shapes=[pltpu.VMEM((2, tm, GMLP_WIDTH), F32), pltpu.VMEM((2, tm, D_MODEL), F32),
                        pltpu.VMEM((6, N_SLAB, tm, LANES), F32), pltpu.VMEM((4, tm, LANES), F32),
                        pltpu.VMEM((tm, GMLP_WIDTH), BF16), pltpu.VMEM((N_SLAB + 2, tm, LANES), F32)],
        compiler_params=_params(1),
        name="mixer_in",
    )(hb, pos, invf, w_in, b_gates, ln_g, ln_b, w_s, b_s, w_gb, *next_weights)


def _attn_kernel(qkv_ref, acc_ref, stat_ref, *accf_ref, n_blk):
    n_items = qkv_ref.shape[0] // ATTN_BLOCK
    dilation = n_items // n_blk
    has_prev = n_blk > 1
    row = lax.broadcasted_iota(jnp.int32, (2 * ATTN_BLOCK, ATTN_BLOCK), 0) % ATTN_BLOCK
    col = lax.broadcasted_iota(jnp.int32, (2 * ATTN_BLOCK, ATTN_BLOCK), 1)
    neg_inf = jnp.float32(-jnp.inf)
    bias_cur = jnp.where(col <= row, 0.0, neg_inf)
    bias_prev = jnp.where(col >= row, 0.0, neg_inf)
    lane = lax.broadcasted_iota(jnp.int32, (ATTN_BLOCK, LANES), 1)
    low_head = lane < HEAD_DIM
    n_keys = 2 * ATTN_BLOCK if has_prev else ATTN_BLOCK
    low_head_keys = lax.broadcasted_iota(jnp.int32, (n_keys, LANES), 1) < HEAD_DIM
    contract_last = (((1,), (1,)), ((), ()))

    def item(j, carry):
        r0 = pl.multiple_of(j * ATTN_BLOCK, ATTN_BLOCK)
        if dilation == 1:
            out_rows = pl.ds(r0, ATTN_BLOCK)
        else:
            out_rows = pl.ds((j % n_blk) * (ATTN_BLOCK * dilation) + j // n_blk, ATTN_BLOCK, stride=dilation)
        if has_prev:
            p0 = pl.multiple_of(jnp.maximum(j - 1, 0) * ATTN_BLOCK, ATTN_BLOCK)
            prev_bias = bias_prev + jnp.where(j % n_blk > 0, 0.0, neg_inf)
        scores = []
        for pair in range(N_SLAB):
            q_sl = slice(pair * LANES, (pair + 1) * LANES)
            k_sl = slice(GROUP_WIDTH + pair * LANES, GROUP_WIDTH + (pair + 1) * LANES)
            q = qkv_ref[pl.ds(r0, ATTN_BLOCK), q_sl]
            zero = jnp.zeros_like(q)
            q2 = jnp.concatenate([jnp.where(low_head, q, zero), jnp.where(low_head, zero, q)], axis=0)
            keys = qkv_ref[pl.ds(r0, ATTN_BLOCK), k_sl]
            if has_prev:
                keys = jnp.concatenate([qkv_ref[pl.ds(p0, ATTN_BLOCK), k_sl], keys], axis=0)
            s = lax.dot_general(q2, keys, contract_last, preferred_element_type=F32)
            if has_prev:
                s = s + jnp.concatenate([prev_bias, bias_cur], axis=1)
            else:
                s = s + bias_cur
            scores.append(s)
        maxes = [jnp.max(s, axis=1, keepdims=True) for s in scores]
        probs = [jnp.exp2(s - m).astype(BF16) for s, m in zip(scores, maxes)]
        stat = jnp.zeros((ATTN_BLOCK, LANES), F32)
        for pair in range(N_SLAB):
            sl = slice(pair * LANES, (pair + 1) * LANES)
            v_sl = slice(2 * GROUP_WIDTH + pair * LANES, 2 * GROUP_WIDTH + (pair + 1) * LANES)
            vals = qkv_ref[pl.ds(r0, ATTN_BLOCK), v_sl]
            if has_prev:
                vals = jnp.concatenate([qkv_ref[pl.ds(p0, ATTN_BLOCK), v_sl], vals], axis=0)
            one = jnp.ones_like(vals)
            r_even = _dot(probs[pair][:ATTN_BLOCK], jnp.where(low_head_keys, vals, one))
            r_odd = _dot(probs[pair][ATTN_BLOCK:], jnp.where(low_head_keys, one, vals))
            numer = jnp.where(low_head, r_even, r_odd)
            if dilation == 1:
                acc_ref[out_rows, sl] = numer.astype(BF16)
            else:
                accf_ref[0][pair, out_rows, :] = numer
            m_even, m_odd = maxes[pair][:ATTN_BLOCK], maxes[pair][ATTN_BLOCK:]
            for h, m_h, r_h in ((2 * pair, m_even, r_even), (2 * pair + 1, m_odd, r_odd)):
                stat = jnp.where(lane == _stat_lane(h), m_h, stat)
                stat = jnp.where(lane == _stat_lane(h) + _DEN_SHIFT, r_h, stat)
        stat_ref[out_rows, :] = stat
        return carry

    lax.fori_loop(0, n_items, item, 0, unroll=n_items)
    if dilation > 1:
        acc_ref[...] = jnp.concatenate([accf_ref[0][pair] for pair in range(N_SLAB)], axis=1).astype(BF16)


def _attention(qkv):
    b, d, l, width = qkv.shape
    s = d * l

    def seq(w):
        return pl.BlockSpec((None, s, w), lambda bi: (bi, 0, 0))

    acc, stat = pl.pallas_call(
        functools.partial(_attn_kernel, n_blk=l // ATTN_BLOCK),
        grid=(b,),
        in_specs=[seq(width)],
        out_specs=[seq(GROUP_WIDTH), seq(LANES)],
        out_shape=[jax.ShapeDtypeStruct((b, s, GROUP_WIDTH), BF16), jax.ShapeDtypeStruct((b, s, LANES), F32)],
        scratch_shapes=[pltpu.VMEM((N_SLAB, s, LANES), F32)] if d > 1 else [],
        compiler_params=_params(1),
        name=f"band_attention_d{d}",
    )(qkv.reshape(b, s, width))
    return acc.reshape(b * s, GROUP_WIDTH), stat.reshape(b * s, LANES)


def _mixer_out_kernel(a0_ref, a1_ref, a2_ref, s0_ref, s1_ref, s2_ref, ga_ref, gm_ref, h1_ref, expand_ref,
                      wab_ref, wout_ref, g2_ref, b2_ref, wg_ref, wu_ref, wd_ref, g3_ref, b3_ref, out_ref):
    stats = [s_ref[...] for s_ref in (s0_ref, s1_ref, s2_ref)]
    accs = [a_ref[...].astype(F32) for a_ref in (a0_ref, a1_ref, a2_ref)]
    lane = lax.broadcasted_iota(jnp.int32, (1, LANES), 1)
    is_max_lane = functools.reduce(jnp.logical_or, [lane == _stat_lane(h) for h in range(HEADS_PER_GROUP)])
    m = jnp.maximum(jnp.maximum(stats[0], stats[1]), stats[2])
    es = [jnp.exp2(x - m) for x in stats]
    dens = [pltpu.roll(x, LANES - _DEN_SHIFT, 1) for x in stats]
    den = es[0] * dens[0] + es[1] * dens[1] + es[2] * dens[2]
    inv_den = 1.0 / jnp.where(is_max_lane, den, 1.0)
    y = None
    for e, acc in zip(es, accs):
        w = e * inv_den
        w_hi = w.astype(BF16)
        w_lo = (w - w_hi.astype(F32)).astype(BF16)
        w_wide = _dot(jnp.concatenate([w_hi, w_lo], axis=1), expand_ref[...])
        term = w_wide * acc
        y = term if y is None else y + term
    branch_a = _dot(y.astype(BF16), wab_ref[...])
    merged = ga_ref[...].astype(F32) * branch_a + gm_ref[...].astype(F32)
    mix = _dot(merged.astype(BF16), wout_ref[...])
    h2 = _deepnorm(h1_ref[...], mix, 1.0, g2_ref[...], b2_ref[...])
    ffn = _swiglu(h2.astype(BF16), wg_ref, wu_ref, wd_ref)
    out_ref[...] = _deepnorm(h2, ffn, 0.5, g3_ref[...], b3_ref[...])


def _mixer_out(accs, stats, ga, gm, h1, expand, wab, wout, g2, b2, wg, wu, wd, g3, b3, tm):
    t = h1.shape[0]
    return pl.pallas_call(
        _mixer_out_kernel,
        grid=(t // tm,),
        in_specs=[_rows(tm, GROUP_WIDTH)] * len(accs) + [_rows(tm, LANES)] * len(stats)
                 + [_rows(tm, D_MODEL)] * 3
                 + [_resident((2 * LANES, GROUP_WIDTH)), _resident((GROUP_WIDTH, D_MODEL)),
                    _resident((D_MODEL, D_MODEL)), _resident((1, D_MODEL)), _resident((1, D_MODEL)),
                    _resident((D_MODEL, D_FF)), _resident((D_MODEL, D_FF)), _resident((D_FF, D_MODEL)),
                    _resident((1, D_MODEL)), _resident((1, D_MODEL))],
        out_specs=_rows(tm, D_MODEL),
        out_shape=jax.ShapeDtypeStruct((t, D_MODEL), F32),
        compiler_params=_params(1),
        name="mixer_out_ffn",
    )(*accs, *stats, ga, gm, h1, expand, wab, wout, g2, b2, wg, wu, wd, g3, b3)


def kernel(x, positions, ffn1_w_gate, ffn1_w_up, ffn1_w_down, ln1_g, ln1_b, w_in, b_gates, gmlp_ln_g, gmlp_ln_b, gmlp_w_s, gmlp_b_s, w_attn_branch, w_gmlp_branch, w_out, ln2_g, ln2_b, ffn2_w_gate, ffn2_w_up, ffn2_w_down, ln3_g, ln3_b):
    b, s, d_model = x.shape
    assert d_model == D_MODEL and ln1_g.shape[0] == DEPTH == 1
    t = b * s
    tm = 512
    assert t % tm == 0 and tm % GMLP_CHUNK == 0 and s % tm == 0

    lane = jnp.arange(LANES) % (HEAD_DIM // 2)
    invf = (ROPE_THETA ** (-(2 * lane).astype(F32) / HEAD_DIM)).reshape(1, LANES)
    stat_lane_of_col = jnp.array([_stat_lane(c // HEAD_DIM) for c in range(GROUP_WIDTH)])
    expand = (jnp.arange(LANES)[:, None] == stat_lane_of_col[None, :]).astype(BF16)
    expand = jnp.concatenate([expand, expand], axis=0)
    b_s = jnp.repeat(gmlp_b_s[0].T, GMLP_WIDTH // GMLP_GROUPS, axis=1)

    h = x.reshape(t, D_MODEL)
    w_in_cols = [(base + gi * GROUP_WIDTH, GROUP_WIDTH) for gi in range(N_ATTN_GROUPS) for base in (_Q0, _K0, _V0)]
    w_in_cols.append((_U0, w_in.shape[-1] - _U0))
    h1, h1b, w_in_b, w_gb_b = _ffn_ln(h, ffn1_w_gate[0], ffn1_w_up[0], ffn1_w_down[0], ln1_g, ln1_b,
                                      (w_in[0], w_gmlp_branch[0]), (tuple(w_in_cols), None), tm)
    pos_packed = jnp.repeat(positions.reshape(t * (HEAD_DIM // 2) // LANES, -1), HEAD_DIM // 2, axis=1)
    outs = _mixer_in(h1b, pos_packed, invf, w_in_b, b_gates, gmlp_ln_g, gmlp_ln_b, gmlp_w_s[0], b_s, w_gb_b,
                     (w_attn_branch[0], w_out[0], ffn2_w_gate[0], ffn2_w_up[0], ffn2_w_down[0]), b, tm)
    ga, gm = outs[N_ATTN_GROUPS], outs[N_ATTN_GROUPS + 1]
    out_weights = outs[N_ATTN_GROUPS + 2:]
    accs, stats = [], []
    for gi in range(N_ATTN_GROUPS):
        acc, stat = _attention(outs[gi])
        accs.append(acc)
        stats.append(stat)
    wab, wout, wg2, wu2, wd2 = out_weights
    out = _mixer_out(accs, stats, ga, gm, h1, expand, wab, wout, ln2_g, ln2_b, wg2, wu2, wd2, ln3_g, ln3_b, tm)
    return out.reshape(b, s, D_MODEL)
```

```python
import functools
import math

import jax
import jax.numpy as jnp
from jax import lax
from jax.experimental import pallas as pl
from jax.experimental.pallas import tpu as pltpu

D_MODEL = 1024
HEAD_DIM = 64
HEADS_PER_GROUP = 8
ATTN_PATTERNS = ((128, 1), (512, 4), (2048, 16))
N_ATTN_GROUPS = len(ATTN_PATTERNS)
GROUP_WIDTH = HEADS_PER_GROUP * HEAD_DIM
ATTN_WIDTH = N_ATTN_GROUPS * GROUP_WIDTH
ATTN_BLOCK = 128
ROPE_THETA = 10000.0
GMLP_CHUNK = 128
GMLP_GROUPS = 8
GMLP_WIDTH = D_MODEL
D_FF = 2816
DEPTH = 1
ALPHA = (2 * DEPTH) ** 0.25
LN_EPS = 1e-5

LANES = 128
VMEM_LIMIT_BYTES = 60 * 1024 * 1024
N_SLAB = GROUP_WIDTH // LANES

_Q0, _K0, _V0 = 0, ATTN_WIDTH, 2 * ATTN_WIDTH
_U0 = 3 * ATTN_WIDTH
_VG0 = _U0 + GMLP_WIDTH
_GA0 = _VG0 + GMLP_WIDTH

_Q_SCALE = HEAD_DIM ** -0.5 * math.log2(math.e)

_DEN_SHIFT = 8


def _stat_lane(h):
    return HEAD_DIM + h if h % 2 == 0 else h


BF16 = jnp.bfloat16
F32 = jnp.float32


def _layer_norm(x, g, b, eps=LN_EPS):
    mu = jnp.mean(x, axis=-1, keepdims=True)
    xc = x - mu
    var = jnp.mean(xc * xc, axis=-1, keepdims=True)
    return xc * lax.rsqrt(var + eps) * g + b


def _deepnorm(x, update, scale, g, b):
    return _layer_norm(x + (scale / ALPHA) * update, g, b, LN_EPS / ALPHA ** 2)


def _dot(a, b):
    return jnp.dot(a, b, preferred_element_type=F32)


def _swiglu(xb, wg_ref, wu_ref, wd_ref):
    g = _dot(xb, wg_ref[...])
    u = _dot(xb, wu_ref[...])
    a = g * jax.nn.sigmoid(g) * u
    return _dot(a.astype(BF16), wd_ref[...])


def _gelu(x):
    return 0.5 * x * (1.0 + lax.erf(x * math.sqrt(0.5)))


def _resident(shape):
    return pl.BlockSpec(shape, lambda *_: (0,) * len(shape), pipeline_mode=pl.Buffered(1))


def _rows(tm, width):
    return pl.BlockSpec((tm, width), lambda i: (i, 0))


def _residue_major_spec(d, tm, width, tiles_per_seq):
    return pl.BlockSpec((None, d, tm // d, width), lambda i: (i // tiles_per_seq, 0, i % tiles_per_seq, 0))


def _params(n_axes):
    return pltpu.CompilerParams(dimension_semantics=("arbitrary",) * n_axes,
                                vmem_limit_bytes=VMEM_LIMIT_BYTES)


_HBM = pl.BlockSpec(memory_space=pl.ANY)


FILL_SLOTS = 4


def _fill_bf16(w_hbm, w_vmem, stage, sem, rows):
    k, n = w_hbm.shape
    slots = stage.shape[0]
    assert k % rows == 0 and rows == stage.shape[1] and n <= stage.shape[2] and sem.shape[0] == slots
    n_blk = k // rows

    def copy(i):
        dst = stage.at[i % slots, :, pl.ds(0, n)]
        return pltpu.make_async_copy(w_hbm.at[pl.ds(i * rows, rows), :], dst, sem.at[i % slots])

    for i in range(min(slots - 1, n_blk)):
        copy(i).start()
    for i in range(n_blk):
        if i + slots - 1 < n_blk:
            copy(i + slots - 1).start()
        copy(i).wait()
        w_vmem[i * rows:(i + 1) * rows, :] = stage[i % slots, :, 0:n].astype(BF16)


def _weight_scratch(shapes, stage_rows, stage_cols):
    return ([pltpu.VMEM(s, BF16) for s in shapes]
            + [pltpu.VMEM((FILL_SLOTS, stage_rows, stage_cols), F32), pltpu.SemaphoreType.DMA((FILL_SLOTS,))])


BF16_SUBLANES = 16


def _side_cast_spec(shape, n_steps):
    k, n = shape
    n_blocks = n_steps
    while k % n_blocks or (k // n_blocks) % BF16_SUBLANES:
        n_blocks //= 2
    assert n_blocks >= 1 and n_steps % n_blocks == 0
    steps_per_block = n_steps // n_blocks
    return pl.BlockSpec((k // n_blocks, n), lambda i: (i // steps_per_block, 0))


def _side_cast(pairs, col_chunks=()):
    for idx, (src_ref, dst_ref) in enumerate(pairs):
        chunks = col_chunks[idx] if idx < len(col_chunks) and col_chunks[idx] else [(0, src_ref.shape[1])]
        dst0 = 0
        for src0, width in chunks:
            dst_ref[:, dst0:dst0 + width] = src_ref[:, src0:src0 + width].astype(BF16)
            dst0 += width


FFN_FILL_ROWS = 256


def _ffn_ln_kernel(x_ref, wg_hbm, wu_hbm, wd_hbm, g_ref, b_ref, *rest, n_side, side_cols):
    side_in, rest = rest[:n_side], rest[n_side:]
    h_ref, hb_ref = rest[:2]
    side_out, (wg_ref, wu_ref, wd_ref, stage_ref, sem) = rest[2:2 + n_side], rest[2 + n_side:]

    @pl.when(pl.program_id(0) == 0)
    def _():
        for w_hbm, w_ref in ((wg_hbm, wg_ref), (wu_hbm, wu_ref), (wd_hbm, wd_ref)):
            _fill_bf16(w_hbm, w_ref, stage_ref, sem, FFN_FILL_ROWS)

    x = x_ref[...]
    y = _swiglu(x.astype(BF16), wg_ref, wu_ref, wd_ref)
    h = _deepnorm(x, y, 0.5, g_ref[...], b_ref[...])
    h_ref[...] = h
    hb_ref[...] = pltpu.bitcast(h.astype(BF16), jnp.uint32)
    _side_cast(zip(side_in, side_out), side_cols)


def _ffn_ln(x, wg, wu, wd, g, b, next_weights, side_cols, tm):
    t = x.shape[0]
    n_steps = t // tm
    side_specs = [_side_cast_spec(w.shape, n_steps) for w in next_weights]
    return pl.pallas_call(
        functools.partial(_ffn_ln_kernel, n_side=len(next_weights), side_cols=side_cols),
        grid=(n_steps,),
        in_specs=[_rows(tm, D_MODEL), _HBM, _HBM, _HBM, _resident((1, D_MODEL)), _resident((1, D_MODEL))]
                 + side_specs,
        out_specs=[_rows(tm, D_MODEL), _rows(tm // 2, D_MODEL)] + side_specs,
        out_shape=[jax.ShapeDtypeStruct((t, D_MODEL), F32), jax.ShapeDtypeStruct((t // 2, D_MODEL), jnp.uint32)]
                  + [jax.ShapeDtypeStruct(w.shape, BF16) for w in next_weights],
        scratch_shapes=_weight_scratch([(D_MODEL, D_FF), (D_MODEL, D_FF), (D_FF, D_MODEL)], FFN_FILL_ROWS, D_FF),
        compiler_params=_params(1),
        name="ffn_ln",
    )(x, wg, wu, wd, g, b, *next_weights)


def _project_group(hb_ref, w_ref, gi, dst_ref, slot):
    width = 3 * GROUP_WIDTH
    res = _dot(pltpu.bitcast(hb_ref[...], BF16), w_ref[:, gi * width:(gi + 1) * width])
    for i in range(3):
        for j in range(N_SLAB):
            c0 = i * GROUP_WIDTH + j * LANES
            dst_ref[slot + i, j] = res[:, c0:c0 + LANES]


GATHER_STRIDE = 4


def _emit_residue_major(p_ref, out_ref, col0, tmp_ref, cos_ref=None, sin_ref=None):
    d, rows, _ = out_ref.shape
    tm = d * rows
    sources = [p_ref.at[j] for j in range(N_SLAB)] + ([cos_ref, sin_ref] if cos_ref is not None else [])
    two_level = d > GATHER_STRIDE
    if two_level:
        part = tm // GATHER_STRIDE
        for i, src in enumerate(sources):
            for a in range(GATHER_STRIDE):
                tmp_ref[i, a * part:(a + 1) * part, :] = src[pl.ds(a, part, stride=GATHER_STRIDE), :]
        sources = [tmp_ref.at[i] for i in range(len(sources))]
    lane = lax.broadcasted_iota(jnp.int32, (1, LANES), 1)
    first_half = (lane % HEAD_DIM) < (HEAD_DIM // 2)
    for r in range(d):
        if two_level:
            a, b = r % GATHER_STRIDE, r // GATHER_STRIDE
            rs = pl.ds(a * part + b, rows, stride=d // GATHER_STRIDE)
        else:
            rs = pl.ds(r, rows, stride=d) if d > 1 else slice(None)
        if cos_ref is not None:
            c, s = sources[N_SLAB][rs, :], sources[N_SLAB + 1][rs, :]
        blocks = []
        for j in range(N_SLAB):
            x = sources[j][rs, :]
            if cos_ref is not None:
                partner = jnp.where(first_half, pltpu.roll(x, LANES - HEAD_DIM // 2, 1),
                                    pltpu.roll(x, HEAD_DIM // 2, 1))
                x = x * c + partner * s
            blocks.append(x)
        out_ref[r, :, col0:col0 + GROUP_WIDTH] = jnp.concatenate(blocks, axis=1).astype(out_ref.dtype)


def _mixer_in_kernel(hb_ref, pos_ref, invf_ref, w_ref, bg_ref, lng_ref, lnb_ref, ws_ref, bs_ref, wgb_ref,
                     *rest, n_side):
    side_in, rest = rest[:n_side], rest[n_side:]
    qkv_refs, (ga_ref, gm_ref) = rest[:N_ATTN_GROUPS], rest[N_ATTN_GROUPS:N_ATTN_GROUPS + 2]
    side_out = rest[N_ATTN_GROUPS + 2:N_ATTN_GROUPS + 2 + n_side]
    a_ref, g_ref, p_ref, cs_ref, vgn_ref, tmp_ref = rest[N_ATTN_GROUPS + 2 + n_side:]
    tm = 2 * hb_ref.shape[0]
    _side_cast(zip(side_in, side_out))

    def project_group(gi, slot):
        _project_group(hb_ref, w_ref, gi, p_ref, slot)

    def emit_group(gi, slot):
        out_ref = qkv_refs[gi]
        _emit_residue_major(p_ref.at[slot], out_ref, 0, tmp_ref, cs_ref.at[2], cs_ref.at[3])
        _emit_residue_major(p_ref.at[slot + 1], out_ref, GROUP_WIDTH, tmp_ref, cs_ref.at[0], cs_ref.at[1])
        _emit_residue_major(p_ref.at[slot + 2], out_ref, 2 * GROUP_WIDTH, tmp_ref)

    uv = _dot(pltpu.bitcast(hb_ref[...], BF16), w_ref[:, _U0:_U0 + 2 * GMLP_WIDTH])
    a_ref[0] = uv[:, :GMLP_WIDTH]
    a_ref[1] = uv[:, GMLP_WIDTH:]
    n_freq = HEAD_DIM // 2
    per_row = LANES // n_freq
    ang = pos_ref[...].astype(F32) * invf_ref[...]
    lane = lax.broadcasted_iota(jnp.int32, (1, LANES), 1)
    first_half = (lane % HEAD_DIM) < n_freq
    lane_group = lane // n_freq
    for k, packed in enumerate((jnp.cos(ang), jnp.sin(ang))):
        rolled = [packed] + [pltpu.roll(packed, n_freq * s, 1) for s in range(1, per_row)]
        for a in range(per_row):
            spread = rolled[(0 - a) % per_row]
            for b in range(1, per_row):
                spread = jnp.where(lane_group == b, rolled[(b - a) % per_row], spread)
            if k == 1:
                spread = jnp.where(first_half, -spread, spread)
            rows = pl.ds(a, tm // per_row, stride=per_row)
            cs_ref[k, rows, :] = spread
            cs_ref[k + 2, rows, :] = spread * _Q_SCALE

    gates = _dot(pltpu.bitcast(hb_ref[...], BF16), w_ref[:, _GA0:_GA0 + 2 * D_MODEL]) + bg_ref[...]
    g_ref[1] = gates[:, :D_MODEL]
    g_ref[0] = gates[:, D_MODEL:]
    vgn_ref[...] = _layer_norm(_gelu(a_ref[1]), lng_ref[...], lnb_ref[...]).astype(BF16)

    project_group(0, 0)
    n_chunk = tm // GMLP_CHUNK
    gdim = GMLP_WIDTH // GMLP_GROUPS
    row = lax.broadcasted_iota(jnp.int32, (GMLP_CHUNK, GMLP_CHUNK), 0)
    col = lax.broadcasted_iota(jnp.int32, (GMLP_CHUNK, GMLP_CHUNK), 1)
    causal = col <= row
    for g in range(GMLP_GROUPS):
        ws = jnp.where(causal, ws_ref[g], 0.0).astype(BF16)
        rhs = jnp.concatenate(
            [vgn_ref[c * GMLP_CHUNK:(c + 1) * GMLP_CHUNK, g * gdim:(g + 1) * gdim] for c in range(n_chunk)],
            axis=1)
        mixed = _dot(ws, rhs)
        for c in range(n_chunk):
            a_ref[1, c * GMLP_CHUNK:(c + 1) * GMLP_CHUNK, g * gdim:(g + 1) * gdim] = (
                mixed[:, c * gdim:(c + 1) * gdim] + bs_ref[:, g * gdim:(g + 1) * gdim])
    a_ref[0] = _gelu(a_ref[0])

    project_group(1, 3)
    vgn_ref[...] = (a_ref[0] * a_ref[1]).astype(BF16)
    emit_group(0, 0)
    ga_ref[...] = jax.nn.sigmoid(g_ref[1]).astype(BF16)

    project_group(2, 0)
    emit_group(1, 3)
    g_ref[0] = jax.nn.sigmoid(g_ref[0])

    gm_ref[...] = (g_ref[0] * _dot(vgn_ref[...], wgb_ref[...])).astype(BF16)
    emit_group(2, 0)


def _mixer_in(hb, pos, invf, w_in, b_gates, ln_g, ln_b, w_s, b_s, w_gb, next_weights, batch, tm):
    t = 2 * hb.shape[0]
    seq = t // batch
    n_steps = t // tm
    in_width = w_in.shape[1]
    qkv_specs, qkv_shapes = [], []
    for _, d in ATTN_PATTERNS:
        qkv_specs += [_residue_major_spec(d, tm, 3 * GROUP_WIDTH, seq // tm)]
        qkv_shapes += [jax.ShapeDtypeStruct((batch, d, seq // d, 3 * GROUP_WIDTH), BF16)]
    side_specs = [_side_cast_spec(w.shape, n_steps) for w in next_weights]
    return pl.pallas_call(
        functools.partial(_mixer_in_kernel, n_side=len(next_weights)),
        grid=(n_steps,),
        in_specs=[_rows(tm // 2, D_MODEL), _rows(tm * (HEAD_DIM // 2) // LANES, LANES), _resident((1, LANES)),
                  _resident((D_MODEL, in_width)),
                  _resident((1, 2 * D_MODEL)), _resident((1, GMLP_WIDTH)), _resident((1, GMLP_WIDTH)),
                  _resident((GMLP_GROUPS, GMLP_CHUNK, GMLP_CHUNK)), _resident((GMLP_CHUNK, GMLP_WIDTH)),
                  _resident((GMLP_WIDTH, D_MODEL))] + side_specs,
        out_specs=qkv_specs + [_rows(tm, D_MODEL), _rows(tm, D_MODEL)] + side_specs,
        out_shape=qkv_shapes + [jax.ShapeDtypeStruct((t, D_MODEL), BF16)] * 2
                  + [jax.ShapeDtypeStruct(w.shape, BF16) for w in next_weights],
        scratch_shapes=[pltpu.VMEM((2, tm, GMLP_WIDTH), F32), pltpu.VMEM((2, tm, D_MODEL), F32),
                        pltpu.VMEM((6, N_SLAB, tm, LANES), F32), pltpu.VMEM((4, tm, LANES), F32),
                        pltpu.VMEM((tm, GMLP_WIDTH), BF16), pltpu.VMEM((N_SLAB + 2, tm, LANES), F32)],
        compiler_params=_params(1),
        name="mixer_in",
    )(hb, pos, invf, w_in, b_gates, ln_g, ln_b, w_s, b_s, w_gb, *next_weights)


def _attn_kernel(qkv_ref, acc_ref, stat_ref, accf_ref=None, tmp_ref=None, *, n_blk):
    seq = qkv_ref.shape[0]
    n_items = seq // ATTN_BLOCK
    dilation = n_items // n_blk
    two_level = dilation > GATHER_STRIDE
    part = seq // GATHER_STRIDE
    has_prev = n_blk > 1
    row = lax.broadcasted_iota(jnp.int32, (2 * ATTN_BLOCK, ATTN_BLOCK), 0) % ATTN_BLOCK
    col = lax.broadcasted_iota(jnp.int32, (2 * ATTN_BLOCK, ATTN_BLOCK), 1)
    neg_inf = jnp.float32(-jnp.inf)
    bias_cur = jnp.where(col <= row, 0.0, neg_inf)
    bias_prev = jnp.where(col >= row, 0.0, neg_inf)
    lane = lax.broadcasted_iota(jnp.int32, (ATTN_BLOCK, LANES), 1)
    low_head = lane < HEAD_DIM
    n_keys = 2 * ATTN_BLOCK if has_prev else ATTN_BLOCK
    low_head_keys = lax.broadcasted_iota(jnp.int32, (n_keys, LANES), 1) < HEAD_DIM
    contract_last = (((1,), (1,)), ((), ()))

    def item(j, carry):
        r0 = pl.multiple_of(j * ATTN_BLOCK, ATTN_BLOCK)
        n, r = j % n_blk, j // n_blk
        if dilation == 1:
            out_rows = pl.ds(r0, ATTN_BLOCK)
        elif not two_level:
            out_rows = pl.ds(n * (ATTN_BLOCK * dilation) + r, ATTN_BLOCK, stride=dilation)
        else:
            sub = dilation // GATHER_STRIDE
            a, b = r % GATHER_STRIDE, r // GATHER_STRIDE
            out_rows = pl.ds(a * part + n * (ATTN_BLOCK * sub) + b, ATTN_BLOCK, stride=sub)
        if has_prev:
            p0 = pl.multiple_of(jnp.maximum(j - 1, 0) * ATTN_BLOCK, ATTN_BLOCK)
            prev_bias = bias_prev + jnp.where(j % n_blk > 0, 0.0, neg_inf)
        scores = []
        for pair in range(N_SLAB):
            q_sl = slice(pair * LANES, (pair + 1) * LANES)
            k_sl = slice(GROUP_WIDTH + pair * LANES, GROUP_WIDTH + (pair + 1) * LANES)
            q = qkv_ref[pl.ds(r0, ATTN_BLOCK), q_sl]
            zero = jnp.zeros_like(q)
            q2 = jnp.concatenate([jnp.where(low_head, q, zero), jnp.where(low_head, zero, q)], axis=0)
            keys = qkv_ref[pl.ds(r0, ATTN_BLOCK), k_sl]
            if has_prev:
                keys = jnp.concatenate([qkv_ref[pl.ds(p0, ATTN_BLOCK), k_sl], keys], axis=0)
            s = lax.dot_general(q2, keys, contract_last, preferred_element_type=F32)
            if has_prev:
                s = s + jnp.concatenate([prev_bias, bias_cur], axis=1)
            else:
                s = s + bias_cur
            scores.append(s)
        maxes = [jnp.max(s, axis=1, keepdims=True) for s in scores]
        probs = [jnp.exp2(s - m).astype(BF16) for s, m in zip(scores, maxes)]
        stat = jnp.zeros((ATTN_BLOCK, LANES), F32)
        for pair in range(N_SLAB):
            sl = slice(pair * LANES, (pair + 1) * LANES)
            v_sl = slice(2 * GROUP_WIDTH + pair * LANES, 2 * GROUP_WIDTH + (pair + 1) * LANES)
            vals = qkv_ref[pl.ds(r0, ATTN_BLOCK), v_sl]
            if has_prev:
                vals = jnp.concatenate([qkv_ref[pl.ds(p0, ATTN_BLOCK), v_sl], vals], axis=0)
            one = jnp.ones_like(vals)
            r_even = _dot(probs[pair][:ATTN_BLOCK], jnp.where(low_head_keys, vals, one))
            r_odd = _dot(probs[pair][ATTN_BLOCK:], jnp.where(low_head_keys, one, vals))
            numer = jnp.where(low_head, r_even, r_odd)
            if dilation == 1:
                acc_ref[out_rows, sl] = numer.astype(BF16)
            else:
                (tmp_ref if two_level else accf_ref)[pair, out_rows, :] = numer
            m_even, m_odd = maxes[pair][:ATTN_BLOCK], maxes[pair][ATTN_BLOCK:]
            for h, m_h, r_h in ((2 * pair, m_even, r_even), (2 * pair + 1, m_odd, r_odd)):
                stat = jnp.where(lane == _stat_lane(h), m_h, stat)
                stat = jnp.where(lane == _stat_lane(h) + _DEN_SHIFT, r_h, stat)
        if two_level:
            tmp_ref[N_SLAB, out_rows, :] = stat
        else:
            stat_ref[out_rows, :] = stat
        return carry

    lax.fori_loop(0, n_items, item, 0, unroll=n_items)
    if two_level:
        for a in range(GATHER_STRIDE):
            rows = pl.ds(a, part, stride=GATHER_STRIDE)
            for pair in range(N_SLAB):
                accf_ref[pair, rows, :] = tmp_ref[pair, a * part:(a + 1) * part, :]
            stat_ref[rows, :] = tmp_ref[N_SLAB, a * part:(a + 1) * part, :]
    if dilation > 1:
        acc_ref[...] = jnp.concatenate([accf_ref[pair] for pair in range(N_SLAB)], axis=1).astype(BF16)


def _attention(qkv):
    b, d, l, width = qkv.shape
    s = d * l

    def seq(w):
        return pl.BlockSpec((None, s, w), lambda bi: (bi, 0, 0))

    acc, stat = pl.pallas_call(
        functools.partial(_attn_kernel, n_blk=l // ATTN_BLOCK),
        grid=(b,),
        in_specs=[seq(width)],
        out_specs=[seq(GROUP_WIDTH), seq(LANES)],
        out_shape=[jax.ShapeDtypeStruct((b, s, GROUP_WIDTH), BF16), jax.ShapeDtypeStruct((b, s, LANES), F32)],
        scratch_shapes=([pltpu.VMEM((N_SLAB, s, LANES), F32)] if d > 1 else [])
                       + ([pltpu.VMEM((N_SLAB + 1, s, LANES), F32)] if d > GATHER_STRIDE else []),
        compiler_params=_params(1),
        name=f"band_attention_d{d}",
    )(qkv.reshape(b, s, width))
    return acc.reshape(b * s, GROUP_WIDTH), stat.reshape(b * s, LANES)


def _mixer_out_kernel(a0_ref, a1_ref, a2_ref, s0_ref, s1_ref, s2_ref, ga_ref, gm_ref, h1_ref, expand_ref,
                      wab_ref, wout_ref, g2_ref, b2_ref, wg_ref, wu_ref, wd_ref, g3_ref, b3_ref, out_ref):
    stats = [s_ref[...] for s_ref in (s0_ref, s1_ref, s2_ref)]
    accs = [a_ref[...].astype(F32) for a_ref in (a0_ref, a1_ref, a2_ref)]
    lane = lax.broadcasted_iota(jnp.int32, (1, LANES), 1)
    is_max_lane = functools.reduce(jnp.logical_or, [lane == _stat_lane(h) for h in range(HEADS_PER_GROUP)])
    m = jnp.maximum(jnp.maximum(stats[0], stats[1]), stats[2])
    es = [jnp.exp2(x - m) for x in stats]
    dens = [pltpu.roll(x, LANES - _DEN_SHIFT, 1) for x in stats]
    den = es[0] * dens[0] + es[1] * dens[1] + es[2] * dens[2]
    inv_den = 1.0 / jnp.where(is_max_lane, den, 1.0)
    y = None
    for e, acc in zip(es, accs):
        w = e * inv_den
        w_hi = w.astype(BF16)
        w_lo = (w - w_hi.astype(F32)).astype(BF16)
        w_wide = _dot(jnp.concatenate([w_hi, w_lo], axis=1), expand_ref[...])
        term = w_wide * acc
        y = term if y is None else y + term
    branch_a = _dot(y.astype(BF16), wab_ref[...])
    merged = ga_ref[...].astype(F32) * branch_a + gm_ref[...].astype(F32)
    mix = _dot(merged.astype(BF16), wout_ref[...])
    h2 = _deepnorm(h1_ref[...], mix, 1.0, g2_ref[...], b2_ref[...])
    ffn = _swiglu(h2.astype(BF16), wg_ref, wu_ref, wd_ref)
    out_ref[...] = _deepnorm(h2, ffn, 0.5, g3_ref[...], b3_ref[...])


def _mixer_out(accs, stats, ga, gm, h1, expand, wab, wout, g2, b2, wg, wu, wd, g3, b3, tm):
    t = h1.shape[0]
    return pl.pallas_call(
        _mixer_out_kernel,
        grid=(t // tm,),
        in_specs=[_rows(tm, GROUP_WIDTH)] * len(accs) + [_rows(tm, LANES)] * len(stats)
                 + [_rows(tm, D_MODEL)] * 3
                 + [_resident((2 * LANES, GROUP_WIDTH)), _resident((GROUP_WIDTH, D_MODEL)),
                    _resident((D_MODEL, D_MODEL)), _resident((1, D_MODEL)), _resident((1, D_MODEL)),
                    _resident((D_MODEL, D_FF)), _resident((D_MODEL, D_FF)), _resident((D_FF, D_MODEL)),
                    _resident((1, D_MODEL)), _resident((1, D_MODEL))],
        out_specs=_rows(tm, D_MODEL),
        out_shape=jax.ShapeDtypeStruct((t, D_MODEL), F32),
        compiler_params=_params(1),
        name="mixer_out_ffn",
    )(*accs, *stats, ga, gm, h1, expand, wab, wout, g2, b2, wg, wu, wd, g3, b3)


def kernel(x, positions, ffn1_w_gate, ffn1_w_up, ffn1_w_down, ln1_g, ln1_b, w_in, b_gates, gmlp_ln_g, gmlp_ln_b, gmlp_w_s, gmlp_b_s, w_attn_branch, w_gmlp_branch, w_out, ln2_g, ln2_b, ffn2_w_gate, ffn2_w_up, ffn2_w_down, ln3_g, ln3_b):
    b, s, d_model = x.shape
    assert d_model == D_MODEL and ln1_g.shape[0] == DEPTH == 1
    t = b * s
    tm = 512
    assert t % tm == 0 and tm % GMLP_CHUNK == 0 and s % tm == 0

    lane = jnp.arange(LANES) % (HEAD_DIM // 2)
    invf = (ROPE_THETA ** (-(2 * lane).astype(F32) / HEAD_DIM)).reshape(1, LANES)
    stat_lane_of_col = jnp.array([_stat_lane(c // HEAD_DIM) for c in range(GROUP_WIDTH)])
    expand = (jnp.arange(LANES)[:, None] == stat_lane_of_col[None, :]).astype(BF16)
    expand = jnp.concatenate([expand, expand], axis=0)
    b_s = jnp.repeat(gmlp_b_s[0].T, GMLP_WIDTH // GMLP_GROUPS, axis=1)

    h = x.reshape(t, D_MODEL)
    w_in_cols = [(base + gi * GROUP_WIDTH, GROUP_WIDTH) for gi in range(N_ATTN_GROUPS) for base in (_Q0, _K0, _V0)]
    w_in_cols.append((_U0, w_in.shape[-1] - _U0))
    h1, h1b, w_in_b, w_gb_b = _ffn_ln(h, ffn1_w_gate[0], ffn1_w_up[0], ffn1_w_down[0], ln1_g, ln1_b,
                                      (w_in[0], w_gmlp_branch[0]), (tuple(w_in_cols), None), tm)
    pos_packed = jnp.repeat(positions.reshape(t * (HEAD_DIM // 2) // LANES, -1), HEAD_DIM // 2, axis=1)
    outs = _mixer_in(h1b, pos_packed, invf, w_in_b, b_gates, gmlp_ln_g, gmlp_ln_b, gmlp_w_s[0], b_s, w_gb_b,
                     (w_attn_branch[0], w_out[0], ffn2_w_gate[0], ffn2_w_up[0], ffn2_w_down[0]), b, tm)
    ga, gm = outs[N_ATTN_GROUPS], outs[N_ATTN_GROUPS + 1]
    out_weights = outs[N_ATTN_GROUPS + 2:]
    accs, stats = [], []
    for gi in range(N_ATTN_GROUPS):
        acc, stat = _attention(outs[gi])
        accs.append(acc)
        stats.append(stat)
    wab, wout, wg2, wu2, wd2 = out_weights
    out = _mixer_out(accs, stats, ga, gm, h1, expand, wab, wout, ln2_g, ln2_b, wg2, wu2, wd2, ln3_g, ln3_b, tm)
    return out.reshape(b, s, D_MODEL)
```

```python
import functools
import math

import jax
import jax.numpy as jnp
from jax import lax
from jax.experimental import pallas as pl
from jax.experimental.pallas import tpu as pltpu

D_MODEL = 1024
HEAD_DIM = 64
HEADS_PER_GROUP = 8
ATTN_PATTERNS = ((128, 1), (512, 4), (2048, 16))
N_ATTN_GROUPS = len(ATTN_PATTERNS)
GROUP_WIDTH = HEADS_PER_GROUP * HEAD_DIM
ATTN_WIDTH = N_ATTN_GROUPS * GROUP_WIDTH
ATTN_BLOCK = 128
ROPE_THETA = 10000.0
GMLP_CHUNK = 128
GMLP_GROUPS = 8
GMLP_WIDTH = D_MODEL
D_FF = 2816
DEPTH = 1
ALPHA = (2 * DEPTH) ** 0.25
LN_EPS = 1e-5

LANES = 128
VMEM_LIMIT_BYTES = 60 * 1024 * 1024
N_SLAB = GROUP_WIDTH // LANES

_Q0, _K0, _V0 = 0, ATTN_WIDTH, 2 * ATTN_WIDTH
_U0 = 3 * ATTN_WIDTH
_VG0 = _U0 + GMLP_WIDTH
_GA0 = _VG0 + GMLP_WIDTH

_Q_SCALE = HEAD_DIM ** -0.5 * math.log2(math.e)

_DEN_SHIFT = 8


def _stat_lane(h):
    return HEAD_DIM + h if h % 2 == 0 else h


BF16 = jnp.bfloat16
F32 = jnp.float32


def _layer_norm(x, g, b, eps=LN_EPS):
    mu = jnp.mean(x, axis=-1, keepdims=True)
    xc = x - mu
    var = jnp.mean(xc * xc, axis=-1, keepdims=True)
    return xc * lax.rsqrt(var + eps) * g + b


def _deepnorm(x, update, scale, g, b):
    return _layer_norm(x + (scale / ALPHA) * update, g, b, LN_EPS / ALPHA ** 2)


def _dot(a, b):
    return jnp.dot(a, b, preferred_element_type=F32)


def _swiglu(xb, wg_ref, wu_ref, wd_ref):
    g = _dot(xb, wg_ref[...])
    u = _dot(xb, wu_ref[...])
    a = g * jax.nn.sigmoid(g) * u
    return _dot(a.astype(BF16), wd_ref[...])


def _gelu(x):
    return 0.5 * x * (1.0 + lax.erf(x * math.sqrt(0.5)))


def _resident(shape):
    return pl.BlockSpec(shape, lambda *_: (0,) * len(shape), pipeline_mode=pl.Buffered(1))


def _rows(tm, width):
    return pl.BlockSpec((tm, width), lambda i: (i, 0))


def _residue_major_spec(d, tm, width, tiles_per_seq):
    return pl.BlockSpec((None, d, tm // d, width), lambda i: (i // tiles_per_seq, 0, i % tiles_per_seq, 0))


def _params(n_axes):
    return pltpu.CompilerParams(dimension_semantics=("arbitrary",) * n_axes,
                                vmem_limit_bytes=VMEM_LIMIT_BYTES)


_HBM = pl.BlockSpec(memory_space=pl.ANY)


FILL_SLOTS = 8


def _fill_bf16(w_hbm, w_vmem, stage, sem, rows):
    k, n = w_hbm.shape
    slots = stage.shape[0]
    assert k % rows == 0 and rows == stage.shape[1] and n <= stage.shape[2] and sem.shape[0] == slots
    n_blk = k // rows

    def copy(i):
        dst = stage.at[i % slots, :, pl.ds(0, n)]
        return pltpu.make_async_copy(w_hbm.at[pl.ds(i * rows, rows), :], dst, sem.at[i % slots])

    for i in range(min(slots - 1, n_blk)):
        copy(i).start()
    for i in range(n_blk):
        if i + slots - 1 < n_blk:
            copy(i + slots - 1).start()
        copy(i).wait()
        w_vmem[i * rows:(i + 1) * rows, :] = stage[i % slots, :, 0:n].astype(BF16)


def _weight_scratch(shapes, stage_rows, stage_cols):
    return ([pltpu.VMEM(s, BF16) for s in shapes]
            + [pltpu.VMEM((FILL_SLOTS, stage_rows, stage_cols), F32), pltpu.SemaphoreType.DMA((FILL_SLOTS,))])


BF16_SUBLANES = 16


def _side_cast_spec(shape, n_steps):
    k, n = shape
    n_blocks = n_steps
    while k % n_blocks or (k // n_blocks) % BF16_SUBLANES:
        n_blocks //= 2
    assert n_blocks >= 1 and n_steps % n_blocks == 0
    steps_per_block = n_steps // n_blocks
    return pl.BlockSpec((k // n_blocks, n), lambda i: (i // steps_per_block, 0))


def _side_cast(pairs, col_chunks=()):
    for idx, (src_ref, dst_ref) in enumerate(pairs):
        chunks = col_chunks[idx] if idx < len(col_chunks) and col_chunks[idx] else [(0, src_ref.shape[1])]
        dst0 = 0
        for src0, width in chunks:
            dst_ref[:, dst0:dst0 + width] = src_ref[:, src0:src0 + width].astype(BF16)
            dst0 += width


FFN_FILL_ROWS = 128


def _ffn_ln_kernel(x_ref, wg_hbm, wu_hbm, wd_hbm, g_ref, b_ref, *rest, n_side, side_cols):
    side_in, rest = rest[:n_side], rest[n_side:]
    h_ref, hb_ref = rest[:2]
    side_out, (wg_ref, wu_ref, wd_ref, stage_ref, sem) = rest[2:2 + n_side], rest[2 + n_side:]

    @pl.when(pl.program_id(0) == 0)
    def _():
        for w_hbm, w_ref in ((wg_hbm, wg_ref), (wu_hbm, wu_ref), (wd_hbm, wd_ref)):
            _fill_bf16(w_hbm, w_ref, stage_ref, sem, FFN_FILL_ROWS)

    x = x_ref[...]
    y = _swiglu(x.astype(BF16), wg_ref, wu_ref, wd_ref)
    h = _deepnorm(x, y, 0.5, g_ref[...], b_ref[...])
    h_ref[...] = h
    hb_ref[...] = pltpu.bitcast(h.astype(BF16), jnp.uint32)
    _side_cast(zip(side_in, side_out), side_cols)


def _ffn_ln(x, wg, wu, wd, g, b, next_weights, side_cols, tm):
    t = x.shape[0]
    n_steps = t // tm
    side_specs = [_side_cast_spec(w.shape, n_steps) for w in next_weights]
    return pl.pallas_call(
        functools.partial(_ffn_ln_kernel, n_side=len(next_weights), side_cols=side_cols),
        grid=(n_steps,),
        in_specs=[_rows(tm, D_MODEL), _HBM, _HBM, _HBM, _resident((1, D_MODEL)), _resident((1, D_MODEL))]
                 + side_specs,
        out_specs=[_rows(tm, D_MODEL), _rows(tm // 2, D_MODEL)] + side_specs,
        out_shape=[jax.ShapeDtypeStruct((t, D_MODEL), F32), jax.ShapeDtypeStruct((t // 2, D_MODEL), jnp.uint32)]
                  + [jax.ShapeDtypeStruct(w.shape, BF16) for w in next_weights],
        scratch_shapes=_weight_scratch([(D_MODEL, D_FF), (D_MODEL, D_FF), (D_FF, D_MODEL)], FFN_FILL_ROWS, D_FF),
        compiler_params=_params(1),
        name="ffn_ln",
    )(x, wg, wu, wd, g, b, *next_weights)


def _project_group(hb_ref, w_ref, gi, dst_ref, slot):
    width = 3 * GROUP_WIDTH
    res = _dot(pltpu.bitcast(hb_ref[...], BF16), w_ref[:, gi * width:(gi + 1) * width])
    for i in range(3):
        for j in range(N_SLAB):
            c0 = i * GROUP_WIDTH + j * LANES
            dst_ref[slot + i, j] = res[:, c0:c0 + LANES]


GATHER_STRIDE = 4


def _emit_residue_major(p_ref, out_ref, col0, tmp_ref, cos_ref=None, sin_ref=None):
    d, rows, _ = out_ref.shape
    tm = d * rows
    sources = [p_ref.at[j] for j in range(N_SLAB)] + ([cos_ref, sin_ref] if cos_ref is not None else [])
    two_level = d > GATHER_STRIDE
    if two_level:
        part = tm // GATHER_STRIDE
        for i, src in enumerate(sources):
            for a in range(GATHER_STRIDE):
                tmp_ref[i, a * part:(a + 1) * part, :] = src[pl.ds(a, part, stride=GATHER_STRIDE), :]
        sources = [tmp_ref.at[i] for i in range(len(sources))]
    lane = lax.broadcasted_iota(jnp.int32, (1, LANES), 1)
    first_half = (lane % HEAD_DIM) < (HEAD_DIM // 2)
    for r in range(d):
        if two_level:
            a, b = r % GATHER_STRIDE, r // GATHER_STRIDE
            rs = pl.ds(a * part + b, rows, stride=d // GATHER_STRIDE)
        else:
            rs = pl.ds(r, rows, stride=d) if d > 1 else slice(None)
        if cos_ref is not None:
            c, s = sources[N_SLAB][rs, :], sources[N_SLAB + 1][rs, :]
        blocks = []
        for j in range(N_SLAB):
            x = sources[j][rs, :]
            if cos_ref is not None:
                partner = jnp.where(first_half, pltpu.roll(x, LANES - HEAD_DIM // 2, 1),
                                    pltpu.roll(x, HEAD_DIM // 2, 1))
                x = x * c + partner * s
            blocks.append(x)
        out_ref[r, :, col0:col0 + GROUP_WIDTH] = jnp.concatenate(blocks, axis=1).astype(out_ref.dtype)


def _mixer_in_kernel(hb_ref, pos_ref, invf_ref, w_ref, bg_ref, lng_ref, lnb_ref, ws_ref, bs_ref, wgb_ref,
                     *rest, n_side):
    side_in, rest = rest[:n_side], rest[n_side:]
    qkv_refs, (ga_ref, gm_ref) = rest[:N_ATTN_GROUPS], rest[N_ATTN_GROUPS:N_ATTN_GROUPS + 2]
    side_out = rest[N_ATTN_GROUPS + 2:N_ATTN_GROUPS + 2 + n_side]
    a_ref, g_ref, p_ref, cs_ref, vgn_ref, tmp_ref = rest[N_ATTN_GROUPS + 2 + n_side:]
    tm = 2 * hb_ref.shape[0]
    _side_cast(zip(side_in, side_out))

    def project_group(gi, slot):
        _project_group(hb_ref, w_ref, gi, p_ref, slot)

    def emit_group(gi, slot):
        out_ref = qkv_refs[gi]
        _emit_residue_major(p_ref.at[slot], out_ref, 0, tmp_ref, cs_ref.at[2], cs_ref.at[3])
        _emit_residue_major(p_ref.at[slot + 1], out_ref, GROUP_WIDTH, tmp_ref, cs_ref.at[0], cs_ref.at[1])
        _emit_residue_major(p_ref.at[slot + 2], out_ref, 2 * GROUP_WIDTH, tmp_ref)

    uv = _dot(pltpu.bitcast(hb_ref[...], BF16), w_ref[:, _U0:_U0 + 2 * GMLP_WIDTH])
    a_ref[0] = uv[:, :GMLP_WIDTH]
    a_ref[1] = uv[:, GMLP_WIDTH:]
    n_freq = HEAD_DIM // 2
    per_row = LANES // n_freq
    ang = pos_ref[...].astype(F32) * invf_ref[...]
    lane = lax.broadcasted_iota(jnp.int32, (1, LANES), 1)
    first_half = (lane % HEAD_DIM) < n_freq
    lane_group = lane // n_freq
    for k, packed in enumerate((jnp.cos(ang), jnp.sin(ang))):
        rolled = [packed] + [pltpu.roll(packed, n_freq * s, 1) for s in range(1, per_row)]
        for a in range(per_row):
            spread = rolled[(0 - a) % per_row]
            for b in range(1, per_row):
                spread = jnp.where(lane_group == b, rolled[(b - a) % per_row], spread)
            if k == 1:
                spread = jnp.where(first_half, -spread, spread)
            rows = pl.ds(a, tm // per_row, stride=per_row)
            cs_ref[k, rows, :] = spread
            cs_ref[k + 2, rows, :] = spread * _Q_SCALE

    gates = _dot(pltpu.bitcast(hb_ref[...], BF16), w_ref[:, _GA0:_GA0 + 2 * D_MODEL]) + bg_ref[...]
    g_ref[1] = gates[:, :D_MODEL]
    g_ref[0] = gates[:, D_MODEL:]
    vgn_ref[...] = _layer_norm(_gelu(a_ref[1]), lng_ref[...], lnb_ref[...]).astype(BF16)

    project_group(0, 0)
    n_chunk = tm // GMLP_CHUNK
    gdim = GMLP_WIDTH // GMLP_GROUPS
    row = lax.broadcasted_iota(jnp.int32, (GMLP_CHUNK, GMLP_CHUNK), 0)
    col = lax.broadcasted_iota(jnp.int32, (GMLP_CHUNK, GMLP_CHUNK), 1)
    causal = col <= row
    for g in range(GMLP_GROUPS):
        ws = jnp.where(causal, ws_ref[g], 0.0).astype(BF16)
        rhs = jnp.concatenate(
            [vgn_ref[c * GMLP_CHUNK:(c + 1) * GMLP_CHUNK, g * gdim:(g + 1) * gdim] for c in range(n_chunk)],
            axis=1)
        mixed = _dot(ws, rhs)
        for c in range(n_chunk):
            a_ref[1, c * GMLP_CHUNK:(c + 1) * GMLP_CHUNK, g * gdim:(g + 1) * gdim] = (
                mixed[:, c * gdim:(c + 1) * gdim] + bs_ref[:, g * gdim:(g + 1) * gdim])
    a_ref[0] = _gelu(a_ref[0])

    project_group(1, 3)
    vgn_ref[...] = (a_ref[0] * a_ref[1]).astype(BF16)
    emit_group(0, 0)
    ga_ref[...] = jax.nn.sigmoid(g_ref[1]).astype(BF16)

    project_group(2, 0)
    emit_group(1, 3)
    g_ref[0] = jax.nn.sigmoid(g_ref[0])

    gm_ref[...] = (g_ref[0] * _dot(vgn_ref[...], wgb_ref[...])).astype(BF16)
    emit_group(2, 0)


def _mixer_in(hb, pos, invf, w_in, b_gates, ln_g, ln_b, w_s, b_s, w_gb, next_weights, batch, tm):
    t = 2 * hb.shape[0]
    seq = t // batch
    n_steps = t // tm
    in_width = w_in.shape[1]
    qkv_specs, qkv_shapes = [], []
    for _, d in ATTN_PATTERNS:
        qkv_specs += [_residue_major_spec(d, tm, 3 * GROUP_WIDTH, seq // tm)]
        qkv_shapes += [jax.ShapeDtypeStruct((batch, d, seq // d, 3 * GROUP_WIDTH), BF16)]
    side_specs = [_side_cast_spec(w.shape, n_steps) for w in next_weights]
    return pl.pallas_call(
        functools.partial(_mixer_in_kernel, n_side=len(next_weights)),
        grid=(n_steps,),
        in_specs=[_rows(tm // 2, D_MODEL), _rows(tm * (HEAD_DIM // 2) // LANES, LANES), _resident((1, LANES)),
                  _resident((D_MODEL, in_width)),
                  _resident((1, 2 * D_MODEL)), _resident((1, GMLP_WIDTH)), _resident((1, GMLP_WIDTH)),
                  _resident((GMLP_GROUPS, GMLP_CHUNK, GMLP_CHUNK)), _resident((GMLP_CHUNK, GMLP_WIDTH)),
                  _resident((GMLP_WIDTH, D_MODEL))] + side_specs,
        out_specs=qkv_specs + [_rows(tm, D_MODEL), _rows(tm, D_MODEL)] + side_specs,
        out_shape=qkv_shapes + [jax.ShapeDtypeStruct((t, D_MODEL), BF16)] * 2
                  + [jax.ShapeDtypeStruct(w.shape, BF16) for w in next_weights],
        scratch_shapes=[pltpu.VMEM((2, tm, GMLP_WIDTH), F32), pltpu.VMEM((2, tm, D_MODEL), F32),
                        pltpu.VMEM((6, N_SLAB, tm, LANES), F32), pltpu.VMEM((4, tm, LANES), F32),
                        pltpu.VMEM((tm, GMLP_WIDTH), BF16), pltpu.VMEM((N_SLAB + 2, tm, LANES), F32)],
        compiler_params=_params(1),
        name="mixer_in",
    )(hb, pos, invf, w_in, b_gates, ln_g, ln_b, w_s, b_s, w_gb, *next_weights)


def _attn_kernel(qkv_ref, acc_ref, stat_ref, accf_ref=None, tmp_ref=None, *, n_blk):
    seq = qkv_ref.shape[0]
    n_items = seq // ATTN_BLOCK
    dilation = n_items // n_blk
    two_level = dilation > GATHER_STRIDE
    part = seq // GATHER_STRIDE
    has_prev = n_blk > 1
    row = lax.broadcasted_iota(jnp.int32, (2 * ATTN_BLOCK, ATTN_BLOCK), 0) % ATTN_BLOCK
    col = lax.broadcasted_iota(jnp.int32, (2 * ATTN_BLOCK, ATTN_BLOCK), 1)
    neg_inf = jnp.float32(-jnp.inf)
    bias_cur = jnp.where(col <= row, 0.0, neg_inf)
    bias_prev = jnp.where(col >= row, 0.0, neg_inf)
    lane = lax.broadcasted_iota(jnp.int32, (ATTN_BLOCK, LANES), 1)
    low_head = lane < HEAD_DIM
    n_keys = 2 * ATTN_BLOCK if has_prev else ATTN_BLOCK
    low_head_keys = lax.broadcasted_iota(jnp.int32, (n_keys, LANES), 1) < HEAD_DIM
    contract_last = (((1,), (1,)), ((), ()))

    def item(j, carry):
        r0 = pl.multiple_of(j * ATTN_BLOCK, ATTN_BLOCK)
        n, r = j % n_blk, j // n_blk
        if dilation == 1:
            out_rows = pl.ds(r0, ATTN_BLOCK)
        elif not two_level:
            out_rows = pl.ds(n * (ATTN_BLOCK * dilation) + r, ATTN_BLOCK, stride=dilation)
        else:
            sub = dilation // GATHER_STRIDE
            a, b = r % GATHER_STRIDE, r // GATHER_STRIDE
            out_rows = pl.ds(a * part + n * (ATTN_BLOCK * sub) + b, ATTN_BLOCK, stride=sub)
        if has_prev:
            p0 = pl.multiple_of(jnp.maximum(j - 1, 0) * ATTN_BLOCK, ATTN_BLOCK)
            prev_bias = bias_prev + jnp.where(j % n_blk > 0, 0.0, neg_inf)
        scores = []
        for pair in range(N_SLAB):
            q_sl = slice(pair * LANES, (pair + 1) * LANES)
            k_sl = slice(GROUP_WIDTH + pair * LANES, GROUP_WIDTH + (pair + 1) * LANES)
            q = qkv_ref[pl.ds(r0, ATTN_BLOCK), q_sl]
            zero = jnp.zeros_like(q)
            q2 = jnp.concatenate([jnp.where(low_head, q, zero), jnp.where(low_head, zero, q)], axis=0)
            keys = qkv_ref[pl.ds(r0, ATTN_BLOCK), k_sl]
            if has_prev:
                keys = jnp.concatenate([qkv_ref[pl.ds(p0, ATTN_BLOCK), k_sl], keys], axis=0)
            s = lax.dot_general(q2, keys, contract_last, preferred_element_type=F32)
            if has_prev:
                s = s + jnp.concatenate([prev_bias, bias_cur], axis=1)
            else:
                s = s + bias_cur
            scores.append(s)
        maxes = [jnp.max(s, axis=1, keepdims=True) for s in scores]
        probs = [jnp.exp2(s - m).astype(BF16) for s, m in zip(scores, maxes)]
        stat = jnp.zeros((ATTN_BLOCK, LANES), F32)
        for pair in range(N_SLAB):
            sl = slice(pair * LANES, (pair + 1) * LANES)
            v_sl = slice(2 * GROUP_WIDTH + pair * LANES, 2 * GROUP_WIDTH + (pair + 1) * LANES)
            vals = qkv_ref[pl.ds(r0, ATTN_BLOCK), v_sl]
            if has_prev:
                vals = jnp.concatenate([qkv_ref[pl.ds(p0, ATTN_BLOCK), v_sl], vals], axis=0)
            one = jnp.ones_like(vals)
            r_even = _dot(probs[pair][:ATTN_BLOCK], jnp.where(low_head_keys, vals, one))
            r_odd = _dot(probs[pair][ATTN_BLOCK:], jnp.where(low_head_keys, one, vals))
            numer = jnp.where(low_head, r_even, r_odd)
            if dilation == 1:
                acc_ref[out_rows, sl] = numer.astype(BF16)
            else:
                (tmp_ref if two_level else accf_ref)[pair, out_rows, :] = numer
            m_even, m_odd = maxes[pair][:ATTN_BLOCK], maxes[pair][ATTN_BLOCK:]
            for h, m_h, r_h in ((2 * pair, m_even, r_even), (2 * pair + 1, m_odd, r_odd)):
                stat = jnp.where(lane == _stat_lane(h), m_h, stat)
                stat = jnp.where(lane == _stat_lane(h) + _DEN_SHIFT, r_h, stat)
        if two_level:
            tmp_ref[N_SLAB, out_rows, :] = stat
        else:
            stat_ref[out_rows, :] = stat
        return carry

    lax.fori_loop(0, n_items, item, 0, unroll=n_items)
    if two_level:
        for a in range(GATHER_STRIDE):
            rows = pl.ds(a, part, stride=GATHER_STRIDE)
            for pair in range(N_SLAB):
                accf_ref[pair, rows, :] = tmp_ref[pair, a * part:(a + 1) * part, :]
            stat_ref[rows, :] = tmp_ref[N_SLAB, a * part:(a + 1) * part, :]
    if dilation > 1:
        acc_ref[...] = jnp.concatenate([accf_ref[pair] for pair in range(N_SLAB)], axis=1).astype(BF16)


def _attention(qkv):
    b, d, l, width = qkv.shape
    s = d * l

    def seq(w):
        return pl.BlockSpec((None, s, w), lambda bi: (bi, 0, 0))

    acc, stat = pl.pallas_call(
        functools.partial(_attn_kernel, n_blk=l // ATTN_BLOCK),
        grid=(b,),
        in_specs=[seq(width)],
        out_specs=[seq(GROUP_WIDTH), seq(LANES)],
        out_shape=[jax.ShapeDtypeStruct((b, s, GROUP_WIDTH), BF16), jax.ShapeDtypeStruct((b, s, LANES), F32)],
        scratch_shapes=([pltpu.VMEM((N_SLAB, s, LANES), F32)] if d > 1 else [])
                       + ([pltpu.VMEM((N_SLAB + 1, s, LANES), F32)] if d > GATHER_STRIDE else []),
        compiler_params=_params(1),
        name=f"band_attention_d{d}",
    )(qkv.reshape(b, s, width))
    return acc.reshape(b * s, GROUP_WIDTH), stat.reshape(b * s, LANES)


def _mixer_out_kernel(a0_ref, a1_ref, a2_ref, s0_ref, s1_ref, s2_ref, ga_ref, gm_ref, h1_ref, expand_ref,
                      wab_ref, wout_ref, g2_ref, b2_ref, wg_ref, wu_ref, wd_ref, g3_ref, b3_ref, out_ref):
    stats = [s_ref[...] for s_ref in (s0_ref, s1_ref, s2_ref)]
    accs = [a_ref[...].astype(F32) for a_ref in (a0_ref, a1_ref, a2_ref)]
    lane = lax.broadcasted_iota(jnp.int32, (1, LANES), 1)
    is_max_lane = functools.reduce(jnp.logical_or, [lane == _stat_lane(h) for h in range(HEADS_PER_GROUP)])
    m = jnp.maximum(jnp.maximum(stats[0], stats[1]), stats[2])
    es = [jnp.exp2(x - m) for x in stats]
    dens = [pltpu.roll(x, LANES - _DEN_SHIFT, 1) for x in stats]
    den = es[0] * dens[0] + es[1] * dens[1] + es[2] * dens[2]
    inv_den = 1.0 / jnp.where(is_max_lane, den, 1.0)
    y = None
    for e, acc in zip(es, accs):
        w = e * inv_den
        w_hi = w.astype(BF16)
        w_lo = (w - w_hi.astype(F32)).astype(BF16)
        w_wide = _dot(jnp.concatenate([w_hi, w_lo], axis=1), expand_ref[...])
        term = w_wide * acc
        y = term if y is None else y + term
    branch_a = _dot(y.astype(BF16), wab_ref[...])
    merged = ga_ref[...].astype(F32) * branch_a + gm_ref[...].astype(F32)
    mix = _dot(merged.astype(BF16), wout_ref[...])
    h2 = _deepnorm(h1_ref[...], mix, 1.0, g2_ref[...], b2_ref[...])
    ffn = _swiglu(h2.astype(BF16), wg_ref, wu_ref, wd_ref)
    out_ref[...] = _deepnorm(h2, ffn, 0.5, g3_ref[...], b3_ref[...])


def _mixer_out(accs, stats, ga, gm, h1, expand, wab, wout, g2, b2, wg, wu, wd, g3, b3, tm):
    t = h1.shape[0]
    return pl.pallas_call(
        _mixer_out_kernel,
        grid=(t // tm,),
        in_specs=[_rows(tm, GROUP_WIDTH)] * len(accs) + [_rows(tm, LANES)] * len(stats)
                 + [_rows(tm, D_MODEL)] * 3
                 + [_resident((2 * LANES, GROUP_WIDTH)), _resident((GROUP_WIDTH, D_MODEL)),
                    _resident((D_MODEL, D_MODEL)), _resident((1, D_MODEL)), _resident((1, D_MODEL)),
                    _resident((D_MODEL, D_FF)), _resident((D_MODEL, D_FF)), _resident((D_FF, D_MODEL)),
                    _resident((1, D_MODEL)), _resident((1, D_MODEL))],
        out_specs=_rows(tm, D_MODEL),
        out_shape=jax.ShapeDtypeStruct((t, D_MODEL), F32),
        compiler_params=_params(1),
        name="mixer_out_ffn",
    )(*accs, *stats, ga, gm, h1, expand, wab, wout, g2, b2, wg, wu, wd, g3, b3)


def kernel(x, positions, ffn1_w_gate, ffn1_w_up, ffn1_w_down, ln1_g, ln1_b, w_in, b_gates, gmlp_ln_g, gmlp_ln_b, gmlp_w_s, gmlp_b_s, w_attn_branch, w_gmlp_branch, w_out, ln2_g, ln2_b, ffn2_w_gate, ffn2_w_up, ffn2_w_down, ln3_g, ln3_b):
    b, s, d_model = x.shape
    assert d_model == D_MODEL and ln1_g.shape[0] == DEPTH == 1
    t = b * s
    tm = 512
    assert t % tm == 0 and tm % GMLP_CHUNK == 0 and s % tm == 0

    lane = jnp.arange(LANES) % (HEAD_DIM // 2)
    invf = (ROPE_THETA ** (-(2 * lane).astype(F32) / HEAD_DIM)).reshape(1, LANES)
    stat_lane_of_col = jnp.array([_stat_lane(c // HEAD_DIM) for c in range(GROUP_WIDTH)])
    expand = (jnp.arange(LANES)[:, None] == stat_lane_of_col[None, :]).astype(BF16)
    expand = jnp.concatenate([expand, expand], axis=0)
    b_s = jnp.repeat(gmlp_b_s[0].T, GMLP_WIDTH // GMLP_GROUPS, axis=1)

    h = x.reshape(t, D_MODEL)
    w_in_cols = [(base + gi * GROUP_WIDTH, GROUP_WIDTH) for gi in range(N_ATTN_GROUPS) for base in (_Q0, _K0, _V0)]
    w_in_cols.append((_U0, w_in.shape[-1] - _U0))
    h1, h1b, w_in_b, w_gb_b = _ffn_ln(h, ffn1_w_gate[0], ffn1_w_up[0], ffn1_w_down[0], ln1_g, ln1_b,
                                      (w_in[0], w_gmlp_branch[0]), (tuple(w_in_cols), None), tm)
    pos_packed = jnp.repeat(positions.reshape(t * (HEAD_DIM // 2) // LANES, -1), HEAD_DIM // 2, axis=1)
    outs = _mixer_in(h1b, pos_packed, invf, w_in_b, b_gates, gmlp_ln_g, gmlp_ln_b, gmlp_w_s[0], b_s, w_gb_b,
                     (w_attn_branch[0], w_out[0], ffn2_w_gate[0], ffn2_w_up[0], ffn2_w_down[0]), b, tm)
    ga, gm = outs[N_ATTN_GROUPS], outs[N_ATTN_GROUPS + 1]
    out_weights = outs[N_ATTN_GROUPS + 2:]
    accs, stats = [], []
    for gi in range(N_ATTN_GROUPS):
        acc, stat = _attention(outs[gi])
        accs.append(acc)
        stats.append(stat)
    wab, wout, wg2, wu2, wd2 = out_weights
    out = _mixer_out(accs, stats, ga, gm, h1, expand, wab, wout, ln2_g, ln2_b, wg2, wu2, wd2, ln3_g, ln3_b, tm)
    return out.reshape(b, s, D_MODEL)
```

```python
import functools
import math

import jax
import jax.numpy as jnp
from jax import lax
from jax.experimental import pallas as pl
from jax.experimental.pallas import tpu as pltpu

D_MODEL = 1024
HEAD_DIM = 64
HEADS_PER_GROUP = 8
ATTN_PATTERNS = ((128, 1), (512, 4), (2048, 16))
N_ATTN_GROUPS = len(ATTN_PATTERNS)
GROUP_WIDTH = HEADS_PER_GROUP * HEAD_DIM
ATTN_WIDTH = N_ATTN_GROUPS * GROUP_WIDTH
ATTN_BLOCK = 128
ROPE_THETA = 10000.0
GMLP_CHUNK = 128
GMLP_GROUPS = 8
GMLP_WIDTH = D_MODEL
D_FF = 2816
DEPTH = 1
ALPHA = (2 * DEPTH) ** 0.25
LN_EPS = 1e-5

LANES = 128
VMEM_LIMIT_BYTES = 60 * 1024 * 1024
N_SLAB = GROUP_WIDTH // LANES

_Q0, _K0, _V0 = 0, ATTN_WIDTH, 2 * ATTN_WIDTH
_U0 = 3 * ATTN_WIDTH
_VG0 = _U0 + GMLP_WIDTH
_GA0 = _VG0 + GMLP_WIDTH

_Q_SCALE = HEAD_DIM ** -0.5 * math.log2(math.e)

_DEN_SHIFT = 8


def _stat_lane(h):
    return HEAD_DIM + h if h % 2 == 0 else h


BF16 = jnp.bfloat16
F32 = jnp.float32


def _layer_norm(x, g, b, eps=LN_EPS):
    mu = jnp.mean(x, axis=-1, keepdims=True)
    xc = x - mu
    var = jnp.mean(xc * xc, axis=-1, keepdims=True)
    return xc * lax.rsqrt(var + eps) * g + b


def _deepnorm(x, update, scale, g, b):
    return _layer_norm(x + (scale / ALPHA) * update, g, b, LN_EPS / ALPHA ** 2)


def _dot(a, b):
    return jnp.dot(a, b, preferred_element_type=F32)


def _swiglu(xb, wg_ref, wu_ref, wd_ref):
    g = _dot(xb, wg_ref[...])
    u = _dot(xb, wu_ref[...])
    a = g * jax.nn.sigmoid(g) * u
    return _dot(a.astype(BF16), wd_ref[...])


def _gelu(x):
    return 0.5 * x * (1.0 + lax.erf(x * math.sqrt(0.5)))


def _resident(shape):
    return pl.BlockSpec(shape, lambda *_: (0,) * len(shape), pipeline_mode=pl.Buffered(1))


def _rows(tm, width):
    return pl.BlockSpec((tm, width), lambda i: (i, 0))


def _residue_major_spec(d, tm, width, tiles_per_seq):
    return pl.BlockSpec((None, d, tm // d, width), lambda i: (i // tiles_per_seq, 0, i % tiles_per_seq, 0))


def _params(n_axes):
    return pltpu.CompilerParams(dimension_semantics=("arbitrary",) * n_axes,
                                vmem_limit_bytes=VMEM_LIMIT_BYTES)


_HBM = pl.BlockSpec(memory_space=pl.ANY)


FILL_SLOTS = 4


def _fill_bf16(w_hbm, w_vmem, stage, sem, rows):
    k, n = w_hbm.shape
    slots = stage.shape[0]
    assert k % rows == 0 and rows == stage.shape[1] and n <= stage.shape[2] and sem.shape[0] == slots
    n_blk = k // rows

    def copy(i):
        dst = stage.at[i % slots, :, pl.ds(0, n)]
        return pltpu.make_async_copy(w_hbm.at[pl.ds(i * rows, rows), :], dst, sem.at[i % slots])

    for i in range(min(slots - 1, n_blk)):
        copy(i).start()
    for i in range(n_blk):
        if i + slots - 1 < n_blk:
            copy(i + slots - 1).start()
        copy(i).wait()
        w_vmem[i * rows:(i + 1) * rows, :] = stage[i % slots, :, 0:n].astype(BF16)


def _weight_scratch(shapes, stage_rows, stage_cols):
    return ([pltpu.VMEM(s, BF16) for s in shapes]
            + [pltpu.VMEM((FILL_SLOTS, stage_rows, stage_cols), F32), pltpu.SemaphoreType.DMA((FILL_SLOTS,))])


BF16_SUBLANES = 16


def _side_cast_spec(shape, n_steps):
    k, n = shape
    n_blocks = n_steps
    while k % n_blocks or (k // n_blocks) % BF16_SUBLANES:
        n_blocks //= 2
    assert n_blocks >= 1 and n_steps % n_blocks == 0
    steps_per_block = n_steps // n_blocks
    return pl.BlockSpec((k // n_blocks, n), lambda i: (i // steps_per_block, 0))


def _side_cast(pairs, col_chunks=()):
    for idx, (src_ref, dst_ref) in enumerate(pairs):
        chunks = col_chunks[idx] if idx < len(col_chunks) and col_chunks[idx] else [(0, src_ref.shape[1])]
        dst0 = 0
        for src0, width in chunks:
            dst_ref[:, dst0:dst0 + width] = src_ref[:, src0:src0 + width].astype(BF16)
            dst0 += width


FFN_FILL_ROWS = 256


def _ffn_ln_kernel(x_ref, wg_hbm, wu_hbm, wd_hbm, g_ref, b_ref, *rest, n_side, side_cols):
    side_in, rest = rest[:n_side], rest[n_side:]
    h_ref, hb_ref = rest[:2]
    side_out, (wg_ref, wu_ref, wd_ref, stage_ref, sem) = rest[2:2 + n_side], rest[2 + n_side:]

    @pl.when(pl.program_id(0) == 0)
    def _():
        for w_hbm, w_ref in ((wg_hbm, wg_ref), (wu_hbm, wu_ref), (wd_hbm, wd_ref)):
            _fill_bf16(w_hbm, w_ref, stage_ref, sem, FFN_FILL_ROWS)

    x = x_ref[...]
    y = _swiglu(x.astype(BF16), wg_ref, wu_ref, wd_ref)
    h = _deepnorm(x, y, 0.5, g_ref[...], b_ref[...])
    h_ref[...] = h
    hb_ref[...] = pltpu.bitcast(h.astype(BF16), jnp.uint32)
    _side_cast(zip(side_in, side_out), side_cols)


def _ffn_ln(x, wg, wu, wd, g, b, next_weights, side_cols, tm):
    t = x.shape[0]
    n_steps = t // tm
    side_specs = [_side_cast_spec(w.shape, n_steps) for w in next_weights]
    return pl.pallas_call(
        functools.partial(_ffn_ln_kernel, n_side=len(next_weights), side_cols=side_cols),
        grid=(n_steps,),
        in_specs=[_rows(tm, D_MODEL), _HBM, _HBM, _HBM, _resident((1, D_MODEL)), _resident((1, D_MODEL))]
                 + side_specs,
        out_specs=[_rows(tm, D_MODEL), _rows(tm // 2, D_MODEL)] + side_specs,
        out_shape=[jax.ShapeDtypeStruct((t, D_MODEL), F32), jax.ShapeDtypeStruct((t // 2, D_MODEL), jnp.uint32)]
                  + [jax.ShapeDtypeStruct(w.shape, BF16) for w in next_weights],
        scratch_shapes=_weight_scratch([(D_MODEL, D_FF), (D_MODEL, D_FF), (D_FF, D_MODEL)], FFN_FILL_ROWS, D_FF),
        compiler_params=_params(1),
        name="ffn_ln",
    )(x, wg, wu, wd, g, b, *next_weights)


def _project_group(hb_ref, w_ref, gi, dst_ref, slot):
    width = 3 * GROUP_WIDTH
    res = _dot(pltpu.bitcast(hb_ref[...], BF16), w_ref[:, gi * width:(gi + 1) * width])
    for i in range(3):
        for j in range(N_SLAB):
            c0 = i * GROUP_WIDTH + j * LANES
            dst_ref[slot + i, j] = res[:, c0:c0 + LANES]


GATHER_STRIDE = 4


def _emit_residue_major(p_ref, out_ref, col0, tmp_ref, cos_ref=None, sin_ref=None):
    d, rows, _ = out_ref.shape
    tm = d * rows
    sources = [p_ref.at[j] for j in range(N_SLAB)] + ([cos_ref, sin_ref] if cos_ref is not None else [])
    two_level = d > GATHER_STRIDE
    if two_level:
        part = tm // GATHER_STRIDE
        for i, src in enumerate(sources):
            for a in range(GATHER_STRIDE):
                tmp_ref[i, a * part:(a + 1) * part, :] = src[pl.ds(a, part, stride=GATHER_STRIDE), :]
        sources = [tmp_ref.at[i] for i in range(len(sources))]
    lane = lax.broadcasted_iota(jnp.int32, (1, LANES), 1)
    first_half = (lane % HEAD_DIM) < (HEAD_DIM // 2)
    for r in range(d):
        if two_level:
            a, b = r % GATHER_STRIDE, r // GATHER_STRIDE
            rs = pl.ds(a * part + b, rows, stride=d // GATHER_STRIDE)
        else:
            rs = pl.ds(r, rows, stride=d) if d > 1 else slice(None)
        if cos_ref is not None:
            c, s = sources[N_SLAB][rs, :], sources[N_SLAB + 1][rs, :]
        blocks = []
        for j in range(N_SLAB):
            x = sources[j][rs, :]
            if cos_ref is not None:
                partner = jnp.where(first_half, pltpu.roll(x, LANES - HEAD_DIM // 2, 1),
                                    pltpu.roll(x, HEAD_DIM // 2, 1))
                x = x * c + partner * s
            blocks.append(x)
        out_ref[r, :, col0:col0 + GROUP_WIDTH] = jnp.concatenate(blocks, axis=1).astype(out_ref.dtype)


def _mixer_in_kernel(hb_ref, pos_ref, invf_ref, w_ref, bg_ref, lng_ref, lnb_ref, ws_ref, bs_ref, wgb_ref,
                     *rest, n_side):
    side_in, rest = rest[:n_side], rest[n_side:]
    qkv_refs, (ga_ref, gm_ref) = rest[:N_ATTN_GROUPS], rest[N_ATTN_GROUPS:N_ATTN_GROUPS + 2]
    side_out = rest[N_ATTN_GROUPS + 2:N_ATTN_GROUPS + 2 + n_side]
    a_ref, g_ref, p_ref, cs_ref, vgn_ref, tmp_ref = rest[N_ATTN_GROUPS + 2 + n_side:]
    tm = 2 * hb_ref.shape[0]
    _side_cast(zip(side_in, side_out))

    def project_group(gi, slot):
        _project_group(hb_ref, w_ref, gi, p_ref, slot)

    def emit_group(gi, slot):
        out_ref = qkv_refs[gi]
        _emit_residue_major(p_ref.at[slot], out_ref, 0, tmp_ref, cs_ref.at[2], cs_ref.at[3])
        _emit_residue_major(p_ref.at[slot + 1], out_ref, GROUP_WIDTH, tmp_ref, cs_ref.at[0], cs_ref.at[1])
        _emit_residue_major(p_ref.at[slot + 2], out_ref, 2 * GROUP_WIDTH, tmp_ref)

    uv = _dot(pltpu.bitcast(hb_ref[...], BF16), w_ref[:, _U0:_U0 + 2 * GMLP_WIDTH])
    a_ref[0] = uv[:, :GMLP_WIDTH]
    a_ref[1] = uv[:, GMLP_WIDTH:]
    n_freq = HEAD_DIM // 2
    per_row = LANES // n_freq
    ang = pos_ref[...].astype(F32) * invf_ref[...]
    lane = lax.broadcasted_iota(jnp.int32, (1, LANES), 1)
    first_half = (lane % HEAD_DIM) < n_freq
    lane_group = lane // n_freq
    for k, packed in enumerate((jnp.cos(ang), jnp.sin(ang))):
        rolled = [packed] + [pltpu.roll(packed, n_freq * s, 1) for s in range(1, per_row)]
        for a in range(per_row):
            spread = rolled[(0 - a) % per_row]
            for b in range(1, per_row):
                spread = jnp.where(lane_group == b, rolled[(b - a) % per_row], spread)
            if k == 1:
                spread = jnp.where(first_half, -spread, spread)
            rows = pl.ds(a, tm // per_row, stride=per_row)
            cs_ref[k, rows, :] = spread
            cs_ref[k + 2, rows, :] = spread * _Q_SCALE

    gates = _dot(pltpu.bitcast(hb_ref[...], BF16), w_ref[:, _GA0:_GA0 + 2 * D_MODEL]) + bg_ref[...]
    g_ref[1] = gates[:, :D_MODEL]
    g_ref[0] = gates[:, D_MODEL:]
    vgn_ref[...] = _layer_norm(_gelu(a_ref[1]), lng_ref[...], lnb_ref[...]).astype(BF16)

    project_group(0, 0)
    n_chunk = tm // GMLP_CHUNK
    gdim = GMLP_WIDTH // GMLP_GROUPS
    row = lax.broadcasted_iota(jnp.int32, (GMLP_CHUNK, GMLP_CHUNK), 0)
    col = lax.broadcasted_iota(jnp.int32, (GMLP_CHUNK, GMLP_CHUNK), 1)
    causal = col <= row
    for g in range(GMLP_GROUPS):
        ws = jnp.where(causal, ws_ref[g], 0.0).astype(BF16)
        rhs = jnp.concatenate(
            [vgn_ref[c * GMLP_CHUNK:(c + 1) * GMLP_CHUNK, g * gdim:(g + 1) * gdim] for c in range(n_chunk)],
            axis=1)
        mixed = _dot(ws, rhs)
        for c in range(n_chunk):
            a_ref[1, c * GMLP_CHUNK:(c + 1) * GMLP_CHUNK, g * gdim:(g + 1) * gdim] = (
                mixed[:, c * gdim:(c + 1) * gdim] + bs_ref[:, g * gdim:(g + 1) * gdim])
    a_ref[0] = _gelu(a_ref[0])

    project_group(1, 3)
    vgn_ref[...] = (a_ref[0] * a_ref[1]).astype(BF16)
    emit_group(0, 0)
    ga_ref[...] = jax.nn.sigmoid(g_ref[1]).astype(BF16)

    project_group(2, 0)
    emit_group(1, 3)
    g_ref[0] = jax.nn.sigmoid(g_ref[0])

    gm_ref[...] = (g_ref[0] * _dot(vgn_ref[...], wgb_ref[...])).astype(BF16)
    emit_group(2, 0)


def _mixer_in(hb, pos, invf, w_in, b_gates, ln_g, ln_b, w_s, b_s, w_gb, next_weights, batch, tm):
    t = 2 * hb.shape[0]
    seq = t // batch
    n_steps = t // tm
    in_width = w_in.shape[1]
    qkv_specs, qkv_shapes = [], []
    for _, d in ATTN_PATTERNS:
        qkv_specs += [_residue_major_spec(d, tm, 3 * GROUP_WIDTH, seq // tm)]
        qkv_shapes += [jax.ShapeDtypeStruct((batch, d, seq // d, 3 * GROUP_WIDTH), BF16)]
    side_specs = [_side_cast_spec(w.shape, n_steps) for w in next_weights]
    return pl.pallas_call(
        functools.partial(_mixer_in_kernel, n_side=len(next_weights)),
        grid=(n_steps,),
        in_specs=[_rows(tm // 2, D_MODEL), _rows(tm * (HEAD_DIM // 2) // LANES, LANES), _resident((1, LANES)),
                  _resident((D_MODEL, in_width)),
                  _resident((1, 2 * D_MODEL)), _resident((1, GMLP_WIDTH)), _resident((1, GMLP_WIDTH)),
                  _resident((GMLP_GROUPS, GMLP_CHUNK, GMLP_CHUNK)), _resident((GMLP_CHUNK, GMLP_WIDTH)),
                  _resident((GMLP_WIDTH, D_MODEL))] + side_specs,
        out_specs=qkv_specs + [_rows(tm, D_MODEL), _rows(tm, D_MODEL)] + side_specs,
        out_shape=qkv_shapes + [jax.ShapeDtypeStruct((t, D_MODEL), BF16)] * 2
                  + [jax.ShapeDtypeStruct(w.shape, BF16) for w in next_weights],
        scratch_shapes=[pltpu.VMEM((2, tm, GMLP_WIDTH), F32), pltpu.VMEM((2, tm, D_MODEL), F32),
                        pltpu.VMEM((6, N_SLAB, tm, LANES), F32), pltpu.VMEM((4, tm, LANES), F32),
                        pltpu.VMEM((tm, GMLP_WIDTH), BF16), pltpu.VMEM((N_SLAB + 2, tm, LANES), F32)],
        compiler_params=_params(1),
        name="mixer_in",
    )(hb, pos, invf, w_in, b_gates, ln_g, ln_b, w_s, b_s, w_gb, *next_weights)


def _attn_kernel(qkv_ref, acc_ref, stat_ref, accf_ref=None, tmp_ref=None, *, n_blk):
    seq = qkv_ref.shape[0]
    n_items = seq // ATTN_BLOCK
    dilation = n_items // n_blk
    two_level = dilation > GATHER_STRIDE
    part = seq // GATHER_STRIDE
    has_prev = n_blk > 1
    row = lax.broadcasted_iota(jnp.int32, (2 * ATTN_BLOCK, ATTN_BLOCK), 0) % ATTN_BLOCK
    col = lax.broadcasted_iota(jnp.int32, (2 * ATTN_BLOCK, ATTN_BLOCK), 1)
    neg_inf = jnp.float32(-jnp.inf)
    bias_cur = jnp.where(col <= row, 0.0, neg_inf)
    bias_prev = jnp.where(col >= row, 0.0, neg_inf)
    lane = lax.broadcasted_iota(jnp.int32, (ATTN_BLOCK, LANES), 1)
    low_head = lane < HEAD_DIM
    n_keys = 2 * ATTN_BLOCK if has_prev else ATTN_BLOCK
    low_head_keys = lax.broadcasted_iota(jnp.int32, (n_keys, LANES), 1) < HEAD_DIM
    contract_last = (((1,), (1,)), ((), ()))

    def item(j, carry):
        r0 = pl.multiple_of(j * ATTN_BLOCK, ATTN_BLOCK)
        n, r = j % n_blk, j // n_blk
        if dilation == 1:
            out_rows = pl.ds(r0, ATTN_BLOCK)
        elif not two_level:
            out_rows = pl.ds(n * (ATTN_BLOCK * dilation) + r, ATTN_BLOCK, stride=dilation)
        else:
            sub = dilation // GATHER_STRIDE
            a, b = r % GATHER_STRIDE, r // GATHER_STRIDE
            out_rows = pl.ds(a * part + n * (ATTN_BLOCK * sub) + b, ATTN_BLOCK, stride=sub)
        if has_prev:
            p0 = pl.multiple_of(jnp.maximum(j - 1, 0) * ATTN_BLOCK, ATTN_BLOCK)
            prev_bias = bias_prev + jnp.where(j % n_blk > 0, 0.0, neg_inf)
        scores = []
        for pair in range(N_SLAB):
            q_sl = slice(pair * LANES, (pair + 1) * LANES)
            k_sl = slice(GROUP_WIDTH + pair * LANES, GROUP_WIDTH + (pair + 1) * LANES)
            q = qkv_ref[pl.ds(r0, ATTN_BLOCK), q_sl]
            zero = jnp.zeros_like(q)
            q2 = jnp.concatenate([jnp.where(low_head, q, zero), jnp.where(low_head, zero, q)], axis=0)
            keys = qkv_ref[pl.ds(r0, ATTN_BLOCK), k_sl]
            if has_prev:
                keys = jnp.concatenate([qkv_ref[pl.ds(p0, ATTN_BLOCK), k_sl], keys], axis=0)
            s = lax.dot_general(q2, keys, contract_last, preferred_element_type=F32)
            if has_prev:
                s = s + jnp.concatenate([prev_bias, bias_cur], axis=1)
            else:
                s = s + bias_cur
            scores.append(s)
        maxes = [jnp.max(s, axis=1, keepdims=True) for s in scores]
        probs = [jnp.exp2(s - m).astype(BF16) for s, m in zip(scores, maxes)]
        stat = jnp.zeros((ATTN_BLOCK, LANES), F32)
        for pair in range(N_SLAB):
            sl = slice(pair * LANES, (pair + 1) * LANES)
            v_sl = slice(2 * GROUP_WIDTH + pair * LANES, 2 * GROUP_WIDTH + (pair + 1) * LANES)
            vals = qkv_ref[pl.ds(r0, ATTN_BLOCK), v_sl]
            if has_prev:
                vals = jnp.concatenate([qkv_ref[pl.ds(p0, ATTN_BLOCK), v_sl], vals], axis=0)
            one = jnp.ones_like(vals)
            r_even = _dot(probs[pair][:ATTN_BLOCK], jnp.where(low_head_keys, vals, one))
            r_odd = _dot(probs[pair][ATTN_BLOCK:], jnp.where(low_head_keys, one, vals))
            numer = jnp.where(low_head, r_even, r_odd)
            if dilation == 1:
                acc_ref[out_rows, sl] = numer.astype(BF16)
            else:
                (tmp_ref if two_level else accf_ref)[pair, out_rows, :] = numer
            m_even, m_odd = maxes[pair][:ATTN_BLOCK], maxes[pair][ATTN_BLOCK:]
            for h, m_h, r_h in ((2 * pair, m_even, r_even), (2 * pair + 1, m_odd, r_odd)):
                stat = jnp.where(lane == _stat_lane(h), m_h, stat)
                stat = jnp.where(lane == _stat_lane(h) + _DEN_SHIFT, r_h, stat)
        if two_level:
            tmp_ref[N_SLAB, out_rows, :] = stat
        else:
            stat_ref[out_rows, :] = stat
        return carry

    lax.fori_loop(0, n_items, item, 0, unroll=n_items)
    if two_level:
        for a in range(GATHER_STRIDE):
            rows = pl.ds(a, part, stride=GATHER_STRIDE)
            for pair in range(N_SLAB):
                accf_ref[pair, rows, :] = tmp_ref[pair, a * part:(a + 1) * part, :]
            stat_ref[rows, :] = tmp_ref[N_SLAB, a * part:(a + 1) * part, :]
    if dilation > 1:
        acc_ref[...] = jnp.concatenate([accf_ref[pair] for pair in range(N_SLAB)], axis=1).astype(BF16)


def _attention(qkv):
    b, d, l, width = qkv.shape
    s = d * l

    def seq(w):
        return pl.BlockSpec((None, s, w), lambda bi: (bi, 0, 0))

    acc, stat = pl.pallas_call(
        functools.partial(_attn_kernel, n_blk=l // ATTN_BLOCK),
        grid=(b,),
        in_specs=[seq(width)],
        out_specs=[seq(GROUP_WIDTH), seq(LANES)],
        out_shape=[jax.ShapeDtypeStruct((b, s, GROUP_WIDTH), BF16), jax.ShapeDtypeStruct((b, s, LANES), F32)],
        scratch_shapes=([pltpu.VMEM((N_SLAB, s, LANES), F32)] if d > 1 else [])
                       + ([pltpu.VMEM((N_SLAB + 1, s, LANES), F32)] if d > GATHER_STRIDE else []),
        compiler_params=_params(1),
        name=f"band_attention_d{d}",
    )(qkv.reshape(b, s, width))
    return acc.reshape(b * s, GROUP_WIDTH), stat.reshape(b * s, LANES)


def _mixer_out_kernel(a0_ref, a1_ref, a2_ref, s0_ref, s1_ref, s2_ref, ga_ref, gm_ref, h1_ref, expand_ref,
                      wab_ref, wout_ref, g2_ref, b2_ref, wg_ref, wu_ref, wd_ref, g3_ref, b3_ref, out_ref):
    stats = [s_ref[...] for s_ref in (s0_ref, s1_ref, s2_ref)]
    accs = [a_ref[...].astype(F32) for a_ref in (a0_ref, a1_ref, a2_ref)]
    lane = lax.broadcasted_iota(jnp.int32, (1, LANES), 1)
    is_max_lane = functools.reduce(jnp.logical_or, [lane == _stat_lane(h) for h in range(HEADS_PER_GROUP)])
    m = jnp.maximum(jnp.maximum(stats[0], stats[1]), stats[2])
    es = [jnp.exp2(x - m) for x in stats]
    dens = [pltpu.roll(x, LANES - _DEN_SHIFT, 1) for x in stats]
    den = es[0] * dens[0] + es[1] * dens[1] + es[2] * dens[2]
    inv_den = 1.0 / jnp.where(is_max_lane, den, 1.0)
    y = None
    for e, acc in zip(es, accs):
        w = e * inv_den
        w_hi = w.astype(BF16)
        w_lo = (w - w_hi.astype(F32)).astype(BF16)
        w_wide = _dot(jnp.concatenate([w_hi, w_lo], axis=1), expand_ref[...])
        term = w_wide * acc
        y = term if y is None else y + term
    branch_a = _dot(y.astype(BF16), wab_ref[...])
    merged = ga_ref[...].astype(F32) * branch_a + gm_ref[...].astype(F32)
    mix = _dot(merged.astype(BF16), wout_ref[...])
    h2 = _deepnorm(h1_ref[...], mix, 1.0, g2_ref[...], b2_ref[...])
    ffn = _swiglu(h2.astype(BF16), wg_ref, wu_ref, wd_ref)
    out_ref[...] = _deepnorm(h2, ffn, 0.5, g3_ref[...], b3_ref[...])


def _mixer_out(accs, stats, ga, gm, h1, expand, wab, wout, g2, b2, wg, wu, wd, g3, b3, tm):
    t = h1.shape[0]
    return pl.pallas_call(
        _mixer_out_kernel,
        grid=(t // tm,),
        in_specs=[_rows(tm, GROUP_WIDTH)] * len(accs) + [_rows(tm, LANES)] * len(stats)
                 + [_rows(tm, D_MODEL)] * 3
                 + [_resident((2 * LANES, GROUP_WIDTH)), _resident((GROUP_WIDTH, D_MODEL)),
                    _resident((D_MODEL, D_MODEL)), _resident((1, D_MODEL)), _resident((1, D_MODEL)),
                    _resident((D_MODEL, D_FF)), _resident((D_MODEL, D_FF)), _resident((D_FF, D_MODEL)),
                    _resident((1, D_MODEL)), _resident((1, D_MODEL))],
        out_specs=_rows(tm, D_MODEL),
        out_shape=jax.ShapeDtypeStruct((t, D_MODEL), F32),
        compiler_params=_params(1),
        name="mixer_out_ffn",
    )(*accs, *stats, ga, gm, h1, expand, wab, wout, g2, b2, wg, wu, wd, g3, b3)


def kernel(x, positions, ffn1_w_gate, ffn1_w_up, ffn1_w_down, ln1_g, ln1_b, w_in, b_gates, gmlp_ln_g, gmlp_ln_b, gmlp_w_s, gmlp_b_s, w_attn_branch, w_gmlp_branch, w_out, ln2_g, ln2_b, ffn2_w_gate, ffn2_w_up, ffn2_w_down, ln3_g, ln3_b):
    b, s, d_model = x.shape
    assert d_model == D_MODEL and ln1_g.shape[0] == DEPTH == 1
    t = b * s
    tm = 512
    assert t % tm == 0 and tm % GMLP_CHUNK == 0 and s % tm == 0

    lane = jnp.arange(LANES) % (HEAD_DIM // 2)
    invf = (ROPE_THETA ** (-(2 * lane).astype(F32) / HEAD_DIM)).reshape(1, LANES)
    stat_lane_of_col = jnp.array([_stat_lane(c // HEAD_DIM) for c in range(GROUP_WIDTH)])
    expand = (jnp.arange(LANES)[:, None] == stat_lane_of_col[None, :]).astype(BF16)
    expand = jnp.concatenate([expand, expand], axis=0)
    b_s = jnp.repeat(gmlp_b_s[0].T, GMLP_WIDTH // GMLP_GROUPS, axis=1)

    h = x.reshape(t, D_MODEL)
    w_in_cols = [(base + gi * GROUP_WIDTH, GROUP_WIDTH) for gi in range(N_ATTN_GROUPS) for base in (_Q0, _K0, _V0)]
    w_in_cols.append((_U0, w_in.shape[-1] - _U0))
    h1, h1b, w_in_b, w_gb_b, wab, wout = _ffn_ln(
        h, ffn1_w_gate[0], ffn1_w_up[0], ffn1_w_down[0], ln1_g, ln1_b,
        (w_in[0], w_gmlp_branch[0], w_attn_branch[0], w_out[0]), (tuple(w_in_cols), None, None, None), tm)
    pos_packed = jnp.repeat(positions.reshape(t * (HEAD_DIM // 2) // LANES, -1), HEAD_DIM // 2, axis=1)
    outs = _mixer_in(h1b, pos_packed, invf, w_in_b, b_gates, gmlp_ln_g, gmlp_ln_b, gmlp_w_s[0], b_s, w_gb_b,
                     (ffn2_w_gate[0], ffn2_w_up[0], ffn2_w_down[0]), b, tm)
    ga, gm = outs[N_ATTN_GROUPS], outs[N_ATTN_GROUPS + 1]
    wg2, wu2, wd2 = outs[N_ATTN_GROUPS + 2:]
    accs, stats = [], []
    for gi in range(N_ATTN_GROUPS):
        acc, stat = _attention(outs[gi])
        accs.append(acc)
        stats.append(stat)
    out = _mixer_out(accs, stats, ga, gm, h1, expand, wab, wout, ln2_g, ln2_b, wg2, wu2, wd2, ln3_g, ln3_b, tm)
    return out.reshape(b, s, D_MODEL)
```

```python
import functools
import math

import jax
import jax.numpy as jnp
from jax import lax
from jax.experimental import pallas as pl
from jax.experimental.pallas import tpu as pltpu

D_MODEL = 1024
HEAD_DIM = 64
HEADS_PER_GROUP = 8
ATTN_PATTERNS = ((128, 1), (512, 4), (2048, 16))
N_ATTN_GROUPS = len(ATTN_PATTERNS)
GROUP_WIDTH = HEADS_PER_GROUP * HEAD_DIM
ATTN_WIDTH = N_ATTN_GROUPS * GROUP_WIDTH
ATTN_BLOCK = 128
ROPE_THETA = 10000.0
GMLP_CHUNK = 128
GMLP_GROUPS = 8
GMLP_WIDTH = D_MODEL
D_FF = 2816
DEPTH = 1
ALPHA = (2 * DEPTH) ** 0.25
LN_EPS = 1e-5

LANES = 128
VMEM_LIMIT_BYTES = 62 * 1024 * 1024
N_SLAB = GROUP_WIDTH // LANES

_Q0, _K0, _V0 = 0, ATTN_WIDTH, 2 * ATTN_WIDTH
_U0 = 3 * ATTN_WIDTH
_VG0 = _U0 + GMLP_WIDTH
_GA0 = _VG0 + GMLP_WIDTH

_Q_SCALE = HEAD_DIM ** -0.5 * math.log2(math.e)

_DEN_SHIFT = 8


def _stat_lane(h):
    return HEAD_DIM + h if h % 2 == 0 else h


BF16 = jnp.bfloat16
F32 = jnp.float32


def _layer_norm(x, g, b, eps=LN_EPS):
    mu = jnp.mean(x, axis=-1, keepdims=True)
    xc = x - mu
    var = jnp.mean(xc * xc, axis=-1, keepdims=True)
    return xc * lax.rsqrt(var + eps) * g + b


def _deepnorm(x, update, scale, g, b):
    return _layer_norm(x + (scale / ALPHA) * update, g, b, LN_EPS / ALPHA ** 2)


def _dot(a, b):
    return jnp.dot(a, b, preferred_element_type=F32)


def _swiglu(xb, wg_ref, wu_ref, wd_ref):
    g = _dot(xb, wg_ref[...])
    u = _dot(xb, wu_ref[...])
    a = g * jax.nn.sigmoid(g) * u
    return _dot(a.astype(BF16), wd_ref[...])


def _gelu(x):
    return 0.5 * x * (1.0 + lax.erf(x * math.sqrt(0.5)))


def _resident(shape):
    return pl.BlockSpec(shape, lambda *_: (0,) * len(shape), pipeline_mode=pl.Buffered(1))


def _rows(tm, width):
    return pl.BlockSpec((tm, width), lambda i: (i, 0))


def _residue_major_spec(d, tm, width, tiles_per_seq):
    return pl.BlockSpec((None, d, tm // d, width), lambda i: (i // tiles_per_seq, 0, i % tiles_per_seq, 0))


def _params(n_axes):
    return pltpu.CompilerParams(dimension_semantics=("arbitrary",) * n_axes,
                                vmem_limit_bytes=VMEM_LIMIT_BYTES)


_HBM = pl.BlockSpec(memory_space=pl.ANY)


FILL_SLOTS = 4


def _fill_bf16(w_hbm, w_vmem, stage, sem, rows):
    k, n = w_hbm.shape
    slots = stage.shape[0]
    assert k % rows == 0 and rows == stage.shape[1] and n <= stage.shape[2] and sem.shape[0] == slots
    n_blk = k // rows

    def copy(i):
        dst = stage.at[i % slots, :, pl.ds(0, n)]
        return pltpu.make_async_copy(w_hbm.at[pl.ds(i * rows, rows), :], dst, sem.at[i % slots])

    for i in range(min(slots - 1, n_blk)):
        copy(i).start()
    for i in range(n_blk):
        if i + slots - 1 < n_blk:
            copy(i + slots - 1).start()
        copy(i).wait()
        w_vmem[i * rows:(i + 1) * rows, :] = stage[i % slots, :, 0:n].astype(BF16)


def _weight_scratch(shapes, stage_rows, stage_cols):
    return ([pltpu.VMEM(s, BF16) for s in shapes]
            + [pltpu.VMEM((FILL_SLOTS, stage_rows, stage_cols), F32), pltpu.SemaphoreType.DMA((FILL_SLOTS,))])


BF16_SUBLANES = 16


def _side_cast_spec(shape, n_steps):
    k, n = shape
    n_blocks = n_steps
    while k % n_blocks or (k // n_blocks) % BF16_SUBLANES:
        n_blocks //= 2
    assert n_blocks >= 1 and n_steps % n_blocks == 0
    steps_per_block = n_steps // n_blocks
    return pl.BlockSpec((k // n_blocks, n), lambda i: (i // steps_per_block, 0))


def _side_cast(pairs, col_chunks=()):
    for idx, (src_ref, dst_ref) in enumerate(pairs):
        chunks = col_chunks[idx] if idx < len(col_chunks) and col_chunks[idx] else [(0, src_ref.shape[1])]
        dst0 = 0
        for src0, width in chunks:
            dst_ref[:, dst0:dst0 + width] = src_ref[:, src0:src0 + width].astype(BF16)
            dst0 += width


FFN_FILL_ROWS = 64
FFN1_ROWS = 1024


def _ffn_ln_kernel(x_ref, wg_hbm, wu_hbm, wd_hbm, g_ref, b_ref, *rest, n_side, side_cols):
    side_in, rest = rest[:n_side], rest[n_side:]
    h_ref, hb_ref = rest[:2]
    side_out, (wg_ref, wu_ref, wd_ref, stage_ref, sem) = rest[2:2 + n_side], rest[2 + n_side:]

    @pl.when(pl.program_id(0) == 0)
    def _():
        for w_hbm, w_ref in ((wg_hbm, wg_ref), (wu_hbm, wu_ref), (wd_hbm, wd_ref)):
            _fill_bf16(w_hbm, w_ref, stage_ref, sem, FFN_FILL_ROWS)

    x = x_ref[...]
    y = _swiglu(x.astype(BF16), wg_ref, wu_ref, wd_ref)
    h = _deepnorm(x, y, 0.5, g_ref[...], b_ref[...])
    h_ref[...] = h
    hb_ref[...] = pltpu.bitcast(h.astype(BF16), jnp.uint32)
    _side_cast(zip(side_in, side_out), side_cols)


def _ffn_ln(x, wg, wu, wd, g, b, next_weights, side_cols, tm):
    t = x.shape[0]
    n_steps = t // tm
    side_specs = [_side_cast_spec(w.shape, n_steps) for w in next_weights]
    return pl.pallas_call(
        functools.partial(_ffn_ln_kernel, n_side=len(next_weights), side_cols=side_cols),
        grid=(n_steps,),
        in_specs=[_rows(tm, D_MODEL), _HBM, _HBM, _HBM, _resident((1, D_MODEL)), _resident((1, D_MODEL))]
                 + side_specs,
        out_specs=[_rows(tm, D_MODEL), _rows(tm // 2, D_MODEL)] + side_specs,
        out_shape=[jax.ShapeDtypeStruct((t, D_MODEL), F32), jax.ShapeDtypeStruct((t // 2, D_MODEL), jnp.uint32)]
                  + [jax.ShapeDtypeStruct(w.shape, BF16) for w in next_weights],
        scratch_shapes=_weight_scratch([(D_MODEL, D_FF), (D_MODEL, D_FF), (D_FF, D_MODEL)], FFN_FILL_ROWS, D_FF),
        compiler_params=_params(1),
        name="ffn_ln",
    )(x, wg, wu, wd, g, b, *next_weights)


def _project_group(hb_ref, w_ref, gi, dst_ref, slot):
    width = 3 * GROUP_WIDTH
    res = _dot(pltpu.bitcast(hb_ref[...], BF16), w_ref[:, gi * width:(gi + 1) * width])
    for i in range(3):
        for j in range(N_SLAB):
            c0 = i * GROUP_WIDTH + j * LANES
            dst_ref[slot + i, j] = res[:, c0:c0 + LANES]


GATHER_STRIDE = 4


def _emit_residue_major(p_ref, out_ref, col0, tmp_ref, cos_ref=None, sin_ref=None):
    d, rows, _ = out_ref.shape
    tm = d * rows
    sources = [p_ref.at[j] for j in range(N_SLAB)] + ([cos_ref, sin_ref] if cos_ref is not None else [])
    two_level = d > GATHER_STRIDE
    if two_level:
        part = tm // GATHER_STRIDE
        for i, src in enumerate(sources):
            for a in range(GATHER_STRIDE):
                tmp_ref[i, a * part:(a + 1) * part, :] = src[pl.ds(a, part, stride=GATHER_STRIDE), :]
        sources = [tmp_ref.at[i] for i in range(len(sources))]
    lane = lax.broadcasted_iota(jnp.int32, (1, LANES), 1)
    first_half = (lane % HEAD_DIM) < (HEAD_DIM // 2)
    for r in range(d):
        if two_level:
            a, b = r % GATHER_STRIDE, r // GATHER_STRIDE
            rs = pl.ds(a * part + b, rows, stride=d // GATHER_STRIDE)
        else:
            rs = pl.ds(r, rows, stride=d) if d > 1 else slice(None)
        if cos_ref is not None:
            c, s = sources[N_SLAB][rs, :], sources[N_SLAB + 1][rs, :]
        blocks = []
        for j in range(N_SLAB):
            x = sources[j][rs, :]
            if cos_ref is not None:
                partner = jnp.where(first_half, pltpu.roll(x, LANES - HEAD_DIM // 2, 1),
                                    pltpu.roll(x, HEAD_DIM // 2, 1))
                x = x * c + partner * s
            blocks.append(x)
        out_ref[r, :, col0:col0 + GROUP_WIDTH] = jnp.concatenate(blocks, axis=1).astype(out_ref.dtype)


def _mixer_in_kernel(hb_ref, pos_ref, invf_ref, w_ref, bg_ref, lng_ref, lnb_ref, ws_ref, bs_ref, wgb_ref,
                     *rest, n_side):
    side_in, rest = rest[:n_side], rest[n_side:]
    qkv_refs, (ga_ref, gm_ref) = rest[:N_ATTN_GROUPS], rest[N_ATTN_GROUPS:N_ATTN_GROUPS + 2]
    side_out = rest[N_ATTN_GROUPS + 2:N_ATTN_GROUPS + 2 + n_side]
    a_ref, g_ref, p_ref, cs_ref, vgn_ref, tmp_ref = rest[N_ATTN_GROUPS + 2 + n_side:]
    tm = 2 * hb_ref.shape[0]
    _side_cast(zip(side_in, side_out))

    def project_group(gi, slot):
        _project_group(hb_ref, w_ref, gi, p_ref, slot)

    def emit_group(gi, slot):
        out_ref = qkv_refs[gi]
        _emit_residue_major(p_ref.at[slot], out_ref, 0, tmp_ref, cs_ref.at[2], cs_ref.at[3])
        _emit_residue_major(p_ref.at[slot + 1], out_ref, GROUP_WIDTH, tmp_ref, cs_ref.at[0], cs_ref.at[1])
        _emit_residue_major(p_ref.at[slot + 2], out_ref, 2 * GROUP_WIDTH, tmp_ref)

    uv = _dot(pltpu.bitcast(hb_ref[...], BF16), w_ref[:, _U0:_U0 + 2 * GMLP_WIDTH])
    a_ref[0] = uv[:, :GMLP_WIDTH]
    a_ref[1] = uv[:, GMLP_WIDTH:]
    n_freq = HEAD_DIM // 2
    per_row = LANES // n_freq
    ang = pos_ref[...].astype(F32) * invf_ref[...]
    lane = lax.broadcasted_iota(jnp.int32, (1, LANES), 1)
    first_half = (lane % HEAD_DIM) < n_freq
    lane_group = lane // n_freq
    for k, packed in enumerate((jnp.cos(ang), jnp.sin(ang))):
        rolled = [packed] + [pltpu.roll(packed, n_freq * s, 1) for s in range(1, per_row)]
        for a in range(per_row):
            spread = rolled[(0 - a) % per_row]
            for b in range(1, per_row):
                spread = jnp.where(lane_group == b, rolled[(b - a) % per_row], spread)
            if k == 1:
                spread = jnp.where(first_half, -spread, spread)
            rows = pl.ds(a, tm // per_row, stride=per_row)
            cs_ref[k, rows, :] = spread
            cs_ref[k + 2, rows, :] = spread * _Q_SCALE

    gates = _dot(pltpu.bitcast(hb_ref[...], BF16), w_ref[:, _GA0:_GA0 + 2 * D_MODEL]) + bg_ref[...]
    g_ref[1] = gates[:, :D_MODEL]
    g_ref[0] = gates[:, D_MODEL:]
    vgn_ref[...] = _layer_norm(_gelu(a_ref[1]), lng_ref[...], lnb_ref[...]).astype(BF16)

    project_group(0, 0)
    n_chunk = tm // GMLP_CHUNK
    gdim = GMLP_WIDTH // GMLP_GROUPS
    row = lax.broadcasted_iota(jnp.int32, (GMLP_CHUNK, GMLP_CHUNK), 0)
    col = lax.broadcasted_iota(jnp.int32, (GMLP_CHUNK, GMLP_CHUNK), 1)
    causal = col <= row
    for g in range(GMLP_GROUPS):
        ws = jnp.where(causal, ws_ref[g], 0.0).astype(BF16)
        rhs = jnp.concatenate(
            [vgn_ref[c * GMLP_CHUNK:(c + 1) * GMLP_CHUNK, g * gdim:(g + 1) * gdim] for c in range(n_chunk)],
            axis=1)
        mixed = _dot(ws, rhs)
        for c in range(n_chunk):
            a_ref[1, c * GMLP_CHUNK:(c + 1) * GMLP_CHUNK, g * gdim:(g + 1) * gdim] = (
                mixed[:, c * gdim:(c + 1) * gdim] + bs_ref[:, g * gdim:(g + 1) * gdim])
    a_ref[0] = _gelu(a_ref[0])

    project_group(1, 3)
    vgn_ref[...] = (a_ref[0] * a_ref[1]).astype(BF16)
    emit_group(0, 0)
    ga_ref[...] = jax.nn.sigmoid(g_ref[1]).astype(BF16)

    project_group(2, 0)
    emit_group(1, 3)
    g_ref[0] = jax.nn.sigmoid(g_ref[0])

    gm_ref[...] = (g_ref[0] * _dot(vgn_ref[...], wgb_ref[...])).astype(BF16)
    emit_group(2, 0)


def _mixer_in(hb, pos, invf, w_in, b_gates, ln_g, ln_b, w_s, b_s, w_gb, next_weights, batch, tm):
    t = 2 * hb.shape[0]
    seq = t // batch
    n_steps = t // tm
    in_width = w_in.shape[1]
    qkv_specs, qkv_shapes = [], []
    for _, d in ATTN_PATTERNS:
        qkv_specs += [_residue_major_spec(d, tm, 3 * GROUP_WIDTH, seq // tm)]
        qkv_shapes += [jax.ShapeDtypeStruct((batch, d, seq // d, 3 * GROUP_WIDTH), BF16)]
    side_specs = [_side_cast_spec(w.shape, n_steps) for w in next_weights]
    return pl.pallas_call(
        functools.partial(_mixer_in_kernel, n_side=len(next_weights)),
        grid=(n_steps,),
        in_specs=[_rows(tm // 2, D_MODEL), _rows(tm * (HEAD_DIM // 2) // LANES, LANES), _resident((1, LANES)),
                  _resident((D_MODEL, in_width)),
                  _resident((1, 2 * D_MODEL)), _resident((1, GMLP_WIDTH)), _resident((1, GMLP_WIDTH)),
                  _resident((GMLP_GROUPS, GMLP_CHUNK, GMLP_CHUNK)), _resident((GMLP_CHUNK, GMLP_WIDTH)),
                  _resident((GMLP_WIDTH, D_MODEL))] + side_specs,
        out_specs=qkv_specs + [_rows(tm, D_MODEL), _rows(tm, D_MODEL)] + side_specs,
        out_shape=qkv_shapes + [jax.ShapeDtypeStruct((t, D_MODEL), BF16)] * 2
                  + [jax.ShapeDtypeStruct(w.shape, BF16) for w in next_weights],
        scratch_shapes=[pltpu.VMEM((2, tm, GMLP_WIDTH), F32), pltpu.VMEM((2, tm, D_MODEL), F32),
                        pltpu.VMEM((6, N_SLAB, tm, LANES), F32), pltpu.VMEM((4, tm, LANES), F32),
                        pltpu.VMEM((tm, GMLP_WIDTH), BF16), pltpu.VMEM((N_SLAB + 2, tm, LANES), F32)],
        compiler_params=_params(1),
        name="mixer_in",
    )(hb, pos, invf, w_in, b_gates, ln_g, ln_b, w_s, b_s, w_gb, *next_weights)


def _attn_kernel(qkv_ref, acc_ref, stat_ref, accf_ref=None, tmp_ref=None, *, n_blk):
    seq = qkv_ref.shape[0]
    n_items = seq // ATTN_BLOCK
    dilation = n_items // n_blk
    two_level = dilation > GATHER_STRIDE
    part = seq // GATHER_STRIDE
    has_prev = n_blk > 1
    row = lax.broadcasted_iota(jnp.int32, (2 * ATTN_BLOCK, ATTN_BLOCK), 0) % ATTN_BLOCK
    col = lax.broadcasted_iota(jnp.int32, (2 * ATTN_BLOCK, ATTN_BLOCK), 1)
    neg_inf = jnp.float32(-jnp.inf)
    bias_cur = jnp.where(col <= row, 0.0, neg_inf)
    bias_prev = jnp.where(col >= row, 0.0, neg_inf)
    lane = lax.broadcasted_iota(jnp.int32, (ATTN_BLOCK, LANES), 1)
    low_head = lane < HEAD_DIM
    n_keys = 2 * ATTN_BLOCK if has_prev else ATTN_BLOCK
    low_head_keys = lax.broadcasted_iota(jnp.int32, (n_keys, LANES), 1) < HEAD_DIM
    contract_last = (((1,), (1,)), ((), ()))

    def item(j, carry):
        r0 = pl.multiple_of(j * ATTN_BLOCK, ATTN_BLOCK)
        n, r = j % n_blk, j // n_blk
        if dilation == 1:
            out_rows = pl.ds(r0, ATTN_BLOCK)
        elif not two_level:
            out_rows = pl.ds(n * (ATTN_BLOCK * dilation) + r, ATTN_BLOCK, stride=dilation)
        else:
            sub = dilation // GATHER_STRIDE
            a, b = r % GATHER_STRIDE, r // GATHER_STRIDE
            out_rows = pl.ds(a * part + n * (ATTN_BLOCK * sub) + b, ATTN_BLOCK, stride=sub)
        if has_prev:
            p0 = pl.multiple_of(jnp.maximum(j - 1, 0) * ATTN_BLOCK, ATTN_BLOCK)
            prev_bias = bias_prev + jnp.where(j % n_blk > 0, 0.0, neg_inf)
        scores = []
        for pair in range(N_SLAB):
            q_sl = slice(pair * LANES, (pair + 1) * LANES)
            k_sl = slice(GROUP_WIDTH + pair * LANES, GROUP_WIDTH + (pair + 1) * LANES)
            q = qkv_ref[pl.ds(r0, ATTN_BLOCK), q_sl]
            zero = jnp.zeros_like(q)
            q2 = jnp.concatenate([jnp.where(low_head, q, zero), jnp.where(low_head, zero, q)], axis=0)
            keys = qkv_ref[pl.ds(r0, ATTN_BLOCK), k_sl]
            if has_prev:
                keys = jnp.concatenate([qkv_ref[pl.ds(p0, ATTN_BLOCK), k_sl], keys], axis=0)
            s = lax.dot_general(q2, keys, contract_last, preferred_element_type=F32)
            if has_prev:
                s = s + jnp.concatenate([prev_bias, bias_cur], axis=1)
            else:
                s = s + bias_cur
            scores.append(s)
        maxes = [jnp.max(s, axis=1, keepdims=True) for s in scores]
        probs = [jnp.exp2(s - m).astype(BF16) for s, m in zip(scores, maxes)]
        stat = jnp.zeros((ATTN_BLOCK, LANES), F32)
        for pair in range(N_SLAB):
            sl = slice(pair * LANES, (pair + 1) * LANES)
            v_sl = slice(2 * GROUP_WIDTH + pair * LANES, 2 * GROUP_WIDTH + (pair + 1) * LANES)
            vals = qkv_ref[pl.ds(r0, ATTN_BLOCK), v_sl]
            if has_prev:
                vals = jnp.concatenate([qkv_ref[pl.ds(p0, ATTN_BLOCK), v_sl], vals], axis=0)
            one = jnp.ones_like(vals)
            r_even = _dot(probs[pair][:ATTN_BLOCK], jnp.where(low_head_keys, vals, one))
            r_odd = _dot(probs[pair][ATTN_BLOCK:], jnp.where(low_head_keys, one, vals))
            numer = jnp.where(low_head, r_even, r_odd)
            if dilation == 1:
                acc_ref[out_rows, sl] = numer.astype(BF16)
            else:
                (tmp_ref if two_level else accf_ref)[pair, out_rows, :] = numer
            m_even, m_odd = maxes[pair][:ATTN_BLOCK], maxes[pair][ATTN_BLOCK:]
            for h, m_h, r_h in ((2 * pair, m_even, r_even), (2 * pair + 1, m_odd, r_odd)):
                stat = jnp.where(lane == _stat_lane(h), m_h, stat)
                stat = jnp.where(lane == _stat_lane(h) + _DEN_SHIFT, r_h, stat)
        if two_level:
            tmp_ref[N_SLAB, out_rows, :] = stat
        else:
            stat_ref[out_rows, :] = stat
        return carry

    lax.fori_loop(0, n_items, item, 0, unroll=n_items)
    if two_level:
        for a in range(GATHER_STRIDE):
            rows = pl.ds(a, part, stride=GATHER_STRIDE)
            for pair in range(N_SLAB):
                accf_ref[pair, rows, :] = tmp_ref[pair, a * part:(a + 1) * part, :]
            stat_ref[rows, :] = tmp_ref[N_SLAB, a * part:(a + 1) * part, :]
    if dilation > 1:
        acc_ref[...] = jnp.concatenate([accf_ref[pair] for pair in range(N_SLAB)], axis=1).astype(BF16)


def _attention(qkv):
    b, d, l, width = qkv.shape
    s = d * l

    def seq(w):
        return pl.BlockSpec((None, s, w), lambda bi: (bi, 0, 0))

    acc, stat = pl.pallas_call(
        functools.partial(_attn_kernel, n_blk=l // ATTN_BLOCK),
        grid=(b,),
        in_specs=[seq(width)],
        out_specs=[seq(GROUP_WIDTH), seq(LANES)],
        out_shape=[jax.ShapeDtypeStruct((b, s, GROUP_WIDTH), BF16), jax.ShapeDtypeStruct((b, s, LANES), F32)],
        scratch_shapes=([pltpu.VMEM((N_SLAB, s, LANES), F32)] if d > 1 else [])
                       + ([pltpu.VMEM((N_SLAB + 1, s, LANES), F32)] if d > GATHER_STRIDE else []),
        compiler_params=_params(1),
        name=f"band_attention_d{d}",
    )(qkv.reshape(b, s, width))
    return acc.reshape(b * s, GROUP_WIDTH), stat.reshape(b * s, LANES)


def _mixer_out_kernel(a0_ref, a1_ref, a2_ref, s0_ref, s1_ref, s2_ref, ga_ref, gm_ref, h1_ref, expand_ref,
                      wab_ref, wout_ref, g2_ref, b2_ref, wg_ref, wu_ref, wd_ref, g3_ref, b3_ref, out_ref):
    stats = [s_ref[...] for s_ref in (s0_ref, s1_ref, s2_ref)]
    accs = [a_ref[...].astype(F32) for a_ref in (a0_ref, a1_ref, a2_ref)]
    lane = lax.broadcasted_iota(jnp.int32, (1, LANES), 1)
    is_max_lane = functools.reduce(jnp.logical_or, [lane == _stat_lane(h) for h in range(HEADS_PER_GROUP)])
    m = jnp.maximum(jnp.maximum(stats[0], stats[1]), stats[2])
    es = [jnp.exp2(x - m) for x in stats]
    dens = [pltpu.roll(x, LANES - _DEN_SHIFT, 1) for x in stats]
    den = es[0] * dens[0] + es[1] * dens[1] + es[2] * dens[2]
    inv_den = 1.0 / jnp.where(is_max_lane, den, 1.0)
    y = None
    for e, acc in zip(es, accs):
        w = e * inv_den
        w_hi = w.astype(BF16)
        w_lo = (w - w_hi.astype(F32)).astype(BF16)
        w_wide = _dot(jnp.concatenate([w_hi, w_lo], axis=1), expand_ref[...])
        term = w_wide * acc
        y = term if y is None else y + term
    branch_a = _dot(y.astype(BF16), wab_ref[...])
    merged = ga_ref[...].astype(F32) * branch_a + gm_ref[...].astype(F32)
    mix = _dot(merged.astype(BF16), wout_ref[...])
    h2 = _deepnorm(h1_ref[...], mix, 1.0, g2_ref[...], b2_ref[...])
    ffn = _swiglu(h2.astype(BF16), wg_ref, wu_ref, wd_ref)
    out_ref[...] = _deepnorm(h2, ffn, 0.5, g3_ref[...], b3_ref[...])


def _mixer_out(accs, stats, ga, gm, h1, expand, wab, wout, g2, b2, wg, wu, wd, g3, b3, tm):
    t = h1.shape[0]
    return pl.pallas_call(
        _mixer_out_kernel,
        grid=(t // tm,),
        in_specs=[_rows(tm, GROUP_WIDTH)] * len(accs) + [_rows(tm, LANES)] * len(stats)
                 + [_rows(tm, D_MODEL)] * 3
                 + [_resident((2 * LANES, GROUP_WIDTH)), _resident((GROUP_WIDTH, D_MODEL)),
                    _resident((D_MODEL, D_MODEL)), _resident((1, D_MODEL)), _resident((1, D_MODEL)),
                    _resident((D_MODEL, D_FF)), _resident((D_MODEL, D_FF)), _resident((D_FF, D_MODEL)),
                    _resident((1, D_MODEL)), _resident((1, D_MODEL))],
        out_specs=_rows(tm, D_MODEL),
        out_shape=jax.ShapeDtypeStruct((t, D_MODEL), F32),
        compiler_params=_params(1),
        name="mixer_out_ffn",
    )(*accs, *stats, ga, gm, h1, expand, wab, wout, g2, b2, wg, wu, wd, g3, b3)


def kernel(x, positions, ffn1_w_gate, ffn1_w_up, ffn1_w_down, ln1_g, ln1_b, w_in, b_gates, gmlp_ln_g, gmlp_ln_b, gmlp_w_s, gmlp_b_s, w_attn_branch, w_gmlp_branch, w_out, ln2_g, ln2_b, ffn2_w_gate, ffn2_w_up, ffn2_w_down, ln3_g, ln3_b):
    b, s, d_model = x.shape
    assert d_model == D_MODEL and ln1_g.shape[0] == DEPTH == 1
    t = b * s
    tm = 512
    assert t % tm == 0 and tm % GMLP_CHUNK == 0 and s % tm == 0

    lane = jnp.arange(LANES) % (HEAD_DIM // 2)
    invf = (ROPE_THETA ** (-(2 * lane).astype(F32) / HEAD_DIM)).reshape(1, LANES)
    stat_lane_of_col = jnp.array([_stat_lane(c // HEAD_DIM) for c in range(GROUP_WIDTH)])
    expand = (jnp.arange(LANES)[:, None] == stat_lane_of_col[None, :]).astype(BF16)
    expand = jnp.concatenate([expand, expand], axis=0)
    b_s = jnp.repeat(gmlp_b_s[0].T, GMLP_WIDTH // GMLP_GROUPS, axis=1)

    h = x.reshape(t, D_MODEL)
    w_in_cols = [(base + gi * GROUP_WIDTH, GROUP_WIDTH) for gi in range(N_ATTN_GROUPS) for base in (_Q0, _K0, _V0)]
    w_in_cols.append((_U0, w_in.shape[-1] - _U0))
    h1, h1b, w_in_b, w_gb_b = _ffn_ln(h, ffn1_w_gate[0], ffn1_w_up[0], ffn1_w_down[0], ln1_g, ln1_b,
                                      (w_in[0], w_gmlp_branch[0]), (tuple(w_in_cols), None), FFN1_ROWS)
    pos_packed = jnp.repeat(positions.reshape(t * (HEAD_DIM // 2) // LANES, -1), HEAD_DIM // 2, axis=1)
    outs = _mixer_in(h1b, pos_packed, invf, w_in_b, b_gates, gmlp_ln_g, gmlp_ln_b, gmlp_w_s[0], b_s, w_gb_b,
                     (w_attn_branch[0], w_out[0], ffn2_w_gate[0], ffn2_w_up[0], ffn2_w_down[0]), b, tm)
    ga, gm = outs[N_ATTN_GROUPS], outs[N_ATTN_GROUPS + 1]
    out_weights = outs[N_ATTN_GROUPS + 2:]
    accs, stats = [], []
    for gi in range(N_ATTN_GROUPS):
        acc, stat = _attention(outs[gi])
        accs.append(acc)
        stats.append(stat)
    wab, wout, wg2, wu2, wd2 = out_weights
    out = _mixer_out(accs, stats, ga, gm, h1, expand, wab, wout, ln2_g, ln2_b, wg2, wu2, wd2, ln3_g, ln3_b, tm)
    return out.reshape(b, s, D_MODEL)
```
